```python
import math
import jax, jax.numpy as jnp
from jax import lax
import numpy as np

D_MODEL = 1024
BATCH = 2
SEQ = 8192
DEPTH = 1

GLA_HEADS = 4
GLA_DK = 128
GLA_DV = 256
GLA_RANK = 16
GLA_TAU = 16.0
GLA_CHUNK = 64
GLA_K_W = GLA_HEADS * GLA_DK
GLA_V_W = GLA_HEADS * GLA_DV
DIFF_HEADS = 8
DIFF_DH = 64
DIFF_QK_W = DIFF_HEADS * 2 * DIFF_DH
DIFF_V_W = DIFF_HEADS * 2 * DIFF_DH
ROT_DIM = DIFF_DH // 4
ROPE_THETA = 500000.0
Q_BLOCK = 128
N_EXPERTS = 32
TOP_K = 4
D_FF = 1024
SWIGLU_LIMIT = 7.0
SWIGLU_ALPHA = 1.702
MOE_BLOCK = 128
DN_ALPHA = (2 * DEPTH) ** 0.25
DN_BETA = (8 * DEPTH) ** -0.25
EPS = 1e-5

IN_SPLITS = (GLA_K_W, GLA_K_W, GLA_V_W, GLA_V_W, GLA_RANK, GLA_RANK,
             DIFF_QK_W, DIFF_QK_W, DIFF_V_W, D_MODEL, D_MODEL)
VALUE_BLOCKS = (2, 8)
N_IN = sum(IN_SPLITS)

kernel_name = "hybrid_gla_diffattn_moe_deepnorm"


def _layer_norm(x, g, b):
    xf = x.astype(jnp.float32)
    mu = jnp.mean(xf, -1, keepdims=True)
    var = jnp.mean(jnp.square(xf - mu), -1, keepdims=True)
    return ((xf - mu) * lax.rsqrt(var + EPS) * g + b).astype(x.dtype)


def _rms_norm(x, g):
    xf = x.astype(jnp.float32)
    return (xf * lax.rsqrt(jnp.mean(jnp.square(xf), -1, keepdims=True) + EPS) * g).astype(x.dtype)


def _apply_partial_rope(t, cos, sin):
    half = ROT_DIM // 2
    x1 = t[..., :half].astype(jnp.float32)
    x2 = t[..., half:ROT_DIM].astype(jnp.float32)
    rot = jnp.concatenate([x1 * cos - x2 * sin, x2 * cos + x1 * sin], -1).astype(t.dtype)
    return jnp.concatenate([rot, t[..., ROT_DIM:]], -1)


def _gla_chunked(q, k, v, logf):
    B, S, H, DK = q.shape
    DV = v.shape[-1]
    N, C = S // GLA_CHUNK, GLA_CHUNK

    def blk(t):
        return t.astype(jnp.float32).reshape(B, N, C, H, t.shape[-1]).transpose(0, 3, 1, 2, 4)

    q, k, v, logf = blk(q), blk(k), blk(v), blk(logf)
    b = jnp.cumsum(logf, axis=3)
    b_last = b[:, :, :, -1:, :]
    q_in = q * jnp.exp(b)
    k_in = k * jnp.exp(-b)
    k_st = k * jnp.exp(b_last - b)
    mask = jnp.tril(jnp.ones((C, C), dtype=bool))
    att = jnp.einsum('bhncd,bhnmd->bhncm', q_in, k_in)
    att = jnp.where(mask, att, 0.0)
    o_intra = jnp.einsum('bhncm,bhnme->bhnce', att, v)
    u = jnp.einsum('bhncd,bhnce->bhnde', k_st, v)
    decay = jnp.exp(b_last[:, :, :, 0, :])

    def step(state, inp):
        dec, un = inp
        return dec[..., None] * state + un, state

    _, s_prev = lax.scan(step, jnp.zeros((B, H, DK, DV), jnp.float32),
                         (decay.transpose(2, 0, 1, 3), u.transpose(2, 0, 1, 3, 4)))
    o_inter = jnp.einsum('bhncd,nbhde->bhnce', q_in, s_prev)
    o = o_intra + o_inter
    return o.transpose(0, 2, 3, 1, 4).reshape(B, S, H, DV)


def _diff_attention(q, k, v, lam):
    B, S, H, _, DH = q.shape
    nq = S // Q_BLOCK
    scale = DH ** -0.5
    qb = q.reshape(B, nq, Q_BLOCK, H, 2, DH).transpose(1, 0, 2, 3, 4, 5)

    def one_block(qi):
        s = jnp.einsum('bqhpd,bkhpd->bhpqk', qi, k).astype(jnp.float32) * scale
        p = jax.nn.softmax(s, axis=-1)
        pw = p[:, :, 0] - lam * p[:, :, 1]
        return jnp.einsum('bhqk,bkhe->bqhe', pw.astype(v.dtype), v)

    o = lax.map(one_block, qb)
    return o.transpose(1, 0, 2, 3, 4).reshape(B, S, H, 2 * DH)


def _token_mixer(h, w_in, wf_fwd, bf_fwd, wf_bwd, bf_bwd, gla_norm_g,
                 lq1, lk1, lq2, lk2, diff_norm_g, w_br_gla, w_br_diff, w_out,
                 cos, sin, lam_init):
    B, S, D = h.shape
    proj = h @ w_in
    cuts = np.cumsum(IN_SPLITS)[:-1].tolist()
    (qg, kg, vg, rg, lr_f, lr_b, qd, kd, vd, gate_g, gate_d) = jnp.split(proj, cuts, axis=-1)

    qg = qg.reshape(B, S, GLA_HEADS, GLA_DK) * (GLA_DK ** -0.5)
    kg = kg.reshape(B, S, GLA_HEADS, GLA_DK)
    vg = vg.reshape(B, S, GLA_HEADS, GLA_DV)
    logf_f = (jax.nn.log_sigmoid((lr_f @ wf_fwd + bf_fwd).astype(jnp.float32)) / GLA_TAU
              ).reshape(B, S, GLA_HEADS, GLA_DK)
    logf_b = (jax.nn.log_sigmoid((lr_b @ wf_bwd + bf_bwd).astype(jnp.float32)) / GLA_TAU
              ).reshape(B, S, GLA_HEADS, GLA_DK)
    o_f = _gla_chunked(qg, kg, vg, logf_f)
    flip = lambda t: jnp.flip(t, axis=1)
    o_b = flip(_gla_chunked(flip(qg), flip(kg), flip(vg), flip(logf_b)))
    o_gla = _rms_norm(o_f + o_b, gla_norm_g).astype(h.dtype).reshape(B, S, GLA_V_W)
    o_gla = o_gla * jax.nn.silu(rg)

    qd = _apply_partial_rope(qd.reshape(B, S, DIFF_HEADS, 2, DIFF_DH), cos, sin)
    kd = _apply_partial_rope(kd.reshape(B, S, DIFF_HEADS, 2, DIFF_DH), cos, sin)
    vd = vd.reshape(B, S, DIFF_HEADS, 2 * DIFF_DH)
    lam = (jnp.exp(jnp.sum(lq1.astype(jnp.float32) * lk1.astype(jnp.float32)))
           - jnp.exp(jnp.sum(lq2.astype(jnp.float32) * lk2.astype(jnp.float32))) + lam_init)
    o_diff = _diff_attention(qd, kd, vd, lam)
    o_diff = (_rms_norm(o_diff, diff_norm_g) * (1.0 - lam_init)).reshape(B, S, DIFF_V_W)

    mixed = (jax.nn.sigmoid(gate_g) * (o_gla @ w_br_gla)
             + jax.nn.sigmoid(gate_d) * (o_diff @ w_br_diff))
    return mixed @ w_out


def _moe(h, router_w, router_b, w_gate, b_gate, w_up, b_up, w_down, b_down):
    B, S, D = h.shape
    T = B * S
    A = T * TOP_K
    xf = h.reshape(T, D)
    logits = (xf @ router_w + router_b).astype(jnp.float32)
    top_v, top_i = lax.top_k(logits, TOP_K)
    gates = jax.nn.softmax(top_v, axis=-1)

    e_flat = top_i.reshape(A).astype(jnp.int32)
    tok = jnp.arange(A, dtype=jnp.int32) // TOP_K
    w_flat = gates.reshape(A)
    order = jnp.argsort(e_flat)
    e_sorted = e_flat[order]
    counts = jnp.bincount(e_flat, length=N_EXPERTS)
    padded = ((counts + MOE_BLOCK - 1) // MOE_BLOCK) * MOE_BLOCK
    pend = jnp.cumsum(padded)
    pstart = pend - padded
    ustart = jnp.cumsum(counts) - counts
    dest = pstart[e_sorted] + jnp.arange(A, dtype=jnp.int32) - ustart[e_sorted]
    P = A + N_EXPERTS * MOE_BLOCK
    NB = P // MOE_BLOCK
    row_tok = jnp.full((P,), T, jnp.int32).at[dest].set(tok[order])
    row_w = jnp.zeros((P,), xf.dtype).at[dest].set(w_flat[order].astype(xf.dtype))
    blk_e = jnp.minimum(jnp.searchsorted(pend, jnp.arange(NB, dtype=jnp.int32) * MOE_BLOCK,
                                         side='right'), N_EXPERTS - 1).astype(jnp.int32)
    x_pad = jnp.concatenate([xf, jnp.zeros((1, D), xf.dtype)], axis=0)
    xb = x_pad[row_tok].reshape(NB, MOE_BLOCK, D)

    def expert_block(args):
        xe, e = args
        g = jnp.minimum(xe @ w_gate[e] + b_gate[e], SWIGLU_LIMIT)
        u = jnp.clip(xe @ w_up[e] + b_up[e], -SWIGLU_LIMIT, SWIGLU_LIMIT)
        a = g * jax.nn.sigmoid(SWIGLU_ALPHA * g) * (u + 1.0)
        return a @ w_down[e] + b_down[e]

    yb = lax.map(expert_block, (xb, blk_e)).reshape(P, D)
    y = jnp.zeros((T + 1, D), yb.dtype).at[row_tok].add(yb * row_w[:, None])[:T]
    return y.reshape(B, S, D)


def setup_inputs(seed: int = 0) -> dict:
    key = jax.random.key(seed)
    ks = iter(jax.random.split(key, 32))
    L = DEPTH

    def nrm(shape, scale):
        return jax.random.normal(next(ks), shape, jnp.float32) * scale

    col_scale = np.concatenate([np.full((n,), DN_BETA if i in VALUE_BLOCKS else 1.0, np.float32)
                                for i, n in enumerate(IN_SPLITS)])
    return {
        "x": nrm((BATCH, SEQ, D_MODEL), 1.0),
        "ln0_g": 1.0 + nrm((D_MODEL,), 0.02),
        "ln0_b": nrm((D_MODEL,), 0.02),
        "w_in": nrm((L, D_MODEL, N_IN), D_MODEL ** -0.5) * jnp.asarray(col_scale),
        "gla_wf_fwd": nrm((L, GLA_RANK, GLA_K_W), GLA_RANK ** -0.5),
        "gla_bf_fwd": nrm((L, GLA_K_W), 0.1),
        "gla_wf_bwd": nrm((L, GLA_RANK, GLA_K_W), GLA_RANK ** -0.5),
        "gla_bf_bwd": nrm((L, GLA_K_W), 0.1),
        "gla_norm_g": 1.0 + nrm((L, GLA_DV), 0.02),
        "diff_lq1": nrm((L, DIFF_DH), 0.1),
        "diff_lk1": nrm((L, DIFF_DH), 0.1),
        "diff_lq2": nrm((L, DIFF_DH), 0.1),
        "diff_lk2": nrm((L, DIFF_DH), 0.1),
        "diff_norm_g": 1.0 + nrm((L, 2 * DIFF_DH), 0.02),
        "w_br_gla": nrm((L, GLA_V_W, D_MODEL), GLA_V_W ** -0.5 * DN_BETA),
        "w_br_diff": nrm((L, DIFF_V_W, D_MODEL), DIFF_V_W ** -0.5 * DN_BETA),
        "w_out": nrm((L, D_MODEL, D_MODEL), D_MODEL ** -0.5 * DN_BETA),
        "ln1_g": 1.0 + nrm((L, D_MODEL), 0.02),
        "ln1_b": nrm((L, D_MODEL), 0.02),
        "router_w": nrm((L, D_MODEL, N_EXPERTS), D_MODEL ** -0.5),
        "router_b": nrm((L, N_EXPERTS), 0.01),
        "exp_w_gate": nrm((L, N_EXPERTS, D_MODEL, D_FF), D_MODEL ** -0.5),
        "exp_b_gate": nrm((L, N_EXPERTS, D_FF), 0.02),
        "exp_w_up": nrm((L, N_EXPERTS, D_MODEL, D_FF), D_MODEL ** -0.5),
        "exp_b_up": nrm((L, N_EXPERTS, D_FF), 0.02),
        "exp_w_down": nrm((L, N_EXPERTS, D_FF, D_MODEL), D_FF ** -0.5 * DN_BETA),
        "exp_b_down": nrm((L, N_EXPERTS, D_MODEL), 0.02),
        "ln2_g": 1.0 + nrm((L, D_MODEL), 0.02),
        "ln2_b": nrm((L, D_MODEL), 0.02),
    }


def reference(x, ln0_g, ln0_b, w_in, gla_wf_fwd, gla_bf_fwd, gla_wf_bwd, gla_bf_bwd,
              gla_norm_g, diff_lq1, diff_lk1, diff_lq2, diff_lk2, diff_norm_g,
              w_br_gla, w_br_diff, w_out, ln1_g, ln1_b, router_w, router_b,
              exp_w_gate, exp_b_gate, exp_w_up, exp_b_up, exp_w_down, exp_b_down,
              ln2_g, ln2_b):
    S = x.shape[1]
    pos = jnp.arange(S, dtype=jnp.float32)
    inv_freq = jnp.power(ROPE_THETA, -jnp.arange(0, ROT_DIM, 2, dtype=jnp.float32) / ROT_DIM)
    ang = pos[:, None] * inv_freq[None, :]
    cos = jnp.cos(ang)[None, :, None, None, :]
    sin = jnp.sin(ang)[None, :, None, None, :]

    h = _layer_norm(x, ln0_g, ln0_b)
    for l in range(DEPTH):
        lam_init = 0.8 - 0.6 * math.exp(-0.3 * l)
        mix = _token_mixer(h, w_in[l], gla_wf_fwd[l], gla_bf_fwd[l], gla_wf_bwd[l], gla_bf_bwd[l],
                           gla_norm_g[l], diff_lq1[l], diff_lk1[l], diff_lq2[l], diff_lk2[l],
                           diff_norm_g[l], w_br_gla[l], w_br_diff[l], w_out[l],
                           cos, sin, lam_init)
        h = _layer_norm(DN_ALPHA * h + mix, ln1_g[l], ln1_b[l])
        ffn = _moe(h, router_w[l], router_b[l], exp_w_gate[l], exp_b_gate[l],
                   exp_w_up[l], exp_b_up[l], exp_w_down[l], exp_b_down[l])
        h = _layer_norm(DN_ALPHA * h + ffn, ln2_g[l], ln2_b[l])
    return h
```

```python
import functools
import math

import jax
import jax.numpy as jnp
from jax import lax
from jax.experimental import pallas as pl
from jax.experimental.pallas import tpu as pltpu

F32 = jnp.float32
BF16 = jnp.bfloat16
HIGHEST = lax.Precision.HIGHEST

GLA_HEADS = 4
GLA_DK = 128
GLA_DV = 256
GLA_RANK = 16
GLA_TAU = 16.0
GLA_CHUNK = 64
DIFF_HEADS = 8
DIFF_DH = 64
ROT_DIM = DIFF_DH // 4
ROPE_THETA = 500000.0
N_EXPERTS = 32
TOP_K = 4
SWIGLU_LIMIT = 7.0
SWIGLU_ALPHA = 1.702
EPS = 1e-5

LANES = 128
VMEM_LIMIT = 48 * 1024 * 1024


def _cparams(sem):
    return pltpu.CompilerParams(dimension_semantics=sem, vmem_limit_bytes=VMEM_LIMIT)


def _layer_norm(x, g, b):
    mu = jnp.mean(x, axis=-1, keepdims=True)
    xc = x - mu
    var = jnp.mean(xc * xc, axis=-1, keepdims=True)
    return xc * lax.rsqrt(var + EPS) * g + b


def _sigmoid(x):
    return 1.0 / (1.0 + jnp.exp(-x))


def _log_sigmoid(x):
    return jnp.minimum(x, 0.0) - jnp.log(1.0 + jnp.exp(-jnp.abs(x)))


def _ln_proj_kernel(x_ref, g_ref, b_ref, w_ref, o_ref, *rest, emit_h):
    if emit_h:
        h_ref, h16_s = rest
    else:
        (h16_s,) = rest

    @pl.when(pl.program_id(1) == 0)
    def _():
        h = _layer_norm(x_ref[...], g_ref[...], b_ref[...])
        h16_s[...] = h.astype(BF16)
        if emit_h:
            h_ref[...] = h

    o_ref[...] = jnp.dot(h16_s[...], w_ref[...], preferred_element_type=F32).astype(o_ref.dtype)


def _ln_proj(x, g, b, w16, out_dtype, tn, emit_h, name):
    t, d = x.shape
    n = w16.shape[1]
    tm = min(512, t)
    out_shape = [jax.ShapeDtypeStruct((t, n), out_dtype)]
    out_specs = [pl.BlockSpec((tm, tn), lambda i, j: (i, j))]
    if emit_h:
        out_shape.append(jax.ShapeDtypeStruct((t, d), F32))
        out_specs.append(pl.BlockSpec((tm, d), lambda i, j: (i, 0)))
    res = pl.pallas_call(
        functools.partial(_ln_proj_kernel, emit_h=emit_h),
        out_shape=out_shape,
        grid=(t // tm, n // tn),
        in_specs=[
            pl.BlockSpec((tm, d), lambda i, j: (i, 0)),
            pl.BlockSpec((1, d), lambda i, j: (0, 0)),
            pl.BlockSpec((1, d), lambda i, j: (0, 0)),
            pl.BlockSpec((d, tn), lambda i, j: (0, j)),
        ],
        out_specs=out_specs,
        scratch_shapes=[pltpu.VMEM((tm, d), BF16)],
        compiler_params=_cparams(("parallel", "arbitrary")),
        name=name,
    )(x, g.reshape(1, d), b.reshape(1, d), w16)
    return res


def _gla_kernel(q_ref, kt_ref, v_ref, lr_ref, lrt_ref, wf_ref, wft_ref, bf_ref, bft_ref,
                o_ref, s_ref, *, rev, n_chunks):
    c_len = GLA_CHUNK

    @pl.when(pl.program_id(2) == 0)
    def _():
        s_ref[...] = jnp.zeros_like(s_ref)

    z = jnp.dot(lr_ref[...], wf_ref[...], precision=HIGHEST, preferred_element_type=F32) + bf_ref[...]
    logf = _log_sigmoid(z) * (1.0 / GLA_TAU)
    zt = jnp.dot(wft_ref[...], lrt_ref[...], precision=HIGHEST, preferred_element_type=F32) + bft_ref[...]
    logft = _log_sigmoid(zt) * (1.0 / GLA_TAU)

    ri = lax.broadcasted_iota(jnp.int32, (c_len, c_len), 0)
    ci = lax.broadcasted_iota(jnp.int32, (c_len, c_len), 1)
    lower = ri >= ci
    upper = ri <= ci
    keep = upper if rev else lower
    tri = keep.astype(F32)
    tri_t = (lower if rev else upper).astype(F32)
    last = 0 if rev else c_len - 1
    q_scale = GLA_DK ** -0.5

    order = range(n_chunks - 1, -1, -1) if rev else range(n_chunks)
    for c in order:
        rows = slice(c * c_len, (c + 1) * c_len)
        b = jnp.dot(tri, logf[rows, :], precision=HIGHEST, preferred_element_type=F32)
        bt = jnp.dot(logft[:, rows], tri_t, precision=HIGHEST, preferred_element_type=F32)
        bt_last = bt[:, last:last + 1]
        q_in = (q_ref[rows, :].astype(F32) * q_scale * jnp.exp(b)).astype(BF16)
        kt = kt_ref[:, rows].astype(F32)
        k_in_t = (kt * jnp.exp(-bt)).astype(BF16)
        k_st_t = (kt * jnp.exp(bt_last - bt)).astype(BF16)
        v = v_ref[rows, :]
        att = jnp.dot(q_in, k_in_t, preferred_element_type=F32)
        att = jnp.where(keep, att, 0.0).astype(BF16)
        state = s_ref[...]
        o = (jnp.dot(att, v, preferred_element_type=F32)
             + jnp.dot(q_in, state.astype(BF16), preferred_element_type=F32))
        o_ref[rows, :] = o
        s_ref[...] = jnp.exp(bt_last) * state + jnp.dot(k_st_t, v, preferred_element_type=F32)


def _gla(pa, kgt, pb, lrt, wf_pad, wft_pad, bf, bft, batch, seq, rev):
    t = pa.shape[0]
    n_chunks = 4 if seq % (4 * GLA_CHUNK) == 0 else 1
    r = n_chunks * GLA_CHUNK
    ng = seq // r
    dk, dv, h = GLA_DK, GLA_DV, GLA_HEADS
    v_off = (2 * h * dk) // dv
    lr_blk = (3 * h * dv) // LANES

    def gi(g):
        return ng - 1 - g if rev else g

    return pl.pallas_call(
        functools.partial(_gla_kernel, rev=rev, n_chunks=n_chunks),
        out_shape=jax.ShapeDtypeStruct((t, h * dv), F32),
        grid=(batch, h, ng),
        in_specs=[
            pl.BlockSpec((r, dk), lambda b, hh, g: (b * ng + gi(g), hh)),
            pl.BlockSpec((None, dk, r), lambda b, hh, g: (b, hh, gi(g))),
            pl.BlockSpec((r, dv), lambda b, hh, g: (b * ng + gi(g), v_off + hh)),
            pl.BlockSpec((r, LANES), lambda b, hh, g: (b * ng + gi(g), lr_blk)),
            pl.BlockSpec((None, 2 * GLA_RANK, r), lambda b, hh, g: (b, 0, gi(g))),
            pl.BlockSpec((LANES, dk), lambda b, hh, g: (0, hh)),
            pl.BlockSpec((dk, 2 * GLA_RANK), lambda b, hh, g: (hh, 0)),
            pl.BlockSpec((1, dk), lambda b, hh, g: (0, hh)),
            pl.BlockSpec((dk, 1), lambda b, hh, g: (hh, 0)),
        ],
        out_specs=pl.BlockSpec((r, dv), lambda b, hh, g: (b * ng + gi(g), hh)),
        scratch_shapes=[pltpu.VMEM((dk, dv), F32)],
        compiler_params=_cparams(("parallel", "parallel", "arbitrary")),
        name="gla_rev" if rev else "gla_fwd",
    )(pa, kgt, pa, pb, lrt, wf_pad, wft_pad, bf, bft)


def _rope(x, c, a, b):
    return (x * c + pltpu.roll(x, LANES - ROT_DIM // 2, 1) * a + pltpu.roll(x, ROT_DIM // 2, 1) * b)


def _diff_attn_kernel(q_ref, k_ref, v_ref, ck_ref, ak_ref, bk_ref, cq_ref, aq_ref, bq_ref,
                      lq1_ref, lk1_ref, lq2_ref, lk2_ref, g_ref, o_ref,
                      kt_s, q_s, m_s, l_s, acc_s, *, seq, tk, lam_init):
    dh = DIFF_DH
    tq = q_ref.shape[0]

    @pl.when(pl.program_id(2) == 0)
    def _():
        def body(i, carry):
            rows = pl.ds(pl.multiple_of(i * tk, tk), tk)
            kc = _rope(k_ref[rows, :].astype(F32), ck_ref[rows, :], ak_ref[rows, :], bk_ref[rows, :])
            kt_s[:, rows] = kc.T.astype(BF16)
            return carry
        lax.fori_loop(0, seq // tk, body, 0)

    q = _rope(q_ref[...].astype(F32), cq_ref[...], aq_ref[...], bq_ref[...])
    q_s[...] = (q * (dh ** -0.5)).astype(BF16)
    m_s[...] = jnp.full_like(m_s, -jnp.inf)
    l_s[...] = jnp.zeros_like(l_s)
    acc_s[...] = jnp.zeros_like(acc_s)

    def kv_body(i, carry):
        cols = pl.ds(pl.multiple_of(i * tk, tk), tk)
        kt = kt_s[:, cols]
        v = v_ref[cols, :]
        for p in range(2):
            s = jnp.dot(q_s[:, p * dh:(p + 1) * dh], kt[p * dh:(p + 1) * dh, :], preferred_element_type=F32)
            m_old = m_s[p]
            m_new = jnp.maximum(m_old, jnp.max(s, axis=-1, keepdims=True))
            alpha = jnp.exp(m_old - m_new)
            pr = jnp.exp(s - m_new)
            l_s[p] = alpha * l_s[p] + jnp.sum(pr, axis=-1, keepdims=True)
            acc_s[p] = alpha * acc_s[p] + jnp.dot(pr.astype(BF16), v, preferred_element_type=F32)
            m_s[p] = m_new
        return carry
    lax.fori_loop(0, seq // tk, kv_body, 0)

    lam = (jnp.exp(jnp.sum(lq1_ref[...] * lk1_ref[...], axis=-1, keepdims=True))
           - jnp.exp(jnp.sum(lq2_ref[...] * lk2_ref[...], axis=-1, keepdims=True)) + lam_init)
    o = acc_s[0] / l_s[0] - lam * (acc_s[1] / l_s[1])
    ms = jnp.mean(o * o, axis=-1, keepdims=True)
    o_ref[...] = (o * lax.rsqrt(ms + EPS) * g_ref[...] * (1.0 - lam_init)).astype(o_ref.dtype)


def _diff_attn(pa, tabs, lq1, lk1, lq2, lk2, norm_g, batch, seq, lam_init):
    t = pa.shape[0]
    h, dh = DIFF_HEADS, DIFF_DH
    hw = 2 * dh
    tq = min(256, seq)
    tk = min(512, seq)
    nq = seq // tq
    q_off = (2 * GLA_HEADS * GLA_DK + GLA_HEADS * GLA_DV) // hw
    k_off = q_off + h
    v_off = k_off + h
    vec = lambda n: pl.BlockSpec((1, n), lambda b, hh, qi: (0, 0))
    return pl.pallas_call(
        functools.partial(_diff_attn_kernel, seq=seq, tk=tk, lam_init=lam_init),
        out_shape=jax.ShapeDtypeStruct((t, h * hw), BF16),
        grid=(batch, h, nq),
        in_specs=[
            pl.BlockSpec((tq, hw), lambda b, hh, qi: (b * nq + qi, q_off + hh)),
            pl.BlockSpec((seq, hw), lambda b, hh, qi: (b, k_off + hh)),
            pl.BlockSpec((seq, hw), lambda b, hh, qi: (b, v_off + hh)),
            pl.BlockSpec((seq, hw), lambda b, hh, qi: (0, 0)),
            pl.BlockSpec((seq, hw), lambda b, hh, qi: (0, 0)),
            pl.BlockSpec((seq, hw), lambda b, hh, qi: (0, 0)),
            pl.BlockSpec((tq, hw), lambda b, hh, qi: (qi, 0)),
            pl.BlockSpec((tq, hw), lambda b, hh, qi: (qi, 0)),
            pl.BlockSpec((tq, hw), lambda b, hh, qi: (qi, 0)),
            vec(dh), vec(dh), vec(dh), vec(dh), vec(hw),
        ],
        out_specs=pl.BlockSpec((tq, hw), lambda b, hh, qi: (b * nq + qi, hh)),
        scratch_shapes=[
            pltpu.VMEM((hw, seq), BF16),
            pltpu.VMEM((tq, hw), BF16),
            pltpu.VMEM((2, tq, 1), F32),
            pltpu.VMEM((2, tq, 1), F32),
            pltpu.VMEM((2, tq, hw), F32),
        ],
        compiler_params=_cparams(("parallel", "parallel", "arbitrary")),
        name="diff_attn",
    )(pa, pa, pa, *tabs, *tabs, lq1.reshape(1, dh), lk1.reshape(1, dh), lq2.reshape(1, dh),
      lk2.reshape(1, dh), norm_g.reshape(1, hw))


def _rope_tables(seq):
    half = ROT_DIM // 2
    pos = jnp.arange(seq, dtype=F32)
    inv_freq = jnp.power(ROPE_THETA, -jnp.arange(0, ROT_DIM, 2, dtype=F32) / ROT_DIM)
    ang = pos[:, None] * inv_freq[None, :]
    cos, sin = jnp.cos(ang), jnp.sin(ang)
    ones = jnp.ones((seq, DIFF_DH - ROT_DIM), F32)
    zeros = jnp.zeros((seq, DIFF_DH - ROT_DIM), F32)
    zh = jnp.zeros((seq, half), F32)
    c = jnp.concatenate([cos, cos, ones], axis=1)
    a = jnp.concatenate([-sin, zh, zeros], axis=1)
    b = jnp.concatenate([zh, sin, zeros], axis=1)
    return tuple(jnp.tile(m, (1, 2)) for m in (c, a, b))


def _merge_kernel(of_ref, ob_ref, pb_rg, od_ref, pb_gg, pb_gd, h_ref, gn_ref, wa_ref, wb_ref, wo_ref,
                  g1_ref, b1_ref, rw_ref, rb_ref, h1_ref, route_ref, cnt_ref, *, dn_alpha):
    tm = of_ref.shape[0]

    @pl.when(pl.program_id(0) == 0)
    def _():
        cnt_ref[...] = jnp.zeros_like(cnt_ref)

    og = of_ref[...] + ob_ref[...]
    parts = []
    for hh in range(GLA_HEADS):
        xh = og[:, hh * GLA_DV:(hh + 1) * GLA_DV]
        ms = jnp.mean(xh * xh, axis=-1, keepdims=True)
        parts.append(xh * lax.rsqrt(ms + EPS))
    rg = pb_rg[...]
    o_gla = jnp.concatenate(parts, axis=1) * gn_ref[...] * (rg * _sigmoid(rg))
    a = jnp.dot(o_gla.astype(BF16), wa_ref[...], preferred_element_type=F32)
    bb = jnp.dot(od_ref[...], wb_ref[...], preferred_element_type=F32)
    mixed = _sigmoid(pb_gg[...]) * a + _sigmoid(pb_gd[...]) * bb
    mix = jnp.dot(mixed.astype(BF16), wo_ref[...], preferred_element_type=F32)
    h1 = _layer_norm(dn_alpha * h_ref[...] + mix, g1_ref[...], b1_ref[...])
    h1_ref[...] = h1

    logits = jnp.dot(h1, rw_ref[...], precision=HIGHEST, preferred_element_type=F32) + rb_ref[...]
    lane = lax.broadcasted_iota(jnp.int32, (tm, LANES), 1).astype(F32)
    work = logits
    top_v, top_i, sels = [], [], []
    for _ in range(TOP_K):
        m = jnp.max(work, axis=-1, keepdims=True)
        idx = jnp.min(jnp.where(work == m, lane, float(LANES)), axis=-1, keepdims=True)
        sel = lane == idx
        top_v.append(m)
        top_i.append(idx)
        sels.append(sel)
        work = jnp.where(sel, -jnp.inf, work)
    exps = [jnp.exp(v - top_v[0]) for v in top_v]
    denom = exps[0]
    for e in exps[1:]:
        denom = denom + e
    onehot = sels[0].astype(F32)
    for s in sels[1:]:
        onehot = onehot + s.astype(F32)
    ri = lax.broadcasted_iota(jnp.int32, (tm, tm), 0)
    ci = lax.broadcasted_iota(jnp.int32, (tm, tm), 1)
    strict = (ci < ri).astype(BF16)
    before = jnp.dot(strict, onehot.astype(BF16), preferred_element_type=F32) + cnt_ref[...]
    route = jnp.zeros((tm, LANES), F32)
    for k in range(TOP_K):
        rank = jnp.sum(jnp.where(sels[k], before, 0.0), axis=-1, keepdims=True)
        route = jnp.where(lane == float(k), top_i[k], route)
        route = jnp.where(lane == float(TOP_K + k), exps[k] / denom, route)
        route = jnp.where(lane == float(2 * TOP_K + k), rank, route)
    route_ref[...] = route
    cnt_ref[...] = cnt_ref[...] + jnp.sum(onehot, axis=0, keepdims=True)


def _merge(o_f, o_b, pb, o_diff, h, gn, wa, wb, wo, g1, b1, rw, rb, dn_alpha):
    t, d = h.shape
    tm = min(256, t)
    gw = d // LANES
    row = lambda c: pl.BlockSpec((tm, d), lambda i: (i, c))
    full = lambda shp: pl.BlockSpec(shp, lambda i: (0, 0))
    del gw
    return pl.pallas_call(
        functools.partial(_merge_kernel, dn_alpha=dn_alpha),
        out_shape=[jax.ShapeDtypeStruct((t, d), F32),
                   jax.ShapeDtypeStruct((t, LANES), F32),
                   jax.ShapeDtypeStruct((1, LANES), F32)],
        grid=(t // tm,),
        in_specs=[row(0), row(0), row(0), row(0), row(1), row(2), row(0),
                  full((1, d)), full((d, d)), full((d, d)), full((d, d)),
                  full((1, d)), full((1, d)), full((d, LANES)), full((1, LANES))],
        out_specs=[pl.BlockSpec((tm, d), lambda i: (i, 0)),
                   pl.BlockSpec((tm, LANES), lambda i: (i, 0)),
                   pl.BlockSpec((1, LANES), lambda i: (0, 0))],
        compiler_params=_cparams(("arbitrary",)),
        name="merge_ln1_router",
    )(o_f, o_b, pb, o_diff, pb, pb, h, gn, wa, wb, wo, g1, b1, rw, rb)


def _dispatch_kernel(dest_ref, h_ref, xb_in, xb_out, sem):
    del xb_in
    tm = h_ref.shape[0]

    def row_copy(r, d):
        return pltpu.make_async_copy(h_ref.at[pl.ds(r, 1)], xb_out.at[pl.ds(d, 1)], sem)

    def start(r, carry):
        for k in range(TOP_K):
            row_copy(r, dest_ref[r * TOP_K + k]).start()
        return carry
    lax.fori_loop(0, tm, start, 0)

    def wait(r, carry):
        for k in range(TOP_K):
            row_copy(r, dest_ref[r * TOP_K + k]).wait()
        return carry
    lax.fori_loop(0, tm, wait, 0)


def _dispatch(dest_flat, h1, n_rows):
    t, d = h1.shape
    tm = min(256, t)
    xb0 = jnp.zeros((n_rows, d), F32)
    return pl.pallas_call(
        _dispatch_kernel,
        out_shape=jax.ShapeDtypeStruct((n_rows, d), F32),
        grid=(t // tm,),
        in_specs=[
            pl.BlockSpec((tm * TOP_K,), lambda i: (i,), memory_space=pltpu.SMEM),
            pl.BlockSpec((tm, d), lambda i: (i, 0)),
            pl.BlockSpec(memory_space=pl.ANY),
        ],
        out_specs=pl.BlockSpec(memory_space=pl.ANY),
        scratch_shapes=[pltpu.SemaphoreType.DMA],
        input_output_aliases={2: 0},
        compiler_params=_cparams(("arbitrary",)),
        name="moe_dispatch",
    )(dest_flat, h1, xb0)


def _experts_kernel(be_ref, nu_ref, x_ref, wg_ref, bg_ref, wu_ref, bu_ref, wd_ref, bd_ref, y_ref,
                    wg_s, wu_s, wd_s):
    i = pl.program_id(0)

    @pl.when(i >= nu_ref[0])
    def _():
        y_ref[...] = jnp.zeros_like(y_ref)

    @pl.when(i < nu_ref[0])
    def _():
        prev = be_ref[jnp.maximum(i - 1, 0)]

        @pl.when((i == 0) | (be_ref[i] != prev))
        def _():
            wg_s[...] = wg_ref[...].astype(BF16)
            wu_s[...] = wu_ref[...].astype(BF16)
            wd_s[...] = wd_ref[...].astype(BF16)

        x = x_ref[...].astype(BF16)
        g = jnp.minimum(jnp.dot(x, wg_s[...], preferred_element_type=F32) + bg_ref[...], SWIGLU_LIMIT)
        u = jnp.clip(jnp.dot(x, wu_s[...], preferred_element_type=F32) + bu_ref[...],
                     -SWIGLU_LIMIT, SWIGLU_LIMIT)
        a = g * _sigmoid(SWIGLU_ALPHA * g) * (u + 1.0)
        y_ref[...] = jnp.dot(a.astype(BF16), wd_s[...], preferred_element_type=F32) + bd_ref[...]


def _experts(blk_e, n_used, xb, wg, bg, wu, bu, wd, bd, tb):
    p, d = xb.shape
    ne, _, dff = wg.shape
    nb = p // tb
    rowblk = lambda i, be, nu: (jnp.minimum(i, nu[0] - 1), 0)
    wspec = lambda a, b: pl.BlockSpec((None, a, b), lambda i, be, nu: (be[i], 0, 0))
    return pl.pallas_call(
        _experts_kernel,
        out_shape=jax.ShapeDtypeStruct((p, d), F32),
        grid_spec=pltpu.PrefetchScalarGridSpec(
            num_scalar_prefetch=2,
            grid=(nb,),
            in_specs=[pl.BlockSpec((tb, d), rowblk),
                      wspec(d, dff), wspec(1, dff), wspec(d, dff), wspec(1, dff),
                      wspec(dff, d), wspec(1, d)],
            out_specs=pl.BlockSpec((tb, d), lambda i, be, nu: (i, 0)),
            scratch_shapes=[pltpu.VMEM((d, dff), BF16), pltpu.VMEM((d, dff), BF16),
                            pltpu.VMEM((dff, d), BF16)],
        ),
        compiler_params=_cparams(("arbitrary",)),
        name="moe_experts",
    )(blk_e, n_used, xb, wg, bg.reshape(ne, 1, dff), wu, bu.reshape(ne, 1, dff), wd, bd.reshape(ne, 1, d))


def _combine_kernel(dest_ref, yb_ref, h_ref, route_ref, g_ref, b_ref, o_ref, ybuf, sem, *, dn_alpha):
    tm = h_ref.shape[0]

    def row_copy(r, k, d):
        return pltpu.make_async_copy(yb_ref.at[pl.ds(d, 1)], ybuf.at[k, pl.ds(r, 1)], sem)

    def start(r, carry):
        for k in range(TOP_K):
            row_copy(r, k, dest_ref[r * TOP_K + k]).start()
        return carry
    lax.fori_loop(0, tm, start, 0)

    def wait(r, carry):
        for k in range(TOP_K):
            row_copy(r, k, dest_ref[r * TOP_K + k]).wait()
        return carry
    lax.fori_loop(0, tm, wait, 0)

    route = route_ref[...]
    ffn = jnp.zeros(h_ref.shape, F32)
    for k in range(TOP_K):
        ffn = ffn + route[:, TOP_K + k:TOP_K + k + 1] * ybuf[k]
    o_ref[...] = _layer_norm(dn_alpha * h_ref[...] + ffn, g_ref[...], b_ref[...])


def _combine(dest_flat, yb, h1, route, g2, b2, dn_alpha):
    t, d = h1.shape
    tm = min(128, t)
    return pl.pallas_call(
        functools.partial(_combine_kernel, dn_alpha=dn_alpha),
        out_shape=jax.ShapeDtypeStruct((t, d), F32),
        grid=(t // tm,),
        in_specs=[
            pl.BlockSpec((tm * TOP_K,), lambda i: (i,), memory_space=pltpu.SMEM),
            pl.BlockSpec(memory_space=pl.ANY),
            pl.BlockSpec((tm, d), lambda i: (i, 0)),
            pl.BlockSpec((tm, LANES), lambda i: (i, 0)),
            pl.BlockSpec((1, d), lambda i: (0, 0)),
            pl.BlockSpec((1, d), lambda i: (0, 0)),
        ],
        out_specs=pl.BlockSpec((tm, d), lambda i: (i, 0)),
        scratch_shapes=[pltpu.VMEM((TOP_K, tm, d), F32), pltpu.SemaphoreType.DMA],
        compiler_params=_cparams(("arbitrary",)),
        name="moe_combine_ln2",
    )(dest_flat, yb, h1, route, g2.reshape(1, d), b2.reshape(1, d))


def _moe_block_rows(t):
    return min(256, t)


def kernel(x, ln0_g, ln0_b, w_in, gla_wf_fwd, gla_bf_fwd, gla_wf_bwd, gla_bf_bwd, gla_norm_g, diff_lq1, diff_lk1, diff_lq2, diff_lk2, diff_norm_g, w_br_gla, w_br_diff, w_out, ln1_g, ln1_b, router_w, router_b, exp_w_gate, exp_b_gate, exp_w_up, exp_b_up, exp_w_down, exp_b_down, ln2_g, ln2_b):
    batch, seq, d = x.shape
    depth = w_in.shape[0]
    t = batch * seq
    dn_alpha = (2 * depth) ** 0.25
    gk, gv = GLA_HEADS * GLA_DK, GLA_HEADS * GLA_DV
    dq = DIFF_HEADS * 2 * DIFF_DH
    splits = (gk, gk, gv, gv, GLA_RANK, GLA_RANK, dq, dq, dq, d, d)
    cuts = [0]
    for s in splits:
        cuts.append(cuts[-1] + s)
    col = lambda w, i: w[:, cuts[i]:cuts[i + 1]]
    tabs = _rope_tables(seq)
    tb = _moe_block_rows(t)
    n_slots = t * TOP_K
    n_rows = n_slots + N_EXPERTS * tb
    nb = n_rows // tb

    assert depth == 1, depth
    cur = x.reshape(t, d)
    g_in, b_in = ln0_g, ln0_b
    for l in range(depth):
        lam_init = 0.8 - 0.6 * math.exp(-0.3 * l)
        w = w_in[l]
        wa = jnp.concatenate([col(w, 0), col(w, 1), col(w, 2), col(w, 6), col(w, 7), col(w, 8)], axis=1).astype(BF16)
        wb = jnp.concatenate([col(w, 3), col(w, 9), col(w, 10), col(w, 4), col(w, 5),
                              jnp.zeros((d, LANES - 2 * GLA_RANK), F32)], axis=1).astype(BF16)
        pa, h = _ln_proj(cur, g_in, b_in, wa, BF16, 512, True, "ln_proj_bf16")
        (pb,) = _ln_proj(cur, g_in, b_in, wb, F32, 640, False, "ln_proj_f32")

        kgt = pa[:, gk:2 * gk].reshape(batch, seq, gk).transpose(0, 2, 1)
        lrt = pb[:, 3 * d:3 * d + 2 * GLA_RANK].reshape(batch, seq, 2 * GLA_RANK).transpose(0, 2, 1)
        zr = jnp.zeros((GLA_RANK, gk), F32)
        pad = jnp.zeros((LANES - 2 * GLA_RANK, gk), F32)
        wf_f = jnp.concatenate([gla_wf_fwd[l], zr, pad], axis=0)
        wf_b = jnp.concatenate([zr, gla_wf_bwd[l], pad], axis=0)
        o_f = _gla(pa, kgt, pb, lrt, wf_f, wf_f[:2 * GLA_RANK].T, gla_bf_fwd[l].reshape(1, gk),
                   gla_bf_fwd[l].reshape(gk, 1), batch, seq, rev=False)
        o_b = _gla(pa, kgt, pb, lrt, wf_b, wf_b[:2 * GLA_RANK].T, gla_bf_bwd[l].reshape(1, gk),
                   gla_bf_bwd[l].reshape(gk, 1), batch, seq, rev=True)

        o_diff = _diff_attn(pa, tabs, diff_lq1[l], diff_lk1[l], diff_lq2[l], diff_lk2[l], diff_norm_g[l],
                            batch, seq, lam_init)

        rw = jnp.concatenate([router_w[l], jnp.zeros((d, LANES - N_EXPERTS), F32)], axis=1)
        rb = jnp.concatenate([router_b[l], jnp.full((LANES - N_EXPERTS,), -jnp.inf, F32)]).reshape(1, LANES)
        h1, route, counts = _merge(
            o_f, o_b, pb, o_diff, h, jnp.tile(gla_norm_g[l], GLA_HEADS).reshape(1, gv),
            w_br_gla[l].astype(BF16), w_br_diff[l].astype(BF16), w_out[l].astype(BF16),
            ln1_g[l].reshape(1, d), ln1_b[l].reshape(1, d), rw, rb, dn_alpha)

        cnt = counts[0, :N_EXPERTS].astype(jnp.int32)
        padded = ((cnt + tb - 1) // tb) * tb
        pend = jnp.cumsum(padded)
        pstart = pend - padded
        top_e = route[:, :TOP_K].astype(jnp.int32)
        rank = route[:, 2 * TOP_K:3 * TOP_K].astype(jnp.int32)
        dest = (pstart[top_e] + rank).reshape(n_slots)
        n_used = (pend[-1] // tb).astype(jnp.int32)
        blk = jnp.minimum(jnp.arange(nb, dtype=jnp.int32), n_used - 1)
        blk_e = jnp.minimum(jnp.searchsorted(pend, blk * tb, side='right'), N_EXPERTS - 1).astype(jnp.int32)

        xb = _dispatch(dest, h1, n_rows)
        yb = _experts(blk_e, n_used.reshape(1), xb, exp_w_gate[l], exp_b_gate[l], exp_w_up[l], exp_b_up[l],
                      exp_w_down[l], exp_b_down[l], tb)
        cur = _combine(dest, yb, h1, route, ln2_g[l], ln2_b[l], dn_alpha)
    return cur.reshape(batch, seq, d)
```

```python
import functools
import math

import jax
import jax.numpy as jnp
from jax import lax
from jax.experimental import pallas as pl
from jax.experimental.pallas import tpu as pltpu

F32 = jnp.float32
BF16 = jnp.bfloat16
HIGHEST = lax.Precision.HIGHEST

GLA_HEADS = 4
GLA_DK = 128
GLA_DV = 256
GLA_RANK = 16
GLA_TAU = 16.0
GLA_CHUNK = 64
DIFF_HEADS = 8
DIFF_DH = 64
ROT_DIM = DIFF_DH // 4
ROPE_THETA = 500000.0
N_EXPERTS = 32
TOP_K = 4
SWIGLU_LIMIT = 7.0
SWIGLU_ALPHA = 1.702
EPS = 1e-5

LANES = 128
VMEM_LIMIT = 48 * 1024 * 1024


def _cparams(sem):
    return pltpu.CompilerParams(dimension_semantics=sem, vmem_limit_bytes=VMEM_LIMIT)


def _layer_norm(x, g, b):
    mu = jnp.mean(x, axis=-1, keepdims=True)
    xc = x - mu
    var = jnp.mean(xc * xc, axis=-1, keepdims=True)
    return xc * lax.rsqrt(var + EPS) * g + b


def _sigmoid(x):
    return 1.0 / (1.0 + jnp.exp(-x))


def _log_sigmoid(x):
    return jnp.minimum(x, 0.0) - jnp.log(1.0 + jnp.exp(-jnp.abs(x)))


def _lane_tile(x, n):
    return jnp.concatenate([x] * n, axis=1)


def _split3(x):
    hi = x.astype(BF16)
    r1 = x - hi.astype(F32)
    mid = r1.astype(BF16)
    lo = (r1 - mid.astype(F32)).astype(BF16)
    return hi, mid, lo


def _ln_proj_kernel(x_ref, g_ref, b_ref, w_ref, o_ref, *rest, emit_h):
    if emit_h:
        h_ref, h16_s = rest
    else:
        (h16_s,) = rest

    @pl.when(pl.program_id(1) == 0)
    def _():
        h = _layer_norm(x_ref[...], g_ref[...], b_ref[...])
        h16_s[...] = h.astype(BF16)
        if emit_h:
            h_ref[...] = h

    o_ref[...] = jnp.dot(h16_s[...], w_ref[...], preferred_element_type=F32).astype(o_ref.dtype)


def _ln_proj(x, g, b, w16, out_dtype, tn, emit_h, name):
    t, d = x.shape
    n = w16.shape[1]
    tm = min(512, t)
    out_shape = [jax.ShapeDtypeStruct((t, n), out_dtype)]
    out_specs = [pl.BlockSpec((tm, tn), lambda i, j: (i, j))]
    if emit_h:
        out_shape.append(jax.ShapeDtypeStruct((t, d), F32))
        out_specs.append(pl.BlockSpec((tm, d), lambda i, j: (i, 0)))
    res = pl.pallas_call(
        functools.partial(_ln_proj_kernel, emit_h=emit_h),
        out_shape=out_shape,
        grid=(t // tm, n // tn),
        in_specs=[
            pl.BlockSpec((tm, d), lambda i, j: (i, 0)),
            pl.BlockSpec((1, d), lambda i, j: (0, 0)),
            pl.BlockSpec((1, d), lambda i, j: (0, 0)),
            pl.BlockSpec((d, tn), lambda i, j: (0, j)),
        ],
        out_specs=out_specs,
        scratch_shapes=[pltpu.VMEM((tm, d), BF16)],
        compiler_params=_cparams(("parallel", "arbitrary")),
        name=name,
    )(x, g.reshape(1, d), b.reshape(1, d), w16)
    return res


def _gla_kernel(q_ref, kt_ref, v_ref, lr_ref, lrt_ref, wf_ref, wft_ref, bf_ref, bft_ref,
                o_ref, s_ref, *, rev, n_chunks):
    c_len = GLA_CHUNK

    @pl.when(pl.program_id(2) == 0)
    def _():
        s_ref[...] = jnp.zeros_like(s_ref)

    z = jnp.dot(lr_ref[...].astype(BF16), wf_ref[...].astype(BF16), preferred_element_type=F32) + bf_ref[...]
    logf = _log_sigmoid(z) * (1.0 / GLA_TAU)
    zt = jnp.dot(wft_ref[...].astype(BF16), lrt_ref[...].astype(BF16), preferred_element_type=F32) + bft_ref[...]
    logft = _log_sigmoid(zt) * (1.0 / GLA_TAU)

    r = n_chunks * c_len
    shift = c_len.bit_length() - 1
    ri = lax.broadcasted_iota(jnp.int32, (r, r), 0)
    ci = lax.broadcasted_iota(jnp.int32, (r, r), 1)
    same = jnp.right_shift(ri, shift) == jnp.right_shift(ci, shift)
    lower = same & (ri >= ci)
    upper = same & (ri <= ci)
    keep = upper if rev else lower
    tri = keep.astype(BF16)
    tri_t = (lower if rev else upper).astype(BF16)
    blk = same.astype(BF16)
    q_scale = GLA_DK ** -0.5

    b = sum(jnp.dot(tri, part, preferred_element_type=F32) for part in _split3(logf))
    bts = sum(jnp.dot(part, jnp.concatenate([tri_t, blk], axis=1), preferred_element_type=F32)
              for part in _split3(logft))
    bt, bt_tot = bts[:, :r], bts[:, r:]
    q_in = (q_ref[...].astype(F32) * q_scale * jnp.exp(b)).astype(BF16)
    kt = kt_ref[...].astype(F32)
    k_in_t = (kt * jnp.exp(-bt)).astype(BF16)
    k_st_t = (kt * jnp.exp(bt_tot - bt)).astype(BF16)
    decay = jnp.exp(bt_tot)
    v = v_ref[...]
    att = jnp.dot(q_in, k_in_t, preferred_element_type=F32)
    att = jnp.where(keep, att, 0.0).astype(BF16)
    o_intra = jnp.dot(att, v, preferred_element_type=F32)

    order = range(n_chunks - 1, -1, -1) if rev else range(n_chunks)
    for c in order:
        rows = slice(c * c_len, (c + 1) * c_len)
        state = s_ref[...]
        o_ref[rows, :] = o_intra[rows, :] + jnp.dot(q_in[rows, :], state.astype(BF16),
                                                     preferred_element_type=F32)
        s_ref[...] = (decay[:, c * c_len:c * c_len + 1] * state
                      + jnp.dot(k_st_t[:, rows], v[rows, :], preferred_element_type=F32))


def _gla(pa, kgt, pb, lrt, wf_pad, wft_pad, bf, bft, batch, seq, rev):
    t = pa.shape[0]
    n_chunks = 4 if seq % (4 * GLA_CHUNK) == 0 else 1
    r = n_chunks * GLA_CHUNK
    ng = seq // r
    dk, dv, h = GLA_DK, GLA_DV, GLA_HEADS
    v_off = (2 * h * dk) // dv
    lr_blk = (3 * h * dv) // LANES

    def gi(g):
        return ng - 1 - g if rev else g

    return pl.pallas_call(
        functools.partial(_gla_kernel, rev=rev, n_chunks=n_chunks),
        out_shape=jax.ShapeDtypeStruct((t, h * dv), F32),
        grid=(batch, h, ng),
        in_specs=[
            pl.BlockSpec((r, dk), lambda b, hh, g: (b * ng + gi(g), hh)),
            pl.BlockSpec((None, dk, r), lambda b, hh, g: (b, hh, gi(g))),
            pl.BlockSpec((r, dv), lambda b, hh, g: (b * ng + gi(g), v_off + hh)),
            pl.BlockSpec((r, LANES), lambda b, hh, g: (b * ng + gi(g), lr_blk)),
            pl.BlockSpec((None, 2 * GLA_RANK, r), lambda b, hh, g: (b, 0, gi(g))),
            pl.BlockSpec((LANES, dk), lambda b, hh, g: (0, hh)),
            pl.BlockSpec((dk, 2 * GLA_RANK), lambda b, hh, g: (hh, 0)),
            pl.BlockSpec((1, dk), lambda b, hh, g: (0, hh)),
            pl.BlockSpec((dk, 1), lambda b, hh, g: (hh, 0)),
        ],
        out_specs=pl.BlockSpec((r, dv), lambda b, hh, g: (b * ng + gi(g), hh)),
        scratch_shapes=[pltpu.VMEM((dk, dv), F32)],
        compiler_params=_cparams(("parallel", "parallel", "arbitrary")),
        name="gla_rev" if rev else "gla_fwd",
    )(pa, kgt, pa, pb, lrt, wf_pad, wft_pad, bf, bft)


def _rope(x, c, a, b):
    return (x * c + pltpu.roll(x, LANES - ROT_DIM // 2, 1) * a + pltpu.roll(x, ROT_DIM // 2, 1) * b)


def _diff_attn_kernel(q_ref, k_ref, v_ref, ck_ref, ak_ref, bk_ref, cq_ref, aq_ref, bq_ref,
                      lq1_ref, lk1_ref, lq2_ref, lk2_ref, g_ref, o_ref,
                      kt_s, ve_s, q_s, m_s, acc_s, sa_s, sb_s, *, seq, tk, lam_init):
    dh = DIFF_DH
    hw = 2 * dh

    @pl.when(pl.program_id(2) == 0)
    def _():
        def body(i, carry):
            rows = pl.ds(pl.multiple_of(i * tk, tk), tk)
            kc = _rope(k_ref[rows, :].astype(F32), ck_ref[rows, :], ak_ref[rows, :], bk_ref[rows, :])
            kt_s[:, rows] = kc.T.astype(BF16)
            ve_s[rows, :] = jnp.concatenate([v_ref[rows, :], jnp.ones((tk, hw), BF16)], axis=1)
            return carry
        lax.fori_loop(0, seq // tk, body, 0)

    q = _rope(q_ref[...].astype(F32), cq_ref[...], aq_ref[...], bq_ref[...])
    q_s[...] = (q * (dh ** -0.5 * math.log2(math.e))).astype(BF16)
    m_s[...] = jnp.full_like(m_s, -jnp.inf)
    acc_s[...] = jnp.zeros_like(acc_s)

    def scores(i, s_buf):
        cols = pl.ds(pl.multiple_of(i * tk, tk), tk)
        kt = kt_s[:, cols]
        for p in range(2):
            s_buf[p] = jnp.dot(q_s[:, p * dh:(p + 1) * dh], kt[p * dh:(p + 1) * dh, :],
                               preferred_element_type=F32)

    def softmax_pv(i, s_buf):
        ve = ve_s[pl.ds(pl.multiple_of(i * tk, tk), tk), :]
        for p in range(2):
            s = s_buf[p]
            m_old = m_s[p]
            m_new = jnp.maximum(m_old, jnp.max(s, axis=-1, keepdims=True))
            alpha = jnp.exp2(m_old - m_new)
            pr = jnp.exp2(s - _lane_tile(m_new, tk // LANES))
            acc_s[p] = (_lane_tile(alpha, 2) * acc_s[p]
                        + jnp.dot(pr.astype(BF16), ve, preferred_element_type=F32))
            m_s[p] = m_new

    n_t = seq // tk
    scores(0, sa_s)

    def pair_body(j, carry):
        scores(2 * j + 1, sb_s)
        softmax_pv(2 * j, sa_s)
        scores(2 * j + 2, sa_s)
        softmax_pv(2 * j + 1, sb_s)
        return carry
    lax.fori_loop(0, (n_t - 1) // 2, pair_body, 0)
    if n_t % 2 == 0:
        scores(n_t - 1, sb_s)
        softmax_pv(n_t - 2, sa_s)
        softmax_pv(n_t - 1, sb_s)
    else:
        softmax_pv(n_t - 1, sa_s)

    lam = (jnp.exp(jnp.sum(lq1_ref[...] * lk1_ref[...], axis=-1, keepdims=True))
           - jnp.exp(jnp.sum(lq2_ref[...] * lk2_ref[...], axis=-1, keepdims=True)) + lam_init)
    o = acc_s[0, :, :hw] / acc_s[0, :, hw:] - lam * (acc_s[1, :, :hw] / acc_s[1, :, hw:])
    ms = jnp.mean(o * o, axis=-1, keepdims=True)
    o_ref[...] = (o * lax.rsqrt(ms + EPS) * g_ref[...] * (1.0 - lam_init)).astype(o_ref.dtype)


def _diff_attn(pa, tabs, lq1, lk1, lq2, lk2, norm_g, batch, seq, lam_init):
    t = pa.shape[0]
    h, dh = DIFF_HEADS, DIFF_DH
    hw = 2 * dh
    tq = min(256, seq)
    tk = min(512, seq)
    nq = seq // tq
    q_off = (2 * GLA_HEADS * GLA_DK + GLA_HEADS * GLA_DV) // hw
    k_off = q_off + h
    v_off = k_off + h
    vec = lambda n: pl.BlockSpec((1, n), lambda b, hh, qi: (0, 0))
    return pl.pallas_call(
        functools.partial(_diff_attn_kernel, seq=seq, tk=tk, lam_init=lam_init),
        out_shape=jax.ShapeDtypeStruct((t, h * hw), BF16),
        grid=(batch, h, nq),
        in_specs=[
            pl.BlockSpec((tq, hw), lambda b, hh, qi: (b * nq + qi, q_off + hh)),
            pl.BlockSpec((seq, hw), lambda b, hh, qi: (b, k_off + hh)),
            pl.BlockSpec((seq, hw), lambda b, hh, qi: (b, v_off + hh)),
            pl.BlockSpec((seq, hw), lambda b, hh, qi: (0, 0)),
            pl.BlockSpec((seq, hw), lambda b, hh, qi: (0, 0)),
            pl.BlockSpec((seq, hw), lambda b, hh, qi: (0, 0)),
            pl.BlockSpec((tq, hw), lambda b, hh, qi: (qi, 0)),
            pl.BlockSpec((tq, hw), lambda b, hh, qi: (qi, 0)),
            pl.BlockSpec((tq, hw), lambda b, hh, qi: (qi, 0)),
            vec(dh), vec(dh), vec(dh), vec(dh), vec(hw),
        ],
        out_specs=pl.BlockSpec((tq, hw), lambda b, hh, qi: (b * nq + qi, hh)),
        scratch_shapes=[
            pltpu.VMEM((hw, seq), BF16),
            pltpu.VMEM((seq, 2 * hw), BF16),
            pltpu.VMEM((tq, hw), BF16),
            pltpu.VMEM((2, tq, LANES), F32),
            pltpu.VMEM((2, tq, 2 * hw), F32),
            pltpu.VMEM((2, tq, tk), F32),
            pltpu.VMEM((2, tq, tk), F32),
        ],
        compiler_params=_cparams(("parallel", "parallel", "arbitrary")),
        name="diff_attn",
    )(pa, pa, pa, *tabs, *tabs, lq1.reshape(1, dh), lk1.reshape(1, dh), lq2.reshape(1, dh),
      lk2.reshape(1, dh), norm_g.reshape(1, hw))


def _rope_tables(seq):
    half = ROT_DIM // 2
    pos = jnp.arange(seq, dtype=F32)
    inv_freq = jnp.power(ROPE_THETA, -jnp.arange(0, ROT_DIM, 2, dtype=F32) / ROT_DIM)
    ang = pos[:, None] * inv_freq[None, :]
    cos, sin = jnp.cos(ang), jnp.sin(ang)
    ones = jnp.ones((seq, DIFF_DH - ROT_DIM), F32)
    zeros = jnp.zeros((seq, DIFF_DH - ROT_DIM), F32)
    zh = jnp.zeros((seq, half), F32)
    c = jnp.concatenate([cos, cos, ones], axis=1)
    a = jnp.concatenate([-sin, zh, zeros], axis=1)
    b = jnp.concatenate([zh, sin, zeros], axis=1)
    return tuple(jnp.tile(m, (1, 2)) for m in (c, a, b))


def _merge_kernel(of_ref, ob_ref, pb_rg, od_ref, pb_gg, pb_gd, h_ref, gn_ref, wa_ref, wb_ref, wo_ref,
                  g1_ref, b1_ref, rw_ref, rb_ref, h1_ref, route_ref, cnt_ref, *, dn_alpha):
    tm = of_ref.shape[0]

    @pl.when(pl.program_id(0) == 0)
    def _():
        cnt_ref[...] = jnp.zeros_like(cnt_ref)

    og = of_ref[...] + ob_ref[...]
    parts = []
    for hh in range(GLA_HEADS):
        xh = og[:, hh * GLA_DV:(hh + 1) * GLA_DV]
        ms = jnp.mean(xh * xh, axis=-1, keepdims=True)
        parts.append(xh * lax.rsqrt(ms + EPS))
    rg = pb_rg[...]
    o_gla = jnp.concatenate(parts, axis=1) * gn_ref[...] * (rg * _sigmoid(rg))
    a = jnp.dot(o_gla.astype(BF16), wa_ref[...], preferred_element_type=F32)
    bb = jnp.dot(od_ref[...], wb_ref[...], preferred_element_type=F32)
    mixed = _sigmoid(pb_gg[...]) * a + _sigmoid(pb_gd[...]) * bb
    mix = jnp.dot(mixed.astype(BF16), wo_ref[...], preferred_element_type=F32)
    h1 = _layer_norm(dn_alpha * h_ref[...] + mix, g1_ref[...], b1_ref[...])
    h1_ref[...] = h1

    logits = jnp.dot(h1, rw_ref[...], precision=HIGHEST, preferred_element_type=F32) + rb_ref[...]
    lane = lax.broadcasted_iota(jnp.int32, (tm, LANES), 1).astype(F32)
    work = logits
    top_v, top_i, sels = [], [], []
    for _ in range(TOP_K):
        m = jnp.max(work, axis=-1, keepdims=True)
        idx = jnp.min(jnp.where(work == m, lane, float(LANES)), axis=-1, keepdims=True)
        sel = lane == idx
        top_v.append(m)
        top_i.append(idx)
        sels.append(sel)
        work = jnp.where(sel, -jnp.inf, work)
    exps = [jnp.exp(v - top_v[0]) for v in top_v]
    denom = exps[0]
    for e in exps[1:]:
        denom = denom + e
    onehot = sels[0].astype(F32)
    for s in sels[1:]:
        onehot = onehot + s.astype(F32)
    ri = lax.broadcasted_iota(jnp.int32, (tm, tm), 0)
    ci = lax.broadcasted_iota(jnp.int32, (tm, tm), 1)
    strict = (ci < ri).astype(BF16)
    before = jnp.dot(strict, onehot.astype(BF16), preferred_element_type=F32) + cnt_ref[...]
    route = jnp.zeros((tm, LANES), F32)
    for k in range(TOP_K):
        rank = jnp.sum(jnp.where(sels[k], before, 0.0), axis=-1, keepdims=True)
        route = jnp.where(lane == float(k), top_i[k], route)
        route = jnp.where(lane == float(TOP_K + k), exps[k] / denom, route)
        route = jnp.where(lane == float(2 * TOP_K + k), rank, route)
    route_ref[...] = route
    cnt_ref[...] = cnt_ref[...] + jnp.sum(onehot, axis=0, keepdims=True)


def _merge(o_f, o_b, pb, o_diff, h, gn, wa, wb, wo, g1, b1, rw, rb, dn_alpha):
    t, d = h.shape
    tm = min(256, t)
    gw = d // LANES
    row = lambda c: pl.BlockSpec((tm, d), lambda i: (i, c))
    full = lambda shp: pl.BlockSpec(shp, lambda i: (0, 0))
    del gw
    return pl.pallas_call(
        functools.partial(_merge_kernel, dn_alpha=dn_alpha),
        out_shape=[jax.ShapeDtypeStruct((t, d), F32),
                   jax.ShapeDtypeStruct((t, LANES), F32),
                   jax.ShapeDtypeStruct((1, LANES), F32)],
        grid=(t // tm,),
        in_specs=[row(0), row(0), row(0), row(0), row(1), row(2), row(0),
                  full((1, d)), full((d, d)), full((d, d)), full((d, d)),
                  full((1, d)), full((1, d)), full((d, LANES)), full((1, LANES))],
        out_specs=[pl.BlockSpec((tm, d), lambda i: (i, 0)),
                   pl.BlockSpec((tm, LANES), lambda i: (i, 0)),
                   pl.BlockSpec((1, LANES), lambda i: (0, 0))],
        compiler_params=_cparams(("arbitrary",)),
        name="merge_ln1_router",
    )(o_f, o_b, pb, o_diff, pb, pb, h, gn, wa, wb, wo, g1, b1, rw, rb)


def _dispatch_kernel(dest_ref, h_ref, xb_in, xb_out, sem):
    del xb_in
    tm = h_ref.shape[0]

    def row_copy(r, d):
        return pltpu.make_async_copy(h_ref.at[pl.ds(r, 1)], xb_out.at[pl.ds(d, 1)], sem)

    def start(r, carry):
        for k in range(TOP_K):
            row_copy(r, dest_ref[r * TOP_K + k]).start()
        return carry
    lax.fori_loop(0, tm, start, 0)

    def wait(r, carry):
        for k in range(TOP_K):
            row_copy(r, dest_ref[r * TOP_K + k]).wait()
        return carry
    lax.fori_loop(0, tm, wait, 0)


def _dispatch(dest_flat, h1, n_rows):
    t, d = h1.shape
    tm = min(256, t)
    xb0 = jnp.zeros((n_rows, d), F32)
    return pl.pallas_call(
        _dispatch_kernel,
        out_shape=jax.ShapeDtypeStruct((n_rows, d), F32),
        grid=(t // tm,),
        in_specs=[
            pl.BlockSpec((tm * TOP_K,), lambda i: (i,), memory_space=pltpu.SMEM),
            pl.BlockSpec((tm, d), lambda i: (i, 0)),
            pl.BlockSpec(memory_space=pl.ANY),
        ],
        out_specs=pl.BlockSpec(memory_space=pl.ANY),
        scratch_shapes=[pltpu.SemaphoreType.DMA],
        input_output_aliases={2: 0},
        compiler_params=_cparams(("arbitrary",)),
        name="moe_dispatch",
    )(dest_flat, h1, xb0)


def _experts_kernel(be_ref, nu_ref, x_ref, wg_ref, bg_ref, wu_ref, bu_ref, wd_ref, bd_ref, y_ref,
                    wg_s, wu_s, wd_s):
    i = pl.program_id(0)

    @pl.when(i >= nu_ref[0])
    def _():
        y_ref[...] = jnp.zeros_like(y_ref)

    @pl.when(i < nu_ref[0])
    def _():
        prev = be_ref[jnp.maximum(i - 1, 0)]

        @pl.when((i == 0) | (be_ref[i] != prev))
        def _():
            wg_s[...] = wg_ref[...].astype(BF16)
            wu_s[...] = wu_ref[...].astype(BF16)
            wd_s[...] = wd_ref[...].astype(BF16)

        x = x_ref[...].astype(BF16)
        g = jnp.minimum(jnp.dot(x, wg_s[...], preferred_element_type=F32) + bg_ref[...], SWIGLU_LIMIT)
        u = jnp.clip(jnp.dot(x, wu_s[...], preferred_element_type=F32) + bu_ref[...],
                     -SWIGLU_LIMIT, SWIGLU_LIMIT)
        a = g * _sigmoid(SWIGLU_ALPHA * g) * (u + 1.0)
        y_ref[...] = jnp.dot(a.astype(BF16), wd_s[...], preferred_element_type=F32) + bd_ref[...]


def _experts(blk_e, n_used, xb, wg, bg, wu, bu, wd, bd, tb):
    p, d = xb.shape
    ne, _, dff = wg.shape
    nb = p // tb
    rowblk = lambda i, be, nu: (jnp.minimum(i, nu[0] - 1), 0)
    wspec = lambda a, b: pl.BlockSpec((None, a, b), lambda i, be, nu: (be[i], 0, 0))
    return pl.pallas_call(
        _experts_kernel,
        out_shape=jax.ShapeDtypeStruct((p, d), F32),
        grid_spec=pltpu.PrefetchScalarGridSpec(
            num_scalar_prefetch=2,
            grid=(nb,),
            in_specs=[pl.BlockSpec((tb, d), rowblk),
                      wspec(d, dff), wspec(1, dff), wspec(d, dff), wspec(1, dff),
                      wspec(dff, d), wspec(1, d)],
            out_specs=pl.BlockSpec((tb, d), lambda i, be, nu: (i, 0)),
            scratch_shapes=[pltpu.VMEM((d, dff), BF16), pltpu.VMEM((d, dff), BF16),
                            pltpu.VMEM((dff, d), BF16)],
        ),
        compiler_params=_cparams(("arbitrary",)),
        name="moe_experts",
    )(blk_e, n_used, xb, wg, bg.reshape(ne, 1, dff), wu, bu.reshape(ne, 1, dff), wd, bd.reshape(ne, 1, d))


def _combine_kernel(dest_ref, yb_ref, h_ref, route_ref, g_ref, b_ref, o_ref, ybuf, sem, *, dn_alpha):
    tm = h_ref.shape[0]

    def row_copy(r, k, d):
        return pltpu.make_async_copy(yb_ref.at[pl.ds(d, 1)], ybuf.at[k, pl.ds(r, 1)], sem)

    def start(r, carry):
        for k in range(TOP_K):
            row_copy(r, k, dest_ref[r * TOP_K + k]).start()
        return carry
    lax.fori_loop(0, tm, start, 0)

    def wait(r, carry):
        for k in range(TOP_K):
            row_copy(r, k, dest_ref[r * TOP_K + k]).wait()
        return carry
    lax.fori_loop(0, tm, wait, 0)

    route = route_ref[...]
    ffn = jnp.zeros(h_ref.shape, F32)
    for k in range(TOP_K):
        ffn = ffn + route[:, TOP_K + k:TOP_K + k + 1] * ybuf[k]
    o_ref[...] = _layer_norm(dn_alpha * h_ref[...] + ffn, g_ref[...], b_ref[...])


def _combine(dest_flat, yb, h1, route, g2, b2, dn_alpha):
    t, d = h1.shape
    tm = min(128, t)
    return pl.pallas_call(
        functools.partial(_combine_kernel, dn_alpha=dn_alpha),
        out_shape=jax.ShapeDtypeStruct((t, d), F32),
        grid=(t // tm,),
        in_specs=[
            pl.BlockSpec((tm * TOP_K,), lambda i: (i,), memory_space=pltpu.SMEM),
            pl.BlockSpec(memory_space=pl.ANY),
            pl.BlockSpec((tm, d), lambda i: (i, 0)),
            pl.BlockSpec((tm, LANES), lambda i: (i, 0)),
            pl.BlockSpec((1, d), lambda i: (0, 0)),
            pl.BlockSpec((1, d), lambda i: (0, 0)),
        ],
        out_specs=pl.BlockSpec((tm, d), lambda i: (i, 0)),
        scratch_shapes=[pltpu.VMEM((TOP_K, tm, d), F32), pltpu.SemaphoreType.DMA],
        compiler_params=_cparams(("arbitrary",)),
        name="moe_combine_ln2",
    )(dest_flat, yb, h1, route, g2.reshape(1, d), b2.reshape(1, d))


def _moe_block_rows(t):
    return min(256, t)


def kernel(x, ln0_g, ln0_b, w_in, gla_wf_fwd, gla_bf_fwd, gla_wf_bwd, gla_bf_bwd, gla_norm_g, diff_lq1, diff_lk1, diff_lq2, diff_lk2, diff_norm_g, w_br_gla, w_br_diff, w_out, ln1_g, ln1_b, router_w, router_b, exp_w_gate, exp_b_gate, exp_w_up, exp_b_up, exp_w_down, exp_b_down, ln2_g, ln2_b):
    batch, seq, d = x.shape
    depth = w_in.shape[0]
    t = batch * seq
    dn_alpha = (2 * depth) ** 0.25
    gk, gv = GLA_HEADS * GLA_DK, GLA_HEADS * GLA_DV
    dq = DIFF_HEADS * 2 * DIFF_DH
    splits = (gk, gk, gv, gv, GLA_RANK, GLA_RANK, dq, dq, dq, d, d)
    cuts = [0]
    for s in splits:
        cuts.append(cuts[-1] + s)
    col = lambda w, i: w[:, cuts[i]:cuts[i + 1]]
    tabs = _rope_tables(seq)
    tb = _moe_block_rows(t)
    n_slots = t * TOP_K
    n_rows = n_slots + N_EXPERTS * tb
    nb = n_rows // tb

    assert depth == 1, depth
    cur = x.reshape(t, d)
    g_in, b_in = ln0_g, ln0_b
    for l in range(depth):
        lam_init = 0.8 - 0.6 * math.exp(-0.3 * l)
        w = w_in[l]
        wa = jnp.concatenate([col(w, 0), col(w, 1), col(w, 2), col(w, 6), col(w, 7), col(w, 8)], axis=1).astype(BF16)
        wb = jnp.concatenate([col(w, 3), col(w, 9), col(w, 10), col(w, 4), col(w, 5),
                              jnp.zeros((d, LANES - 2 * GLA_RANK), F32)], axis=1).astype(BF16)
        pa, h = _ln_proj(cur, g_in, b_in, wa, BF16, 512, True, "ln_proj_bf16")
        (pb,) = _ln_proj(cur, g_in, b_in, wb, F32, 640, False, "ln_proj_f32")

        kgt = pa[:, gk:2 * gk].reshape(batch, seq, gk).transpose(0, 2, 1)
        lrt = pb[:, 3 * d:3 * d + 2 * GLA_RANK].reshape(batch, seq, 2 * GLA_RANK).transpose(0, 2, 1)
        zr = jnp.zeros((GLA_RANK, gk), F32)
        pad = jnp.zeros((LANES - 2 * GLA_RANK, gk), F32)
        wf_f = jnp.concatenate([gla_wf_fwd[l], zr, pad], axis=0)
        wf_b = jnp.concatenate([zr, gla_wf_bwd[l], pad], axis=0)
        o_f = _gla(pa, kgt, pb, lrt, wf_f, wf_f[:2 * GLA_RANK].T, gla_bf_fwd[l].reshape(1, gk),
                   gla_bf_fwd[l].reshape(gk, 1), batch, seq, rev=False)
        o_b = _gla(pa, kgt, pb, lrt, wf_b, wf_b[:2 * GLA_RANK].T, gla_bf_bwd[l].reshape(1, gk),
                   gla_bf_bwd[l].reshape(gk, 1), batch, seq, rev=True)

        o_diff = _diff_attn(pa, tabs, diff_lq1[l], diff_lk1[l], diff_lq2[l], diff_lk2[l], diff_norm_g[l],
                            batch, seq, lam_init)

        rw = jnp.concatenate([router_w[l], jnp.zeros((d, LANES - N_EXPERTS), F32)], axis=1)
        rb = jnp.concatenate([router_b[l], jnp.full((LANES - N_EXPERTS,), -jnp.inf, F32)]).reshape(1, LANES)
        h1, route, counts = _merge(
            o_f, o_b, pb, o_diff, h, jnp.tile(gla_norm_g[l], GLA_HEADS).reshape(1, gv),
            w_br_gla[l].astype(BF16), w_br_diff[l].astype(BF16), w_out[l].astype(BF16),
            ln1_g[l].reshape(1, d), ln1_b[l].reshape(1, d), rw, rb, dn_alpha)

        cnt = counts[0, :N_EXPERTS].astype(jnp.int32)
        padded = ((cnt + tb - 1) // tb) * tb
        pend = jnp.cumsum(padded)
        pstart = pend - padded
        top_e = route[:, :TOP_K].astype(jnp.int32)
        rank = route[:, 2 * TOP_K:3 * TOP_K].astype(jnp.int32)
        dest = (pstart[top_e] + rank).reshape(n_slots)
        n_used = (pend[-1] // tb).astype(jnp.int32)
        blk = jnp.minimum(jnp.arange(nb, dtype=jnp.int32), n_used - 1)
        blk_e = jnp.minimum(jnp.sum((pend[None, :] <= (blk * tb)[:, None]).astype(jnp.int32), axis=1),
                            N_EXPERTS - 1)

        xb = _dispatch(dest, h1, n_rows)
        yb = _experts(blk_e, n_used.reshape(1), xb, exp_w_gate[l], exp_b_gate[l], exp_w_up[l], exp_b_up[l],
                      exp_w_down[l], exp_b_down[l], tb)
        cur = _combine(dest, yb, h1, route, ln2_g[l], ln2_b[l], dn_alpha)
    return cur.reshape(batch, seq, d)
```

```python
import functools
import math

import jax
import jax.numpy as jnp
from jax import lax
from jax.experimental import pallas as pl
from jax.experimental.pallas import tpu as pltpu

F32 = jnp.float32
BF16 = jnp.bfloat16
HIGHEST = lax.Precision.HIGHEST

GLA_HEADS = 4
GLA_DK = 128
GLA_DV = 256
GLA_RANK = 16
GLA_TAU = 16.0
GLA_CHUNK = 64
DIFF_HEADS = 8
DIFF_DH = 64
ROT_DIM = DIFF_DH // 4
ROPE_THETA = 500000.0
N_EXPERTS = 32
TOP_K = 4
SWIGLU_LIMIT = 7.0
SWIGLU_ALPHA = 1.702
EPS = 1e-5

LANES = 128
SUBLANES = 8
VMEM_LIMIT = 48 * 1024 * 1024

ROUTE_E = 0
ROUTE_G = TOP_K
ROUTE_R = 2 * TOP_K
MOE_SEG = SUBLANES


def _cparams(sem):
    return pltpu.CompilerParams(dimension_semantics=sem, vmem_limit_bytes=VMEM_LIMIT)


def _layer_norm(x, g, b):
    mu = jnp.mean(x, axis=-1, keepdims=True)
    xc = x - mu
    var = jnp.mean(xc * xc, axis=-1, keepdims=True)
    return xc * lax.rsqrt(var + EPS) * g + b


def _sigmoid(x):
    return 1.0 / (1.0 + jnp.exp(-x))


def _log_sigmoid(x):
    return jnp.minimum(x, 0.0) - jnp.log(1.0 + jnp.exp(-jnp.abs(x)))


def _lane_tile(x, n):
    return jnp.concatenate([x] * n, axis=1)


def _split3(x):
    hi = x.astype(BF16)
    r1 = x - hi.astype(F32)
    mid = r1.astype(BF16)
    lo = (r1 - mid.astype(F32)).astype(BF16)
    return hi, mid, lo


def _ln_proj_kernel(x_ref, g_ref, b_ref, wa_ref, wb_ref, pa_ref, pb_ref, h_ref, *, tna, tnb):
    h = _layer_norm(x_ref[...], g_ref[...], b_ref[...])
    h_ref[...] = h
    h16 = h.astype(BF16)
    for w_ref, o_ref, tn in ((wa_ref, pa_ref, tna), (wb_ref, pb_ref, tnb)):
        for j in range(w_ref.shape[1] // tn):
            cols = slice(j * tn, (j + 1) * tn)
            o_ref[:, cols] = jnp.dot(h16, w_ref[:, cols], preferred_element_type=F32).astype(o_ref.dtype)


def _ln_proj(x, g, b, wa16, wb16, tna, tnb):
    t, d = x.shape
    na, nb = wa16.shape[1], wb16.shape[1]
    tm = min(256, t)
    row = lambda n: pl.BlockSpec((tm, n), lambda i: (i, 0))
    full = lambda r, n: pl.BlockSpec((r, n), lambda i: (0, 0))
    return pl.pallas_call(
        functools.partial(_ln_proj_kernel, tna=tna, tnb=tnb),
        out_shape=[jax.ShapeDtypeStruct((t, na), BF16), jax.ShapeDtypeStruct((t, nb), F32),
                   jax.ShapeDtypeStruct((t, d), F32)],
        grid=(t // tm,),
        in_specs=[row(d), full(1, d), full(1, d), full(d, na), full(d, nb)],
        out_specs=[row(na), row(nb), row(d)],
        compiler_params=_cparams(("parallel",)),
        name="ln_proj",
    )(x, g.reshape(1, d), b.reshape(1, d), wa16, wb16)


def _gla_kernel(q_ref, kt_ref, v_ref, lr_ref, lrt_ref, wf_ref, wft_ref, bf_ref, bft_ref,
                o_ref, s_ref, *, rev, n_chunks):
    c_len = GLA_CHUNK

    @pl.when(pl.program_id(2) == 0)
    def _():
        s_ref[...] = jnp.zeros_like(s_ref)

    z = jnp.dot(lr_ref[...].astype(BF16), wf_ref[...].astype(BF16), preferred_element_type=F32) + bf_ref[...]
    logf = _log_sigmoid(z) * (1.0 / GLA_TAU)
    zt = jnp.dot(wft_ref[...].astype(BF16), lrt_ref[...].astype(BF16), preferred_element_type=F32) + bft_ref[...]
    logft = _log_sigmoid(zt) * (1.0 / GLA_TAU)

    r = n_chunks * c_len
    shift = c_len.bit_length() - 1
    ri = lax.broadcasted_iota(jnp.int32, (r, r), 0)
    ci = lax.broadcasted_iota(jnp.int32, (r, r), 1)
    same = jnp.right_shift(ri, shift) == jnp.right_shift(ci, shift)
    lower = same & (ri >= ci)
    upper = same & (ri <= ci)
    keep = upper if rev else lower
    tri = keep.astype(BF16)
    tri_t = (lower if rev else upper).astype(BF16)
    blk = same.astype(BF16)
    q_scale = GLA_DK ** -0.5

    b = sum(jnp.dot(tri, part, preferred_element_type=F32) for part in _split3(logf))
    bts = sum(jnp.dot(part, jnp.concatenate([tri_t, blk], axis=1), preferred_element_type=F32)
              for part in _split3(logft))
    bt, bt_tot = bts[:, :r], bts[:, r:]
    q_in = (q_ref[...].astype(F32) * q_scale * jnp.exp(b)).astype(BF16)
    kt = kt_ref[...].astype(F32)
    k_in_t = (kt * jnp.exp(-bt)).astype(BF16)
    k_st_t = (kt * jnp.exp(bt_tot - bt)).astype(BF16)
    decay = jnp.exp(bt_tot)
    v = v_ref[...]
    att = jnp.dot(q_in, k_in_t, preferred_element_type=F32)
    att = jnp.where(keep, att, 0.0).astype(BF16)
    o_intra = jnp.dot(att, v, preferred_element_type=F32)

    order = range(n_chunks - 1, -1, -1) if rev else range(n_chunks)
    for c in order:
        rows = slice(c * c_len, (c + 1) * c_len)
        state = s_ref[...]
        o_ref[rows, :] = o_intra[rows, :] + jnp.dot(q_in[rows, :], state.astype(BF16),
                                                     preferred_element_type=F32)
        s_ref[...] = (decay[:, c * c_len:c * c_len + 1] * state
                      + jnp.dot(k_st_t[:, rows], v[rows, :], preferred_element_type=F32))


def _gla(pa, kgt, pb, lrt, wf_pad, wft_pad, bf, bft, batch, seq, rev):
    t = pa.shape[0]
    n_chunks = 4 if seq % (4 * GLA_CHUNK) == 0 else 1
    r = n_chunks * GLA_CHUNK
    ng = seq // r
    dk, dv, h = GLA_DK, GLA_DV, GLA_HEADS
    v_off = (2 * h * dk) // dv
    lr_blk = (3 * h * dv) // LANES

    def gi(g):
        return ng - 1 - g if rev else g

    return pl.pallas_call(
        functools.partial(_gla_kernel, rev=rev, n_chunks=n_chunks),
        out_shape=jax.ShapeDtypeStruct((t, h * dv), F32),
        grid=(batch, h, ng),
        in_specs=[
            pl.BlockSpec((r, dk), lambda b, hh, g: (b * ng + gi(g), hh)),
            pl.BlockSpec((None, dk, r), lambda b, hh, g: (b, hh, gi(g))),
            pl.BlockSpec((r, dv), lambda b, hh, g: (b * ng + gi(g), v_off + hh)),
            pl.BlockSpec((r, LANES), lambda b, hh, g: (b * ng + gi(g), lr_blk)),
            pl.BlockSpec((None, 2 * GLA_RANK, r), lambda b, hh, g: (b, 0, gi(g))),
            pl.BlockSpec((LANES, dk), lambda b, hh, g: (0, hh)),
            pl.BlockSpec((dk, 2 * GLA_RANK), lambda b, hh, g: (hh, 0)),
            pl.BlockSpec((1, dk), lambda b, hh, g: (0, hh)),
            pl.BlockSpec((dk, 1), lambda b, hh, g: (hh, 0)),
        ],
        out_specs=pl.BlockSpec((r, dv), lambda b, hh, g: (b * ng + gi(g), hh)),
        scratch_shapes=[pltpu.VMEM((dk, dv), F32)],
        compiler_params=_cparams(("parallel", "parallel", "arbitrary")),
        name="gla_rev" if rev else "gla_fwd",
    )(pa, kgt, pa, pb, lrt, wf_pad, wft_pad, bf, bft)


def _rope(x, c, a, b):
    return (x * c + pltpu.roll(x, LANES - ROT_DIM // 2, 1) * a + pltpu.roll(x, ROT_DIM // 2, 1) * b)


def _diff_attn_kernel(q_ref, k_ref, v_ref, ck_ref, ak_ref, bk_ref, cq_ref, aq_ref, bq_ref,
                      lq1_ref, lk1_ref, lq2_ref, lk2_ref, g_ref, o_ref,
                      kt_s, ve_s, q_s, m_s, acc_s, sa_s, sb_s, *, seq, tk, lam_init):
    dh = DIFF_DH
    hw = 2 * dh

    @pl.when(pl.program_id(2) == 0)
    def _():
        def body(i, carry):
            rows = pl.ds(pl.multiple_of(i * tk, tk), tk)
            kc = _rope(k_ref[rows, :].astype(F32), ck_ref[rows, :], ak_ref[rows, :], bk_ref[rows, :])
            kt_s[:, rows] = kc.T.astype(BF16)
            ve_s[rows, :] = jnp.concatenate([v_ref[rows, :], jnp.ones((tk, hw), BF16)], axis=1)
            return carry
        lax.fori_loop(0, seq // tk, body, 0)

    q = _rope(q_ref[...].astype(F32), cq_ref[...], aq_ref[...], bq_ref[...])
    q_s[...] = (q * (dh ** -0.5 * math.log2(math.e))).astype(BF16)
    m_s[...] = jnp.full_like(m_s, -jnp.inf)
    acc_s[...] = jnp.zeros_like(acc_s)

    def scores(i, s_buf):
        cols = pl.ds(pl.multiple_of(i * tk, tk), tk)
        kt = kt_s[:, cols]
        for p in range(2):
            s_buf[p] = jnp.dot(q_s[:, p * dh:(p + 1) * dh], kt[p * dh:(p + 1) * dh, :],
                               preferred_element_type=F32)

    def softmax_pv(i, s_buf):
        ve = ve_s[pl.ds(pl.multiple_of(i * tk, tk), tk), :]
        for p in range(2):
            s = s_buf[p]
            m_old = m_s[p]
            m_new = jnp.maximum(m_old, jnp.max(s, axis=-1, keepdims=True))
            alpha = jnp.exp2(m_old - m_new)
            pr = jnp.exp2(s - _lane_tile(m_new, tk // LANES))
            acc_s[p] = (_lane_tile(alpha, 2) * acc_s[p]
                        + jnp.dot(pr.astype(BF16), ve, preferred_element_type=F32))
            m_s[p] = m_new

    n_t = seq // tk
    scores(0, sa_s)

    def pair_body(j, carry):
        scores(2 * j + 1, sb_s)
        softmax_pv(2 * j, sa_s)
        scores(2 * j + 2, sa_s)
        softmax_pv(2 * j + 1, sb_s)
        return carry
    lax.fori_loop(0, (n_t - 1) // 2, pair_body, 0)
    if n_t % 2 == 0:
        scores(n_t - 1, sb_s)
        softmax_pv(n_t - 2, sa_s)
        softmax_pv(n_t - 1, sb_s)
    else:
        softmax_pv(n_t - 1, sa_s)

    lam = (jnp.exp(jnp.sum(lq1_ref[...] * lk1_ref[...], axis=-1, keepdims=True))
           - jnp.exp(jnp.sum(lq2_ref[...] * lk2_ref[...], axis=-1, keepdims=True)) + lam_init)
    o = acc_s[0, :, :hw] / acc_s[0, :, hw:] - lam * (acc_s[1, :, :hw] / acc_s[1, :, hw:])
    ms = jnp.mean(o * o, axis=-1, keepdims=True)
    o_ref[...] = (o * lax.rsqrt(ms + EPS) * g_ref[...] * (1.0 - lam_init)).astype(o_ref.dtype)


def _diff_attn(pa, tabs, lq1, lk1, lq2, lk2, norm_g, batch, seq, lam_init):
    t = pa.shape[0]
    h, dh = DIFF_HEADS, DIFF_DH
    hw = 2 * dh
    tq = min(256, seq)
    tk = min(512, seq)
    nq = seq // tq
    q_off = (2 * GLA_HEADS * GLA_DK + GLA_HEADS * GLA_DV) // hw
    k_off = q_off + h
    v_off = k_off + h
    vec = lambda n: pl.BlockSpec((1, n), lambda b, hh, qi: (0, 0))
    return pl.pallas_call(
        functools.partial(_diff_attn_kernel, seq=seq, tk=tk, lam_init=lam_init),
        out_shape=jax.ShapeDtypeStruct((t, h * hw), BF16),
        grid=(batch, h, nq),
        in_specs=[
            pl.BlockSpec((tq, hw), lambda b, hh, qi: (b * nq + qi, q_off + hh)),
            pl.BlockSpec((seq, hw), lambda b, hh, qi: (b, k_off + hh)),
            pl.BlockSpec((seq, hw), lambda b, hh, qi: (b, v_off + hh)),
            pl.BlockSpec((seq, hw), lambda b, hh, qi: (0, 0)),
            pl.BlockSpec((seq, hw), lambda b, hh, qi: (0, 0)),
            pl.BlockSpec((seq, hw), lambda b, hh, qi: (0, 0)),
            pl.BlockSpec((tq, hw), lambda b, hh, qi: (qi, 0)),
            pl.BlockSpec((tq, hw), lambda b, hh, qi: (qi, 0)),
            pl.BlockSpec((tq, hw), lambda b, hh, qi: (qi, 0)),
            vec(dh), vec(dh), vec(dh), vec(dh), vec(hw),
        ],
        out_specs=pl.BlockSpec((tq, hw), lambda b, hh, qi: (b * nq + qi, hh)),
        scratch_shapes=[
            pltpu.VMEM((hw, seq), BF16),
            pltpu.VMEM((seq, 2 * hw), BF16),
            pltpu.VMEM((tq, hw), BF16),
            pltpu.VMEM((2, tq, LANES), F32),
            pltpu.VMEM((2, tq, 2 * hw), F32),
            pltpu.VMEM((2, tq, tk), F32),
            pltpu.VMEM((2, tq, tk), F32),
        ],
        compiler_params=_cparams(("parallel", "parallel", "arbitrary")),
        name="diff_attn",
    )(pa, pa, pa, *tabs, *tabs, lq1.reshape(1, dh), lk1.reshape(1, dh), lq2.reshape(1, dh),
      lk2.reshape(1, dh), norm_g.reshape(1, hw))


def _rope_tables(seq):
    half = ROT_DIM // 2
    pos = jnp.arange(seq, dtype=F32)
    inv_freq = jnp.power(ROPE_THETA, -jnp.arange(0, ROT_DIM, 2, dtype=F32) / ROT_DIM)
    ang = pos[:, None] * inv_freq[None, :]
    cos, sin = jnp.cos(ang), jnp.sin(ang)
    ones = jnp.ones((seq, DIFF_DH - ROT_DIM), F32)
    zeros = jnp.zeros((seq, DIFF_DH - ROT_DIM), F32)
    zh = jnp.zeros((seq, half), F32)
    c = jnp.concatenate([cos, cos, ones], axis=1)
    a = jnp.concatenate([-sin, zh, zeros], axis=1)
    b = jnp.concatenate([zh, sin, zeros], axis=1)
    return tuple(jnp.tile(m, (1, 2)) for m in (c, a, b))


def _merge_kernel(of_ref, ob_ref, pb_rg, od_ref, pb_gg, pb_gd, h_ref, gn_ref, wa_ref, wb_ref, wo_ref,
                  g1_ref, b1_ref, rwh_ref, rwl_ref, rb_ref, h1_ref, route_ref, routet_ref, cnt_ref,
                  *, dn_alpha):
    tm = of_ref.shape[0]
    og = of_ref[...] + ob_ref[...]
    parts = []
    for hh in range(GLA_HEADS):
        xh = og[:, hh * GLA_DV:(hh + 1) * GLA_DV]
        ms = jnp.mean(xh * xh, axis=-1, keepdims=True)
        parts.append(xh * lax.rsqrt(ms + EPS))
    rg = pb_rg[...]
    o_gla = jnp.concatenate(parts, axis=1) * gn_ref[...] * (rg * _sigmoid(rg))
    a = jnp.dot(o_gla.astype(BF16), wa_ref[...], preferred_element_type=F32)
    bb = jnp.dot(od_ref[...], wb_ref[...], preferred_element_type=F32)
    mixed = _sigmoid(pb_gg[...]) * a + _sigmoid(pb_gd[...]) * bb
    mix = jnp.dot(mixed.astype(BF16), wo_ref[...], preferred_element_type=F32)
    h1 = _layer_norm(dn_alpha * h_ref[...] + mix, g1_ref[...], b1_ref[...])
    h1_ref[...] = h1

    hh = h1.astype(BF16)
    hl = (h1 - hh.astype(F32)).astype(BF16)
    logits = (jnp.dot(hh, rwh_ref[...], preferred_element_type=F32)
              + jnp.dot(hh, rwl_ref[...], preferred_element_type=F32)
              + jnp.dot(hl, rwh_ref[...], preferred_element_type=F32)) + rb_ref[...]
    lane = lax.broadcasted_iota(jnp.int32, (tm, LANES), 1).astype(F32)
    work = logits
    top_v, top_i, sels = [], [], []
    for _ in range(TOP_K):
        m = jnp.max(work, axis=-1, keepdims=True)
        idx = jnp.min(jnp.where(work == m, lane, float(LANES)), axis=-1, keepdims=True)
        sel = lane == idx
        top_v.append(m)
        top_i.append(idx)
        sels.append(sel)
        work = jnp.where(sel, -jnp.inf, work)
    exps = [jnp.exp(v - top_v[0]) for v in top_v]
    denom = exps[0]
    for e in exps[1:]:
        denom = denom + e
    onehot = sels[0].astype(F32)
    for s in sels[1:]:
        onehot = onehot + s.astype(F32)
    ri = lax.broadcasted_iota(jnp.int32, (tm, tm), 0)
    ci = lax.broadcasted_iota(jnp.int32, (tm, tm), 1)
    strict = (ci < ri).astype(BF16)
    before = jnp.dot(strict, onehot.astype(BF16), preferred_element_type=F32)
    route = jnp.zeros((tm, LANES), F32)
    for k in range(TOP_K):
        rank = jnp.sum(jnp.where(sels[k], before, 0.0), axis=-1, keepdims=True)
        route = jnp.where(lane == float(ROUTE_E + k), top_i[k], route)
        route = jnp.where(lane == float(ROUTE_G + k), exps[k] / denom, route)
        route = jnp.where(lane == float(ROUTE_R + k), rank, route)
    route_ref[...] = route
    routet_ref[...] = route.T
    cnt_ref[...] = jnp.sum(onehot, axis=0, keepdims=True)


def _merge(o_f, o_b, pb, o_diff, h, gn, wa, wb, wo, g1, b1, rw, rb, dn_alpha):
    t, d = h.shape
    tm = _moe_tile(t)
    row = lambda c: pl.BlockSpec((tm, d), lambda i: (i, c))
    full = lambda shp: pl.BlockSpec(shp, lambda i: (0, 0))
    rwh = rw.astype(BF16)
    rwl = (rw - rwh.astype(F32)).astype(BF16)
    return pl.pallas_call(
        functools.partial(_merge_kernel, dn_alpha=dn_alpha),
        out_shape=[jax.ShapeDtypeStruct((t, d), F32),
                   jax.ShapeDtypeStruct((t, LANES), F32),
                   jax.ShapeDtypeStruct((LANES, t), F32),
                   jax.ShapeDtypeStruct((t // tm, 1, LANES), F32)],
        grid=(t // tm,),
        in_specs=[row(0), row(0), row(0), row(0), row(1), row(2), row(0),
                  full((1, d)), full((d, d)), full((d, d)), full((d, d)),
                  full((1, d)), full((1, d)), full((d, LANES)), full((d, LANES)), full((1, LANES))],
        out_specs=[pl.BlockSpec((tm, d), lambda i: (i, 0)),
                   pl.BlockSpec((tm, LANES), lambda i: (i, 0)),
                   pl.BlockSpec((LANES, tm), lambda i: (0, i)),
                   pl.BlockSpec((None, 1, LANES), lambda i: (i, 0, 0))],
        compiler_params=_cparams(("parallel",)),
        name="merge_ln1_router",
    )(o_f, o_b, pb, o_diff, pb, pb, h, gn, wa, wb, wo, g1, b1, rwh, rwl, rb)


def _moe_tile(t):
    return min(256, t)


def _moe_slots(tm):
    return tm * TOP_K + N_EXPERTS * MOE_SEG


def _piece_loops(j, lo_s, gb_s, np_s, make_copy):
    def per_expert(e, carry):
        idx = j * N_EXPERTS + e
        base, g = lo_s[idx], gb_s[idx]

        def per_piece(p, c):
            off = p * MOE_SEG
            make_copy(pl.multiple_of(base + off, MOE_SEG), pl.multiple_of(g + off, MOE_SEG)).start()
            return c
        lax.fori_loop(0, np_s[idx], per_piece, 0)
        return carry
    lax.fori_loop(0, N_EXPERTS, per_expert, 0)


def _wait_pieces(n, make_copy):
    def body(p, c):
        make_copy(0, 0).wait()
        return c
    lax.fori_loop(0, n, body, 0)


def _dispatch_kernel(lo_s, gb_s, np_s, npt_s, ts_s, tnp_s, h_ref, routet_ref, locol_ref, xb_ref,
                     sorted_s, zero_s, sem, zsem, *, nt):
    tm = h_ref.shape[0]
    n_slots = sorted_s.shape[1]
    j = pl.program_id(0)
    par = j % 2

    def seg_copy(pp):
        def make(local_row, global_row):
            return pltpu.make_async_copy(sorted_s.at[pp, pl.ds(local_row, MOE_SEG)],
                                         xb_ref.at[pl.ds(global_row, MOE_SEG)], sem.at[pp])
        return make

    def zero_copy(_, global_row):
        return pltpu.make_async_copy(zero_s, xb_ref.at[pl.ds(global_row, MOE_SEG)], zsem)

    @pl.when(j == 0)
    def _():
        zero_s[...] = jnp.zeros_like(zero_s)

        def per_region(e, carry):
            def per_piece(p, c):
                zero_copy(0, pl.multiple_of(ts_s[e] + p * MOE_SEG, MOE_SEG)).start()
                return c
            lax.fori_loop(0, tnp_s[e], per_piece, 0)
            return carry
        lax.fori_loop(0, N_EXPERTS + 1, per_region, 0)
        _wait_pieces(tnp_s[N_EXPERTS + 1], zero_copy)

    routet = routet_ref[...]
    sub = lax.broadcasted_iota(jnp.int32, (LANES, tm), 0).astype(F32)
    slot_iota = lax.broadcasted_iota(jnp.int32, (n_slots, tm), 0).astype(F32)
    perm = jnp.zeros((n_slots, tm), F32)
    for k in range(TOP_K):
        e_row = routet[ROUTE_E + k:ROUTE_E + k + 1, :]
        lo_k = jnp.sum(jnp.where(sub == e_row, locol_ref[...], 0.0), axis=0, keepdims=True)
        slot = lo_k + routet[ROUTE_R + k:ROUTE_R + k + 1, :]
        perm = jnp.where(slot_iota == slot, 1.0, perm)
    sorted_s[par] = jnp.dot(perm.astype(BF16), h_ref[...].astype(BF16), preferred_element_type=F32)

    _piece_loops(j, lo_s, gb_s, np_s, seg_copy(par))

    @pl.when(j > 0)
    def _():
        _wait_pieces(npt_s[j - 1], seg_copy(1 - par))

    @pl.when(j == nt - 1)
    def _():
        _wait_pieces(npt_s[j], seg_copy(par))


def _dispatch(tables, h1, routet, lo_col, n_rows):
    t, d = h1.shape
    tm = _moe_tile(t)
    nt = t // tm
    n_slots = _moe_slots(tm)
    return pl.pallas_call(
        functools.partial(_dispatch_kernel, nt=nt),
        out_shape=jax.ShapeDtypeStruct((n_rows, d), F32),
        grid_spec=pltpu.PrefetchScalarGridSpec(
            num_scalar_prefetch=len(tables),
            grid=(nt,),
            in_specs=[
                pl.BlockSpec((tm, d), lambda i, *_: (i, 0)),
                pl.BlockSpec((LANES, tm), lambda i, *_: (0, i)),
                pl.BlockSpec((None, LANES, 1), lambda i, *_: (i, 0, 0)),
            ],
            out_specs=pl.BlockSpec(memory_space=pl.ANY),
            scratch_shapes=[pltpu.VMEM((2, n_slots, d), F32), pltpu.VMEM((MOE_SEG, d), F32),
                            pltpu.SemaphoreType.DMA((2,)), pltpu.SemaphoreType.DMA],
        ),
        compiler_params=_cparams(("arbitrary",)),
        name="moe_dispatch",
    )(*tables, h1, routet, lo_col)


def _experts_kernel(be_ref, nu_ref, x_ref, wg_ref, bg_ref, wu_ref, bu_ref, wd_ref, bd_ref, y_ref,
                    wg_s, wu_s, wd_s):
    i = pl.program_id(0)

    @pl.when(i >= nu_ref[0])
    def _():
        y_ref[...] = jnp.zeros_like(y_ref)

    @pl.when(i < nu_ref[0])
    def _():
        prev = be_ref[jnp.maximum(i - 1, 0)]

        @pl.when((i == 0) | (be_ref[i] != prev))
        def _():
            wg_s[...] = wg_ref[...].astype(BF16)
            wu_s[...] = wu_ref[...].astype(BF16)
            wd_s[...] = wd_ref[...].astype(BF16)

        x = x_ref[...].astype(BF16)
        g = jnp.minimum(jnp.dot(x, wg_s[...], preferred_element_type=F32) + bg_ref[...], SWIGLU_LIMIT)
        u = jnp.clip(jnp.dot(x, wu_s[...], preferred_element_type=F32) + bu_ref[...],
                     -SWIGLU_LIMIT, SWIGLU_LIMIT)
        a = g * _sigmoid(SWIGLU_ALPHA * g) * (u + 1.0)
        y_ref[...] = jnp.dot(a.astype(BF16), wd_s[...], preferred_element_type=F32) + bd_ref[...]


def _experts(blk_e, n_used, xb, wg, bg, wu, bu, wd, bd, tb):
    p, d = xb.shape
    ne, _, dff = wg.shape
    nb = p // tb
    rowblk = lambda i, be, nu: (jnp.maximum(jnp.minimum(i, nu[0] - 1), 0), 0)
    wspec = lambda a, b: pl.BlockSpec((None, a, b), lambda i, be, nu: (be[i], 0, 0))
    return pl.pallas_call(
        _experts_kernel,
        out_shape=jax.ShapeDtypeStruct((p, d), F32),
        grid_spec=pltpu.PrefetchScalarGridSpec(
            num_scalar_prefetch=2,
            grid=(nb,),
            in_specs=[pl.BlockSpec((tb, d), rowblk),
                      wspec(d, dff), wspec(1, dff), wspec(d, dff), wspec(1, dff),
                      wspec(dff, d), wspec(1, d)],
            out_specs=pl.BlockSpec((tb, d), lambda i, be, nu: (i, 0)),
            scratch_shapes=[pltpu.VMEM((d, dff), BF16), pltpu.VMEM((d, dff), BF16),
                            pltpu.VMEM((dff, d), BF16)],
        ),
        compiler_params=_cparams(("arbitrary",)),
        name="moe_experts",
    )(blk_e, n_used, xb, wg, bg.reshape(ne, 1, dff), wu, bu.reshape(ne, 1, dff), wd, bd.reshape(ne, 1, d))


def _combine_kernel(lo_s, gb_s, np_s, npt_s, yb_ref, h_ref, route_ref, lorow_ref, g_ref, b_ref, o_ref,
                    ybuf, sem, *, dn_alpha, nt):
    tm = h_ref.shape[0]
    n_slots = ybuf.shape[1]
    j = pl.program_id(0)
    par = j % 2

    def seg_copy(pp):
        def make(local_row, global_row):
            return pltpu.make_async_copy(yb_ref.at[pl.ds(global_row, MOE_SEG)],
                                         ybuf.at[pp, pl.ds(local_row, MOE_SEG)], sem.at[pp])
        return make

    @pl.when(j == 0)
    def _():
        ybuf[...] = jnp.zeros_like(ybuf)
        _piece_loops(j, lo_s, gb_s, np_s, seg_copy(0))

    @pl.when(j + 1 < nt)
    def _():
        _piece_loops(j + 1, lo_s, gb_s, np_s, seg_copy(1 - par))

    _wait_pieces(npt_s[j], seg_copy(par))

    route = route_ref[...]
    lane = lax.broadcasted_iota(jnp.int32, (tm, LANES), 1).astype(F32)
    slot_iota = lax.broadcasted_iota(jnp.int32, (tm, n_slots), 1).astype(F32)
    wgt = jnp.zeros((tm, n_slots), F32)
    for k in range(TOP_K):
        e_col = route[:, ROUTE_E + k:ROUTE_E + k + 1]
        lo_k = jnp.sum(jnp.where(lane == e_col, lorow_ref[...], 0.0), axis=-1, keepdims=True)
        slot = lo_k + route[:, ROUTE_R + k:ROUTE_R + k + 1]
        wgt = jnp.where(slot_iota == slot, route[:, ROUTE_G + k:ROUTE_G + k + 1], wgt)
    y = ybuf[par]
    wh = wgt.astype(BF16)
    wl = (wgt - wh.astype(F32)).astype(BF16)
    yh = y.astype(BF16)
    yl = (y - yh.astype(F32)).astype(BF16)
    ffn = (jnp.dot(wh, yh, preferred_element_type=F32) + jnp.dot(wh, yl, preferred_element_type=F32)
           + jnp.dot(wl, yh, preferred_element_type=F32))
    o_ref[...] = _layer_norm(dn_alpha * h_ref[...] + ffn, g_ref[...], b_ref[...])


def _combine(tables, yb, h1, route, lo_row, g2, b2, dn_alpha):
    t, d = h1.shape
    tm = _moe_tile(t)
    nt = t // tm
    return pl.pallas_call(
        functools.partial(_combine_kernel, dn_alpha=dn_alpha, nt=nt),
        out_shape=jax.ShapeDtypeStruct((t, d), F32),
        grid_spec=pltpu.PrefetchScalarGridSpec(
            num_scalar_prefetch=len(tables),
            grid=(nt,),
            in_specs=[
                pl.BlockSpec(memory_space=pl.ANY),
                pl.BlockSpec((tm, d), lambda i, *_: (i, 0)),
                pl.BlockSpec((tm, LANES), lambda i, *_: (i, 0)),
                pl.BlockSpec((None, 1, LANES), lambda i, *_: (i, 0, 0)),
                pl.BlockSpec((1, d), lambda i, *_: (0, 0)),
                pl.BlockSpec((1, d), lambda i, *_: (0, 0)),
            ],
            out_specs=pl.BlockSpec((tm, d), lambda i, *_: (i, 0)),
            scratch_shapes=[pltpu.VMEM((2, _moe_slots(tm), d), F32), pltpu.SemaphoreType.DMA((2,))],
        ),
        compiler_params=_cparams(("arbitrary",)),
        name="moe_combine_ln2",
    )(*tables, yb, h1, route, lo_row, g2.reshape(1, d), b2.reshape(1, d))


def _moe_tables(counts, tm, tb, n_rows):
    seg = ((counts + MOE_SEG - 1) // MOE_SEG) * MOE_SEG
    lo = jnp.cumsum(seg, axis=1) - seg
    rows_e = jnp.sum(seg, axis=0)
    region = ((rows_e + tb - 1) // tb) * tb
    pend = jnp.cumsum(region)
    pstart = pend - region
    gbase = pstart[None, :] + jnp.cumsum(seg, axis=0) - seg
    pieces = seg // MOE_SEG
    tail_start = jnp.concatenate([pstart + rows_e, pend[-1:]])
    tail_pieces = jnp.concatenate([(region - rows_e) // MOE_SEG, (n_rows - pend[-1:]) // MOE_SEG])
    tail_pieces = jnp.concatenate([tail_pieces, jnp.sum(tail_pieces, keepdims=True)])
    i32 = lambda a: a.reshape(-1).astype(jnp.int32)
    seg_tables = (i32(lo), i32(gbase), i32(pieces), i32(jnp.sum(pieces, axis=1)))
    n_used = (pend[-1] // tb).astype(jnp.int32)
    blk = jnp.minimum(jnp.arange(n_rows // tb, dtype=jnp.int32), n_used - 1)
    blk_e = jnp.minimum(jnp.sum((pend[None, :] <= (blk * tb)[:, None]).astype(jnp.int32), axis=1),
                        N_EXPERTS - 1)
    lo_f = jnp.pad(lo.astype(F32), ((0, 0), (0, LANES - N_EXPERTS)))
    return seg_tables, (i32(tail_start), i32(tail_pieces)), blk_e, n_used.reshape(1), lo_f


def kernel(x, ln0_g, ln0_b, w_in, gla_wf_fwd, gla_bf_fwd, gla_wf_bwd, gla_bf_bwd, gla_norm_g, diff_lq1, diff_lk1, diff_lq2, diff_lk2, diff_norm_g, w_br_gla, w_br_diff, w_out, ln1_g, ln1_b, router_w, router_b, exp_w_gate, exp_b_gate, exp_w_up, exp_b_up, exp_w_down, exp_b_down, ln2_g, ln2_b):
    batch, seq, d = x.shape
    depth = w_in.shape[0]
    t = batch * seq
    dn_alpha = (2 * depth) ** 0.25
    gk, gv = GLA_HEADS * GLA_DK, GLA_HEADS * GLA_DV
    dq = DIFF_HEADS * 2 * DIFF_DH
    splits = (gk, gk, gv, gv, GLA_RANK, GLA_RANK, dq, dq, dq, d, d)
    cuts = [0]
    for s in splits:
        cuts.append(cuts[-1] + s)
    col = lambda w, i: w[:, cuts[i]:cuts[i + 1]]
    tabs = _rope_tables(seq)
    tm = _moe_tile(t)
    tb = _moe_tile(t)
    n_rows = t * TOP_K + (t // tm) * N_EXPERTS * (MOE_SEG - 1) + N_EXPERTS * (tb - 1)
    n_rows = ((n_rows + tb - 1) // tb) * tb

    assert depth == 1, depth
    cur = x.reshape(t, d)
    g_in, b_in = ln0_g, ln0_b
    for l in range(depth):
        lam_init = 0.8 - 0.6 * math.exp(-0.3 * l)
        w = w_in[l]
        wa = jnp.concatenate([col(w, 0), col(w, 1), col(w, 2), col(w, 6), col(w, 7), col(w, 8)], axis=1).astype(BF16)
        wb = jnp.concatenate([col(w, 3), col(w, 9), col(w, 10), col(w, 4), col(w, 5),
                              jnp.zeros((d, LANES - 2 * GLA_RANK), F32)], axis=1).astype(BF16)
        pa, pb, h = _ln_proj(cur, g_in, b_in, wa, wb, 512, 640)

        kgt = pa[:, gk:2 * gk].reshape(batch, seq, gk).transpose(0, 2, 1)
        lrt = pb[:, 3 * d:3 * d + 2 * GLA_RANK].reshape(batch, seq, 2 * GLA_RANK).transpose(0, 2, 1)
        zr = jnp.zeros((GLA_RANK, gk), F32)
        pad = jnp.zeros((LANES - 2 * GLA_RANK, gk), F32)
        wf_f = jnp.concatenate([gla_wf_fwd[l], zr, pad], axis=0)
        wf_b = jnp.concatenate([zr, gla_wf_bwd[l], pad], axis=0)
        o_f = _gla(pa, kgt, pb, lrt, wf_f, wf_f[:2 * GLA_RANK].T, gla_bf_fwd[l].reshape(1, gk),
                   gla_bf_fwd[l].reshape(gk, 1), batch, seq, rev=False)
        o_b = _gla(pa, kgt, pb, lrt, wf_b, wf_b[:2 * GLA_RANK].T, gla_bf_bwd[l].reshape(1, gk),
                   gla_bf_bwd[l].reshape(gk, 1), batch, seq, rev=True)

        o_diff = _diff_attn(pa, tabs, diff_lq1[l], diff_lk1[l], diff_lq2[l], diff_lk2[l], diff_norm_g[l],
                            batch, seq, lam_init)

        rw = jnp.concatenate([router_w[l], jnp.zeros((d, LANES - N_EXPERTS), F32)], axis=1)
        rb = jnp.concatenate([router_b[l], jnp.full((LANES - N_EXPERTS,), -jnp.inf, F32)]).reshape(1, LANES)
        h1, route, routet, counts = _merge(
            o_f, o_b, pb, o_diff, h, jnp.tile(gla_norm_g[l], GLA_HEADS).reshape(1, gv),
            w_br_gla[l].astype(BF16), w_br_diff[l].astype(BF16), w_out[l].astype(BF16),
            ln1_g[l].reshape(1, d), ln1_b[l].reshape(1, d), rw, rb, dn_alpha)

        seg_tables, tail_tables, blk_e, n_used, lo_f = _moe_tables(
            counts[:, 0, :N_EXPERTS].astype(jnp.int32), tm, tb, n_rows)
        xb = _dispatch(seg_tables + tail_tables, h1, routet, lo_f[:, :, None], n_rows)
        yb = _experts(blk_e, n_used, xb, exp_w_gate[l], exp_b_gate[l], exp_w_up[l], exp_b_up[l],
                      exp_w_down[l], exp_b_down[l], tb)
        cur = _combine(seg_tables, yb, h1, route, lo_f[:, None, :], ln2_g[l], ln2_b[l], dn_alpha)
    return cur.reshape(batch, seq, d)
```

```python
import functools
import math

import jax
import jax.numpy as jnp
from jax import lax
from jax.experimental import pallas as pl
from jax.experimental.pallas import tpu as pltpu

F32 = jnp.float32
BF16 = jnp.bfloat16
HIGHEST = lax.Precision.HIGHEST

GLA_HEADS = 4
GLA_DK = 128
GLA_DV = 256
GLA_RANK = 16
GLA_TAU = 16.0
GLA_CHUNK = 64
DIFF_HEADS = 8
DIFF_DH = 64
ROT_DIM = DIFF_DH // 4
ROPE_THETA = 500000.0
N_EXPERTS = 32
TOP_K = 4
SWIGLU_LIMIT = 7.0
SWIGLU_ALPHA = 1.702
EPS = 1e-5

LANES = 128
SUBLANES = 8
VMEM_LIMIT = 48 * 1024 * 1024

ROUTE_E = 0
ROUTE_G = TOP_K
ROUTE_R = 2 * TOP_K
MOE_SEG = SUBLANES


def _cparams(sem):
    return pltpu.CompilerParams(dimension_semantics=sem, vmem_limit_bytes=VMEM_LIMIT)


def _layer_norm(x, g, b):
    mu = jnp.mean(x, axis=-1, keepdims=True)
    xc = x - mu
    var = jnp.mean(xc * xc, axis=-1, keepdims=True)
    return xc * lax.rsqrt(var + EPS) * g + b


def _sigmoid(x):
    return 1.0 / (1.0 + jnp.exp(-x))


def _log_sigmoid(x):
    return jnp.minimum(x, 0.0) - jnp.log(1.0 + jnp.exp(-jnp.abs(x)))


def _lane_tile(x, n):
    return jnp.concatenate([x] * n, axis=1)


def _split3(x):
    hi = x.astype(BF16)
    r1 = x - hi.astype(F32)
    mid = r1.astype(BF16)
    lo = (r1 - mid.astype(F32)).astype(BF16)
    return hi, mid, lo


def _rope(x, c, a, b):
    return (x * c + pltpu.roll(x, LANES - ROT_DIM // 2, 1) * a + pltpu.roll(x, ROT_DIM // 2, 1) * b)


def _ln_proj_kernel(x_ref, g_ref, b_ref, wa_ref, wb_ref, c_ref, a_ref, r_ref, pa_ref, pb_ref, h_ref,
                    *, tna, tnb, q_cols, k_cols, q_scale):
    h = _layer_norm(x_ref[...], g_ref[...], b_ref[...])
    h_ref[...] = h
    h16 = h.astype(BF16)
    for j in range(wa_ref.shape[1] // tna):
        res = jnp.dot(h16, wa_ref[:, j * tna:(j + 1) * tna], preferred_element_type=F32)
        for c in range(0, tna, LANES):
            col = j * tna + c
            blk = res[:, c:c + LANES]
            if q_cols[0] <= col < q_cols[1]:
                blk = _rope(blk, c_ref[...], a_ref[...], r_ref[...]) * q_scale
            elif k_cols[0] <= col < k_cols[1]:
                blk = _rope(blk, c_ref[...], a_ref[...], r_ref[...])
            pa_ref[:, col:col + LANES] = blk.astype(pa_ref.dtype)
    for j in range(wb_ref.shape[1] // tnb):
        cols = slice(j * tnb, (j + 1) * tnb)
        pb_ref[:, cols] = jnp.dot(h16, wb_ref[:, cols], preferred_element_type=F32)


def _ln_proj(x, g, b, wa16, wb16, tabs, seq, tna, tnb, q_cols, k_cols, q_scale):
    t, d = x.shape
    na, nb = wa16.shape[1], wb16.shape[1]
    tm = min(256, t)
    nrow = seq // tm
    row = lambda n: pl.BlockSpec((tm, n), lambda i: (i, 0))
    full = lambda r, n: pl.BlockSpec((r, n), lambda i: (0, 0))
    tab = pl.BlockSpec((tm, LANES), lambda i: (i % nrow, 0))
    return pl.pallas_call(
        functools.partial(_ln_proj_kernel, tna=tna, tnb=tnb, q_cols=q_cols, k_cols=k_cols, q_scale=q_scale),
        out_shape=[jax.ShapeDtypeStruct((t, na), BF16), jax.ShapeDtypeStruct((t, nb), F32),
                   jax.ShapeDtypeStruct((t, d), F32)],
        grid=(t // tm,),
        in_specs=[row(d), full(1, d), full(1, d), full(d, na), full(d, nb), tab, tab, tab],
        out_specs=[row(na), row(nb), row(d)],
        compiler_params=_cparams(("parallel",)),
        name="ln_proj",
    )(x, g.reshape(1, d), b.reshape(1, d), wa16, wb16, *tabs)


def _gla_kernel(*refs, n_chunks):
    ins, (of_ref, ob_ref, sf_ref, sb_ref) = refs[:-4], refs[-4:]
    n_in = len(ins) // 2

    @pl.when(pl.program_id(2) == 0)
    def _():
        sf_ref[...] = jnp.zeros_like(sf_ref)
        sb_ref[...] = jnp.zeros_like(sb_ref)

    _gla_chain(*ins[:n_in], of_ref, sf_ref, rev=False, n_chunks=n_chunks)
    _gla_chain(*ins[n_in:], ob_ref, sb_ref, rev=True, n_chunks=n_chunks)


def _gla_chain(q_ref, kt_ref, v_ref, lr_ref, lrt_ref, wf_ref, wft_ref, bf_ref, bft_ref,
               o_ref, s_ref, *, rev, n_chunks):
    c_len = GLA_CHUNK

    z = jnp.dot(lr_ref[...].astype(BF16), wf_ref[...].astype(BF16), preferred_element_type=F32) + bf_ref[...]
    logf = _log_sigmoid(z) * (1.0 / GLA_TAU)
    zt = jnp.dot(wft_ref[...].astype(BF16), lrt_ref[...].astype(BF16), preferred_element_type=F32) + bft_ref[...]
    logft = _log_sigmoid(zt) * (1.0 / GLA_TAU)

    r = n_chunks * c_len
    shift = c_len.bit_length() - 1
    ri = lax.broadcasted_iota(jnp.int32, (r, r), 0)
    ci = lax.broadcasted_iota(jnp.int32, (r, r), 1)
    same = jnp.right_shift(ri, shift) == jnp.right_shift(ci, shift)
    lower = same & (ri >= ci)
    upper = same & (ri <= ci)
    keep = upper if rev else lower
    tri = keep.astype(BF16)
    tri_t = (lower if rev else upper).astype(BF16)
    blk = same.astype(BF16)
    q_scale = GLA_DK ** -0.5

    b = sum(jnp.dot(tri, part, preferred_element_type=F32) for part in _split3(logf))
    bts = sum(jnp.dot(part, jnp.concatenate([tri_t, blk], axis=1), preferred_element_type=F32)
              for part in _split3(logft))
    bt, bt_tot = bts[:, :r], bts[:, r:]
    q_in = (q_ref[...].astype(F32) * q_scale * jnp.exp(b)).astype(BF16)
    kt = kt_ref[...].astype(F32)
    k_in_t = (kt * jnp.exp(-bt)).astype(BF16)
    k_st_t = (kt * jnp.exp(bt_tot - bt)).astype(BF16)
    decay = jnp.exp(bt_tot)
    v = v_ref[...]
    att = jnp.dot(q_in, k_in_t, preferred_element_type=F32)
    att = jnp.where(keep, att, 0.0).astype(BF16)
    o_intra = jnp.dot(att, v, preferred_element_type=F32)

    order = range(n_chunks - 1, -1, -1) if rev else range(n_chunks)
    for c in order:
        rows = slice(c * c_len, (c + 1) * c_len)
        state = s_ref[...]
        o_ref[rows, :] = o_intra[rows, :] + jnp.dot(q_in[rows, :], state.astype(BF16),
                                                     preferred_element_type=F32)
        s_ref[...] = (decay[:, c * c_len:c * c_len + 1] * state
                      + jnp.dot(k_st_t[:, rows], v[rows, :], preferred_element_type=F32))


def _gla(pa, kgt, pb, lrt, gates_f, gates_b, batch, seq):
    t = pa.shape[0]
    n_chunks = 4 if seq % (4 * GLA_CHUNK) == 0 else 1
    r = n_chunks * GLA_CHUNK
    ng = seq // r
    dk, dv, h = GLA_DK, GLA_DV, GLA_HEADS
    v_off = (2 * h * dk) // dv
    lr_blk = (3 * h * dv) // LANES

    def specs(gi):
        return [
            pl.BlockSpec((r, dk), lambda b, hh, g: (b * ng + gi(g), hh)),
            pl.BlockSpec((None, dk, r), lambda b, hh, g: (b, hh, gi(g))),
            pl.BlockSpec((r, dv), lambda b, hh, g: (b * ng + gi(g), v_off + hh)),
            pl.BlockSpec((r, LANES), lambda b, hh, g: (b * ng + gi(g), lr_blk)),
            pl.BlockSpec((None, 2 * GLA_RANK, r), lambda b, hh, g: (b, 0, gi(g))),
            pl.BlockSpec((LANES, dk), lambda b, hh, g: (0, hh)),
            pl.BlockSpec((dk, 2 * GLA_RANK), lambda b, hh, g: (hh, 0)),
            pl.BlockSpec((1, dk), lambda b, hh, g: (0, hh)),
            pl.BlockSpec((dk, 1), lambda b, hh, g: (hh, 0)),
        ]

    fwd = lambda g: g
    bwd = lambda g: ng - 1 - g
    out = lambda gi: pl.BlockSpec((r, dv), lambda b, hh, g: (b * ng + gi(g), hh))
    return pl.pallas_call(
        functools.partial(_gla_kernel, n_chunks=n_chunks),
        out_shape=[jax.ShapeDtypeStruct((t, h * dv), F32)] * 2,
        grid=(batch, h, ng),
        in_specs=specs(fwd) + specs(bwd),
        out_specs=[out(fwd), out(bwd)],
        scratch_shapes=[pltpu.VMEM((dk, dv), F32)] * 2,
        compiler_params=_cparams(("parallel", "parallel", "arbitrary")),
        name="gla",
    )(pa, kgt, pa, pb, lrt, *gates_f, pa, kgt, pa, pb, lrt, *gates_b)


def _diff_attn_kernel(q_s, k_ref, v_ref, lq1_ref, lk1_ref, lq2_ref, lk2_ref, g_ref, o_ref,
                      kt_s, ve_s, m_s, acc_s, sa_s, sb_s, ma_s, mb_s, aa_s, ab_s, *, seq, tk, lam_init):
    dh = DIFF_DH
    hw = 2 * dh

    @pl.when(pl.program_id(2) == 0)
    def _():
        def body(i, carry):
            rows = pl.ds(pl.multiple_of(i * tk, tk), tk)
            kt_s[:, rows] = k_ref[rows, :].astype(F32).T.astype(BF16)
            ve_s[rows, :] = jnp.concatenate([v_ref[rows, :], jnp.ones((tk, hw), BF16)], axis=1)
            return carry
        lax.fori_loop(0, seq // tk, body, 0)

    m_s[...] = jnp.full_like(m_s, -jnp.inf)
    acc_s[...] = jnp.zeros_like(acc_s)

    def scores(i, buf):
        s_buf, mx_buf, al_buf = buf
        cols = pl.ds(pl.multiple_of(i * tk, tk), tk)
        kt = kt_s[:, cols]
        for p in range(2):
            s = jnp.dot(q_s[:, p * dh:(p + 1) * dh], kt[p * dh:(p + 1) * dh, :], preferred_element_type=F32)
            s_buf[p] = s
            m_old = m_s[p]
            m_new = jnp.maximum(m_old, jnp.max(s, axis=-1, keepdims=True))
            al_buf[p] = jnp.exp2(m_old - m_new)
            mx_buf[p] = m_new
            m_s[p] = m_new

    def softmax_pv(i, buf):
        s_buf, mx_buf, al_buf = buf
        ve = ve_s[pl.ds(pl.multiple_of(i * tk, tk), tk), :]
        for p in range(2):
            pr = jnp.exp2(s_buf[p] - _lane_tile(mx_buf[p], tk // LANES))
            acc_s[p] = (_lane_tile(al_buf[p], 2) * acc_s[p]
                        + jnp.dot(pr.astype(BF16), ve, preferred_element_type=F32))

    n_t = seq // tk
    buf_a, buf_b = (sa_s, ma_s, aa_s), (sb_s, mb_s, ab_s)
    scores(0, buf_a)

    def pair_body(j, carry):
        scores(2 * j + 1, buf_b)
        softmax_pv(2 * j, buf_a)
        scores(2 * j + 2, buf_a)
        softmax_pv(2 * j + 1, buf_b)
        return carry
    lax.fori_loop(0, (n_t - 1) // 2, pair_body, 0)
    if n_t % 2 == 0:
        scores(n_t - 1, buf_b)
        softmax_pv(n_t - 2, buf_a)
        softmax_pv(n_t - 1, buf_b)
    else:
        softmax_pv(n_t - 1, buf_a)

    lam = (jnp.exp(jnp.sum(lq1_ref[...] * lk1_ref[...], axis=-1, keepdims=True))
           - jnp.exp(jnp.sum(lq2_ref[...] * lk2_ref[...], axis=-1, keepdims=True)) + lam_init)
    o = acc_s[0, :, :hw] / acc_s[0, :, hw:] - lam * (acc_s[1, :, :hw] / acc_s[1, :, hw:])
    ms = jnp.mean(o * o, axis=-1, keepdims=True)
    o_ref[...] = (o * lax.rsqrt(ms + EPS) * g_ref[...] * (1.0 - lam_init)).astype(o_ref.dtype)


def _diff_attn(pa, lq1, lk1, lq2, lk2, norm_g, batch, seq, lam_init):
    t = pa.shape[0]
    h, dh = DIFF_HEADS, DIFF_DH
    hw = 2 * dh
    tq = min(1024, seq)
    tk = min(512, seq)
    nq = seq // tq
    q_off = (2 * GLA_HEADS * GLA_DK + GLA_HEADS * GLA_DV) // hw
    k_off = q_off + h
    v_off = k_off + h
    vec = lambda n: pl.BlockSpec((1, n), lambda b, hh, qi: (0, 0))
    return pl.pallas_call(
        functools.partial(_diff_attn_kernel, seq=seq, tk=tk, lam_init=lam_init),
        out_shape=jax.ShapeDtypeStruct((t, h * hw), BF16),
        grid=(batch, h, nq),
        in_specs=[
            pl.BlockSpec((tq, hw), lambda b, hh, qi: (b * nq + qi, q_off + hh)),
            pl.BlockSpec((seq, hw), lambda b, hh, qi: (b, k_off + hh)),
            pl.BlockSpec((seq, hw), lambda b, hh, qi: (b, v_off + hh)),
            vec(dh), vec(dh), vec(dh), vec(dh), vec(hw),
        ],
        out_specs=pl.BlockSpec((tq, hw), lambda b, hh, qi: (b * nq + qi, hh)),
        scratch_shapes=[
            pltpu.VMEM((hw, seq), BF16),
            pltpu.VMEM((seq, 2 * hw), BF16),
            pltpu.VMEM((2, tq, LANES), F32),
            pltpu.VMEM((2, tq, 2 * hw), F32),
            pltpu.VMEM((2, tq, tk), F32),
            pltpu.VMEM((2, tq, tk), F32),
        ] + [pltpu.VMEM((2, tq, LANES), F32)] * 4,
        compiler_params=_cparams(("parallel", "parallel", "arbitrary")),
        name="diff_attn",
    )(pa, pa, pa, lq1.reshape(1, dh), lk1.reshape(1, dh), lq2.reshape(1, dh),
      lk2.reshape(1, dh), norm_g.reshape(1, hw))


def _rope_tables(seq):
    half = ROT_DIM // 2
    pos = jnp.arange(seq, dtype=F32)
    inv_freq = jnp.power(ROPE_THETA, -jnp.arange(0, ROT_DIM, 2, dtype=F32) / ROT_DIM)
    ang = pos[:, None] * inv_freq[None, :]
    cos, sin = jnp.cos(ang), jnp.sin(ang)
    ones = jnp.ones((seq, DIFF_DH - ROT_DIM), F32)
    zeros = jnp.zeros((seq, DIFF_DH - ROT_DIM), F32)
    zh = jnp.zeros((seq, half), F32)
    c = jnp.concatenate([cos, cos, ones], axis=1)
    a = jnp.concatenate([-sin, zh, zeros], axis=1)
    b = jnp.concatenate([zh, sin, zeros], axis=1)
    return tuple(jnp.tile(m, (1, 2)) for m in (c, a, b))


def _merge_kernel(of_ref, ob_ref, pb_rg, od_ref, pb_gg, pb_gd, h_ref, gn_ref, wa_ref, wb_ref, wo_ref,
                  g1_ref, b1_ref, rwh_ref, rwl_ref, rb_ref, h1_ref, route_ref, routet_ref, cnt_ref,
                  *, dn_alpha):
    tm = of_ref.shape[0]
    og = of_ref[...] + ob_ref[...]
    parts = []
    for hh in range(GLA_HEADS):
        xh = og[:, hh * GLA_DV:(hh + 1) * GLA_DV]
        ms = jnp.mean(xh * xh, axis=-1, keepdims=True)
        parts.append(xh * lax.rsqrt(ms + EPS))
    rg = pb_rg[...]
    o_gla = jnp.concatenate(parts, axis=1) * gn_ref[...] * (rg * _sigmoid(rg))
    a = jnp.dot(o_gla.astype(BF16), wa_ref[...], preferred_element_type=F32)
    bb = jnp.dot(od_ref[...], wb_ref[...], preferred_element_type=F32)
    mixed = _sigmoid(pb_gg[...]) * a + _sigmoid(pb_gd[...]) * bb
    mix = jnp.dot(mixed.astype(BF16), wo_ref[...], preferred_element_type=F32)
    h1 = _layer_norm(dn_alpha * h_ref[...] + mix, g1_ref[...], b1_ref[...])
    h1_ref[...] = h1

    hh = h1.astype(BF16)
    hl = (h1 - hh.astype(F32)).astype(BF16)
    logits = (jnp.dot(hh, rwh_ref[...], preferred_element_type=F32)
              + jnp.dot(hh, rwl_ref[...], preferred_element_type=F32)
              + jnp.dot(hl, rwh_ref[...], preferred_element_type=F32)) + rb_ref[...]
    lane = lax.broadcasted_iota(jnp.int32, (tm, LANES), 1).astype(F32)
    work = logits
    top_v, top_i, sels = [], [], []
    for _ in range(TOP_K):
        m = jnp.max(work, axis=-1, keepdims=True)
        idx = jnp.min(jnp.where(work == m, lane, float(LANES)), axis=-1, keepdims=True)
        sel = lane == idx
        top_v.append(m)
        top_i.append(idx)
        sels.append(sel)
        work = jnp.where(sel, -jnp.inf, work)
    exps = [jnp.exp(v - top_v[0]) for v in top_v]
    denom = exps[0]
    for e in exps[1:]:
        denom = denom + e
    onehot = sels[0].astype(F32)
    for s in sels[1:]:
        onehot = onehot + s.astype(F32)
    ri = lax.broadcasted_iota(jnp.int32, (tm, tm), 0)
    ci = lax.broadcasted_iota(jnp.int32, (tm, tm), 1)
    strict = (ci < ri).astype(BF16)
    before = jnp.dot(strict, onehot.astype(BF16), preferred_element_type=F32)
    route = jnp.zeros((tm, LANES), F32)
    for k in range(TOP_K):
        rank = jnp.sum(jnp.where(sels[k], before, 0.0), axis=-1, keepdims=True)
        route = jnp.where(lane == float(ROUTE_E + k), top_i[k], route)
        route = jnp.where(lane == float(ROUTE_G + k), exps[k] / denom, route)
        route = jnp.where(lane == float(ROUTE_R + k), rank, route)
    route_ref[...] = route
    routet_ref[...] = route.T
    cnt_ref[...] = jnp.sum(onehot, axis=0, keepdims=True)


def _merge(o_f, o_b, pb, o_diff, h, gn, wa, wb, wo, g1, b1, rw, rb, dn_alpha):
    t, d = h.shape
    tm = _moe_tile(t)
    row = lambda c: pl.BlockSpec((tm, d), lambda i: (i, c))
    full = lambda shp: pl.BlockSpec(shp, lambda i: (0, 0))
    rwh = rw.astype(BF16)
    rwl = (rw - rwh.astype(F32)).astype(BF16)
    return pl.pallas_call(
        functools.partial(_merge_kernel, dn_alpha=dn_alpha),
        out_shape=[jax.ShapeDtypeStruct((t, d), F32),
                   jax.ShapeDtypeStruct((t, LANES), F32),
                   jax.ShapeDtypeStruct((LANES, t), F32),
                   jax.ShapeDtypeStruct((t // tm, 1, LANES), F32)],
        grid=(t // tm,),
        in_specs=[row(0), row(0), row(0), row(0), row(1), row(2), row(0),
                  full((1, d)), full((d, d)), full((d, d)), full((d, d)),
                  full((1, d)), full((1, d)), full((d, LANES)), full((d, LANES)), full((1, LANES))],
        out_specs=[pl.BlockSpec((tm, d), lambda i: (i, 0)),
                   pl.BlockSpec((tm, LANES), lambda i: (i, 0)),
                   pl.BlockSpec((LANES, tm), lambda i: (0, i)),
                   pl.BlockSpec((None, 1, LANES), lambda i: (i, 0, 0))],
        compiler_params=_cparams(("parallel",)),
        name="merge_ln1_router",
    )(o_f, o_b, pb, o_diff, pb, pb, h, gn, wa, wb, wo, g1, b1, rwh, rwl, rb)


def _moe_tile(t):
    return min(256, t)


def _moe_slots(tm):
    return tm * TOP_K + N_EXPERTS * MOE_SEG


def _piece_loops(j, lo_s, gb_s, np_s, make_copy):
    def per_expert(e, carry):
        idx = j * N_EXPERTS + e
        base, g = lo_s[idx], gb_s[idx]

        def per_piece(p, c):
            off = p * MOE_SEG
            make_copy(pl.multiple_of(base + off, MOE_SEG), pl.multiple_of(g + off, MOE_SEG)).start()
            return c
        lax.fori_loop(0, np_s[idx], per_piece, 0)
        return carry
    lax.fori_loop(0, N_EXPERTS, per_expert, 0)


def _wait_pieces(n, make_copy):
    def body(p, c):
        make_copy(0, 0).wait()
        return c
    lax.fori_loop(0, n, body, 0)


def _dispatch_kernel(lo_s, gb_s, np_s, npt_s, ts_s, tnp_s, h_ref, routet_ref, locol_ref, xb_ref,
                     sorted_s, zero_s, sem, zsem, *, nt):
    tm = h_ref.shape[0]
    n_slots = sorted_s.shape[1]
    j = pl.program_id(0)
    par = j % 2

    def seg_copy(pp):
        def make(local_row, global_row):
            return pltpu.make_async_copy(sorted_s.at[pp, pl.ds(local_row, MOE_SEG)],
                                         xb_ref.at[pl.ds(global_row, MOE_SEG)], sem.at[pp])
        return make

    def zero_copy(_, global_row):
        return pltpu.make_async_copy(zero_s, xb_ref.at[pl.ds(global_row, MOE_SEG)], zsem)

    @pl.when(j == 0)
    def _():
        zero_s[...] = jnp.zeros_like(zero_s)

        def per_region(e, carry):
            def per_piece(p, c):
                zero_copy(0, pl.multiple_of(ts_s[e] + p * MOE_SEG, MOE_SEG)).start()
                return c
            lax.fori_loop(0, tnp_s[e], per_piece, 0)
            return carry
        lax.fori_loop(0, N_EXPERTS + 1, per_region, 0)
        _wait_pieces(tnp_s[N_EXPERTS + 1], zero_copy)

    routet = routet_ref[...]
    sub = lax.broadcasted_iota(jnp.int32, (LANES, tm), 0).astype(F32)
    slot_iota = lax.broadcasted_iota(jnp.int32, (n_slots, tm), 0).astype(F32)
    perm = jnp.zeros((n_slots, tm), F32)
    for k in range(TOP_K):
        e_row = routet[ROUTE_E + k:ROUTE_E + k + 1, :]
        lo_k = jnp.sum(jnp.where(sub == e_row, locol_ref[...], 0.0), axis=0, keepdims=True)
        slot = lo_k + routet[ROUTE_R + k:ROUTE_R + k + 1, :]
        perm = jnp.where(slot_iota == slot, 1.0, perm)
    sorted_s[par] = jnp.dot(perm.astype(BF16), h_ref[...].astype(BF16), preferred_element_type=F32)

    _piece_loops(j, lo_s, gb_s, np_s, seg_copy(par))

    @pl.when(j > 0)
    def _():
        _wait_pieces(npt_s[j - 1], seg_copy(1 - par))

    @pl.when(j == nt - 1)
    def _():
        _wait_pieces(npt_s[j], seg_copy(par))


def _dispatch(tables, h1, routet, lo_col, n_rows):
    t, d = h1.shape
    tm = _moe_tile(t)
    nt = t // tm
    n_slots = _moe_slots(tm)
    return pl.pallas_call(
        functools.partial(_dispatch_kernel, nt=nt),
        out_shape=jax.ShapeDtypeStruct((n_rows, d), F32),
        grid_spec=pltpu.PrefetchScalarGridSpec(
            num_scalar_prefetch=len(tables),
            grid=(nt,),
            in_specs=[
                pl.BlockSpec((tm, d), lambda i, *_: (i, 0)),
                pl.BlockSpec((LANES, tm), lambda i, *_: (0, i)),
                pl.BlockSpec((None, LANES, 1), lambda i, *_: (i, 0, 0)),
            ],
            out_specs=pl.BlockSpec(memory_space=pl.ANY),
            scratch_shapes=[pltpu.VMEM((2, n_slots, d), F32), pltpu.VMEM((MOE_SEG, d), F32),
                            pltpu.SemaphoreType.DMA((2,)), pltpu.SemaphoreType.DMA],
        ),
        compiler_params=_cparams(("arbitrary",)),
        name="moe_dispatch",
    )(*tables, h1, routet, lo_col)


def _experts_kernel(be_ref, nu_ref, x_ref, wg_ref, bg_ref, wu_ref, bu_ref, wd_ref, bd_ref, y_ref,
                    wg_s, wu_s, wd_s):
    i = pl.program_id(0)

    @pl.when(i >= nu_ref[0])
    def _():
        y_ref[...] = jnp.zeros_like(y_ref)

    @pl.when(i < nu_ref[0])
    def _():
        prev = be_ref[jnp.maximum(i - 1, 0)]

        @pl.when((i == 0) | (be_ref[i] != prev))
        def _():
            wg_s[...] = wg_ref[...].astype(BF16)
            wu_s[...] = wu_ref[...].astype(BF16)
            wd_s[...] = wd_ref[...].astype(BF16)

        x = x_ref[...].astype(BF16)
        g = jnp.minimum(jnp.dot(x, wg_s[...], preferred_element_type=F32) + bg_ref[...], SWIGLU_LIMIT)
        u = jnp.clip(jnp.dot(x, wu_s[...], preferred_element_type=F32) + bu_ref[...],
                     -SWIGLU_LIMIT, SWIGLU_LIMIT)
        a = g * _sigmoid(SWIGLU_ALPHA * g) * (u + 1.0)
        y_ref[...] = jnp.dot(a.astype(BF16), wd_s[...], preferred_element_type=F32) + bd_ref[...]


def _experts(blk_e, n_used, xb, wg, bg, wu, bu, wd, bd, tb):
    p, d = xb.shape
    ne, _, dff = wg.shape
    nb = p // tb
    rowblk = lambda i, be, nu: (jnp.maximum(jnp.minimum(i, nu[0] - 1), 0), 0)
    wspec = lambda a, b: pl.BlockSpec((None, a, b), lambda i, be, nu: (be[i], 0, 0))
    return pl.pallas_call(
        _experts_kernel,
        out_shape=jax.ShapeDtypeStruct((p, d), F32),
        grid_spec=pltpu.PrefetchScalarGridSpec(
            num_scalar_prefetch=2,
            grid=(nb,),
            in_specs=[pl.BlockSpec((tb, d), rowblk),
                      wspec(d, dff), wspec(1, dff), wspec(d, dff), wspec(1, dff),
                      wspec(dff, d), wspec(1, d)],
            out_specs=pl.BlockSpec((tb, d), lambda i, be, nu: (i, 0)),
            scratch_shapes=[pltpu.VMEM((d, dff), BF16), pltpu.VMEM((d, dff), BF16),
                            pltpu.VMEM((dff, d), BF16)],
        ),
        compiler_params=_cparams(("arbitrary",)),
        name="moe_experts",
    )(blk_e, n_used, xb, wg, bg.reshape(ne, 1, dff), wu, bu.reshape(ne, 1, dff), wd, bd.reshape(ne, 1, d))


def _combine_kernel(lo_s, gb_s, np_s, npt_s, yb_ref, h_ref, route_ref, lorow_ref, g_ref, b_ref, o_ref,
                    ybuf, sem, *, dn_alpha, nt):
    tm = h_ref.shape[0]
    n_slots = ybuf.shape[1]
    j = pl.program_id(0)
    par = j % 2

    def seg_copy(pp):
        def make(local_row, global_row):
            return pltpu.make_async_copy(yb_ref.at[pl.ds(global_row, MOE_SEG)],
                                         ybuf.at[pp, pl.ds(local_row, MOE_SEG)], sem.at[pp])
        return make

    @pl.when(j == 0)
    def _():
        ybuf[...] = jnp.zeros_like(ybuf)
        _piece_loops(j, lo_s, gb_s, np_s, seg_copy(0))

    @pl.when(j + 1 < nt)
    def _():
        _piece_loops(j + 1, lo_s, gb_s, np_s, seg_copy(1 - par))

    _wait_pieces(npt_s[j], seg_copy(par))

    route = route_ref[...]
    lane = lax.broadcasted_iota(jnp.int32, (tm, LANES), 1).astype(F32)
    slot_iota = lax.broadcasted_iota(jnp.int32, (tm, n_slots), 1).astype(F32)
    wgt = jnp.zeros((tm, n_slots), F32)
    for k in range(TOP_K):
        e_col = route[:, ROUTE_E + k:ROUTE_E + k + 1]
        lo_k = jnp.sum(jnp.where(lane == e_col, lorow_ref[...], 0.0), axis=-1, keepdims=True)
        slot = lo_k + route[:, ROUTE_R + k:ROUTE_R + k + 1]
        wgt = jnp.where(slot_iota == slot, route[:, ROUTE_G + k:ROUTE_G + k + 1], wgt)
    y = ybuf[par]
    wh = wgt.astype(BF16)
    wl = (wgt - wh.astype(F32)).astype(BF16)
    yh = y.astype(BF16)
    yl = (y - yh.astype(F32)).astype(BF16)
    ffn = (jnp.dot(wh, yh, preferred_element_type=F32) + jnp.dot(wh, yl, preferred_element_type=F32)
           + jnp.dot(wl, yh, preferred_element_type=F32))
    o_ref[...] = _layer_norm(dn_alpha * h_ref[...] + ffn, g_ref[...], b_ref[...])


def _combine(tables, yb, h1, route, lo_row, g2, b2, dn_alpha):
    t, d = h1.shape
    tm = _moe_tile(t)
    nt = t // tm
    return pl.pallas_call(
        functools.partial(_combine_kernel, dn_alpha=dn_alpha, nt=nt),
        out_shape=jax.ShapeDtypeStruct((t, d), F32),
        grid_spec=pltpu.PrefetchScalarGridSpec(
            num_scalar_prefetch=len(tables),
            grid=(nt,),
            in_specs=[
                pl.BlockSpec(memory_space=pl.ANY),
                pl.BlockSpec((tm, d), lambda i, *_: (i, 0)),
                pl.BlockSpec((tm, LANES), lambda i, *_: (i, 0)),
                pl.BlockSpec((None, 1, LANES), lambda i, *_: (i, 0, 0)),
                pl.BlockSpec((1, d), lambda i, *_: (0, 0)),
                pl.BlockSpec((1, d), lambda i, *_: (0, 0)),
            ],
            out_specs=pl.BlockSpec((tm, d), lambda i, *_: (i, 0)),
            scratch_shapes=[pltpu.VMEM((2, _moe_slots(tm), d), F32), pltpu.SemaphoreType.DMA((2,))],
        ),
        compiler_params=_cparams(("arbitrary",)),
        name="moe_combine_ln2",
    )(*tables, yb, h1, route, lo_row, g2.reshape(1, d), b2.reshape(1, d))


def _moe_tables(counts, tm, tb, n_rows):
    seg = ((counts + MOE_SEG - 1) // MOE_SEG) * MOE_SEG
    lo = jnp.cumsum(seg, axis=1) - seg
    rows_e = jnp.sum(seg, axis=0)
    region = ((rows_e + tb - 1) // tb) * tb
    pend = jnp.cumsum(region)
    pstart = pend - region
    gbase = pstart[None, :] + jnp.cumsum(seg, axis=0) - seg
    pieces = seg // MOE_SEG
    tail_start = jnp.concatenate([pstart + rows_e, pend[-1:]])
    tail_pieces = jnp.concatenate([(region - rows_e) // MOE_SEG, (n_rows - pend[-1:]) // MOE_SEG])
    tail_pieces = jnp.concatenate([tail_pieces, jnp.sum(tail_pieces, keepdims=True)])
    i32 = lambda a: a.reshape(-1).astype(jnp.int32)
    seg_tables = (i32(lo), i32(gbase), i32(pieces), i32(jnp.sum(pieces, axis=1)))
    n_used = (pend[-1] // tb).astype(jnp.int32)
    blk = jnp.minimum(jnp.arange(n_rows // tb, dtype=jnp.int32), n_used - 1)
    blk_e = jnp.minimum(jnp.sum((pend[None, :] <= (blk * tb)[:, None]).astype(jnp.int32), axis=1),
                        N_EXPERTS - 1)
    lo_f = jnp.pad(lo.astype(F32), ((0, 0), (0, LANES - N_EXPERTS)))
    return seg_tables, (i32(tail_start), i32(tail_pieces)), blk_e, n_used.reshape(1), lo_f


def kernel(x, ln0_g, ln0_b, w_in, gla_wf_fwd, gla_bf_fwd, gla_wf_bwd, gla_bf_bwd, gla_norm_g, diff_lq1, diff_lk1, diff_lq2, diff_lk2, diff_norm_g, w_br_gla, w_br_diff, w_out, ln1_g, ln1_b, router_w, router_b, exp_w_gate, exp_b_gate, exp_w_up, exp_b_up, exp_w_down, exp_b_down, ln2_g, ln2_b):
    batch, seq, d = x.shape
    depth = w_in.shape[0]
    t = batch * seq
    dn_alpha = (2 * depth) ** 0.25
    gk, gv = GLA_HEADS * GLA_DK, GLA_HEADS * GLA_DV
    dq = DIFF_HEADS * 2 * DIFF_DH
    splits = (gk, gk, gv, gv, GLA_RANK, GLA_RANK, dq, dq, dq, d, d)
    cuts = [0]
    for s in splits:
        cuts.append(cuts[-1] + s)
    col = lambda w, i: w[:, cuts[i]:cuts[i + 1]]
    tabs = _rope_tables(seq)
    tm = _moe_tile(t)
    tb = _moe_tile(t)
    n_rows = t * TOP_K + (t // tm) * N_EXPERTS * (MOE_SEG - 1) + N_EXPERTS * (tb - 1)
    n_rows = ((n_rows + tb - 1) // tb) * tb

    assert depth == 1, depth
    cur = x.reshape(t, d)
    g_in, b_in = ln0_g, ln0_b
    for l in range(depth):
        lam_init = 0.8 - 0.6 * math.exp(-0.3 * l)
        w = w_in[l]
        wa = jnp.concatenate([col(w, 0), col(w, 1), col(w, 2), col(w, 6), col(w, 7), col(w, 8)], axis=1).astype(BF16)
        wb = jnp.concatenate([col(w, 3), col(w, 9), col(w, 10), col(w, 4), col(w, 5),
                              jnp.zeros((d, LANES - 2 * GLA_RANK), F32)], axis=1).astype(BF16)
        q_lo = 2 * gk + gv
        pa, pb, h = _ln_proj(cur, g_in, b_in, wa, wb, tabs, seq, 512, 640,
                             (q_lo, q_lo + dq), (q_lo + dq, q_lo + 2 * dq),
                             DIFF_DH ** -0.5 * math.log2(math.e))

        kgt = pa[:, gk:2 * gk].reshape(batch, seq, gk).transpose(0, 2, 1)
        lrt = pb[:, 3 * d:3 * d + 2 * GLA_RANK].reshape(batch, seq, 2 * GLA_RANK).transpose(0, 2, 1)
        zr = jnp.zeros((GLA_RANK, gk), F32)
        pad = jnp.zeros((LANES - 2 * GLA_RANK, gk), F32)
        wf_f = jnp.concatenate([gla_wf_fwd[l], zr, pad], axis=0)
        wf_b = jnp.concatenate([zr, gla_wf_bwd[l], pad], axis=0)
        o_f, o_b = _gla(
            pa, kgt, pb, lrt,
            (wf_f, wf_f[:2 * GLA_RANK].T, gla_bf_fwd[l].reshape(1, gk), gla_bf_fwd[l].reshape(gk, 1)),
            (wf_b, wf_b[:2 * GLA_RANK].T, gla_bf_bwd[l].reshape(1, gk), gla_bf_bwd[l].reshape(gk, 1)),
            batch, seq)

        o_diff = _diff_attn(pa, diff_lq1[l], diff_lk1[l], diff_lq2[l], diff_lk2[l], diff_norm_g[l],
                            batch, seq, lam_init)

        rw = jnp.concatenate([router_w[l], jnp.zeros((d, LANES - N_EXPERTS), F32)], axis=1)
        rb = jnp.concatenate([router_b[l], jnp.full((LANES - N_EXPERTS,), -jnp.inf, F32)]).reshape(1, LANES)
        h1, route, routet, counts = _merge(
            o_f, o_b, pb, o_diff, h, jnp.tile(gla_norm_g[l], GLA_HEADS).reshape(1, gv),
            w_br_gla[l].astype(BF16), w_br_diff[l].astype(BF16), w_out[l].astype(BF16),
            ln1_g[l].reshape(1, d), ln1_b[l].reshape(1, d), rw, rb, dn_alpha)

        seg_tables, tail_tables, blk_e, n_used, lo_f = _moe_tables(
            counts[:, 0, :N_EXPERTS].astype(jnp.int32), tm, tb, n_rows)
        xb = _dispatch(seg_tables + tail_tables, h1, routet, lo_f[:, :, None], n_rows)
        yb = _experts(blk_e, n_used, xb, exp_w_gate[l], exp_b_gate[l], exp_w_up[l], exp_b_up[l],
                      exp_w_down[l], exp_b_down[l], tb)
        cur = _combine(seg_tables, yb, h1, route, lo_f[:, None, :], ln2_g[l], ln2_b[l], dn_alpha)
    return cur.reshape(batch, seq, d)
```

```python
import functools
import math

import jax
import jax.numpy as jnp
from jax import lax
from jax.experimental import pallas as pl
from jax.experimental.pallas import tpu as pltpu

F32 = jnp.float32
BF16 = jnp.bfloat16
HIGHEST = lax.Precision.HIGHEST

GLA_HEADS = 4
GLA_DK = 128
GLA_DV = 256
GLA_RANK = 16
GLA_TAU = 16.0
GLA_CHUNK = 64
DIFF_HEADS = 8
DIFF_DH = 64
ROT_DIM = DIFF_DH // 4
ROPE_THETA = 500000.0
N_EXPERTS = 32
TOP_K = 4
SWIGLU_LIMIT = 7.0
SWIGLU_ALPHA = 1.702
EPS = 1e-5

LANES = 128
SUBLANES = 8
VMEM_LIMIT = 48 * 1024 * 1024

ROUTE_E = 0
ROUTE_G = TOP_K
ROUTE_R = 2 * TOP_K
MOE_SEG = SUBLANES


def _cparams(sem):
    return pltpu.CompilerParams(dimension_semantics=sem, vmem_limit_bytes=VMEM_LIMIT)


def _layer_norm(x, g, b):
    mu = jnp.mean(x, axis=-1, keepdims=True)
    xc = x - mu
    var = jnp.mean(xc * xc, axis=-1, keepdims=True)
    return xc * lax.rsqrt(var + EPS) * g + b


def _sigmoid(x):
    return 1.0 / (1.0 + jnp.exp(-x))


def _log_sigmoid(x):
    return jnp.minimum(x, 0.0) - jnp.log(1.0 + jnp.exp(-jnp.abs(x)))


def _lane_tile(x, n):
    return jnp.concatenate([x] * n, axis=1)


def _split3(x):
    hi = x.astype(BF16)
    r1 = x - hi.astype(F32)
    mid = r1.astype(BF16)
    lo = (r1 - mid.astype(F32)).astype(BF16)
    return hi, mid, lo


def _rope(x, c, a, b):
    return (x * c + pltpu.roll(x, LANES - ROT_DIM // 2, 1) * a + pltpu.roll(x, ROT_DIM // 2, 1) * b)


def _ln_proj_kernel(x_ref, g_ref, b_ref, wa_ref, wb_ref, c_ref, a_ref, r_ref, pa_ref, pb_ref, h_ref, kt_ref,
                    *, tna, tnb, q_cols, k_cols, q_scale, t_cols):
    h = _layer_norm(x_ref[...], g_ref[...], b_ref[...])
    h_ref[...] = h
    h16 = h.astype(BF16)
    for j in range(wa_ref.shape[1] // tna):
        res = jnp.dot(h16, wa_ref[:, j * tna:(j + 1) * tna], preferred_element_type=F32)
        for c in range(0, tna, LANES):
            col = j * tna + c
            blk = res[:, c:c + LANES]
            if q_cols[0] <= col < q_cols[1]:
                blk = _rope(blk, c_ref[...], a_ref[...], r_ref[...]) * q_scale
            elif k_cols[0] <= col < k_cols[1]:
                blk = _rope(blk, c_ref[...], a_ref[...], r_ref[...])
            pa_ref[:, col:col + LANES] = blk.astype(pa_ref.dtype)
            if t_cols[0] <= col < t_cols[1]:
                kt_ref[col - t_cols[0]:col - t_cols[0] + LANES, :] = blk.T.astype(kt_ref.dtype)
    for j in range(wb_ref.shape[1] // tnb):
        cols = slice(j * tnb, (j + 1) * tnb)
        pb_ref[:, cols] = jnp.dot(h16, wb_ref[:, cols], preferred_element_type=F32)


def _ln_proj(x, g, b, wa16, wb16, tabs, seq, tna, tnb, q_cols, k_cols, q_scale, t_cols):
    t, d = x.shape
    na, nb = wa16.shape[1], wb16.shape[1]
    tm = min(256, t)
    nrow = seq // tm
    tw = t_cols[1] - t_cols[0]
    row = lambda n: pl.BlockSpec((tm, n), lambda i: (i, 0))
    full = lambda r, n: pl.BlockSpec((r, n), lambda i: (0, 0))
    tab = pl.BlockSpec((tm, LANES), lambda i: (i % nrow, 0))
    return pl.pallas_call(
        functools.partial(_ln_proj_kernel, tna=tna, tnb=tnb, q_cols=q_cols, k_cols=k_cols, q_scale=q_scale,
                          t_cols=t_cols),
        out_shape=[jax.ShapeDtypeStruct((t, na), BF16), jax.ShapeDtypeStruct((t, nb), F32),
                   jax.ShapeDtypeStruct((t, d), F32), jax.ShapeDtypeStruct((t // seq, tw, seq), BF16)],
        grid=(t // tm,),
        in_specs=[row(d), full(1, d), full(1, d), full(d, na), full(d, nb), tab, tab, tab],
        out_specs=[row(na), row(nb), row(d), pl.BlockSpec((None, tw, tm), lambda i: (i // nrow, 0, i % nrow))],
        compiler_params=_cparams(("parallel",)),
        name="ln_proj",
    )(x, g.reshape(1, d), b.reshape(1, d), wa16, wb16, *tabs)


def _gla_kernel(*refs, n_chunks, batch):
    ins, (of_ref, ob_ref, sf_ref, sb_ref) = refs[:-4], refs[-4:]
    n_in = len(ins) // 2

    @pl.when(pl.program_id(1) == 0)
    def _():
        sf_ref[...] = jnp.zeros_like(sf_ref)
        sb_ref[...] = jnp.zeros_like(sb_ref)

    chains = []
    for bb in range(batch):
        chains.append(_gla_chain(*ins[:n_in], of_ref, sf_ref, bb=bb, rev=False, n_chunks=n_chunks))
        chains.append(_gla_chain(*ins[n_in:], ob_ref, sb_ref, bb=bb, rev=True, n_chunks=n_chunks))
    for _ in zip(*chains):
        pass


def _gla_chain(q_ref, kt_ref, v_ref, lr_ref, lrt_ref, wf_ref, wft_ref, bf_ref, bft_ref,
               o_ref, s_ref, *, bb, rev, n_chunks):
    c_len = GLA_CHUNK

    z = jnp.dot(lr_ref[bb].astype(BF16), wf_ref[...].astype(BF16), preferred_element_type=F32) + bf_ref[...]
    logf = _log_sigmoid(z) * (1.0 / GLA_TAU)
    zt = jnp.dot(wft_ref[...].astype(BF16), lrt_ref[bb].astype(BF16), preferred_element_type=F32) + bft_ref[...]
    logft = _log_sigmoid(zt) * (1.0 / GLA_TAU)
    yield

    r = n_chunks * c_len
    shift = c_len.bit_length() - 1
    ri = lax.broadcasted_iota(jnp.int32, (r, r), 0)
    ci = lax.broadcasted_iota(jnp.int32, (r, r), 1)
    same = jnp.right_shift(ri, shift) == jnp.right_shift(ci, shift)
    lower = same & (ri >= ci)
    upper = same & (ri <= ci)
    keep = upper if rev else lower
    tri = keep.astype(BF16)
    tri_t = (lower if rev else upper).astype(BF16)
    blk = same.astype(BF16)
    q_scale = GLA_DK ** -0.5

    b = sum(jnp.dot(tri, part, preferred_element_type=F32) for part in _split3(logf))
    bts = sum(jnp.dot(part, jnp.concatenate([tri_t, blk], axis=1), preferred_element_type=F32)
              for part in _split3(logft))
    bt, bt_tot = bts[:, :r], bts[:, r:]
    yield
    q_in = (q_ref[bb].astype(F32) * q_scale * jnp.exp(b)).astype(BF16)
    kt = kt_ref[bb].astype(F32)
    k_in_t = (kt * jnp.exp(-bt)).astype(BF16)
    k_st_t = (kt * jnp.exp(bt_tot - bt)).astype(BF16)
    decay = jnp.exp(bt_tot)
    v = v_ref[bb]
    yield
    att = jnp.dot(q_in, k_in_t, preferred_element_type=F32)
    att = jnp.where(keep, att, 0.0).astype(BF16)
    yield
    o_intra = jnp.dot(att, v, preferred_element_type=F32)
    yield

    order = range(n_chunks - 1, -1, -1) if rev else range(n_chunks)
    for c in order:
        rows = slice(c * c_len, (c + 1) * c_len)
        state = s_ref[bb]
        o_ref[bb, rows, :] = o_intra[rows, :] + jnp.dot(q_in[rows, :], state.astype(BF16),
                                                         preferred_element_type=F32)
        s_ref[bb] = (decay[:, c * c_len:c * c_len + 1] * state
                     + jnp.dot(k_st_t[:, rows], v[rows, :], preferred_element_type=F32))
        yield


def _gla(pa, kgt, pb, lrt, gates_f, gates_b, batch, seq):
    t = pa.shape[0]
    n_chunks = 4 if seq % (4 * GLA_CHUNK) == 0 else 1
    r = n_chunks * GLA_CHUNK
    ng = seq // r
    dk, dv, h = GLA_DK, GLA_DV, GLA_HEADS
    v_off = (2 * h * dk) // dv
    lr_blk = (3 * h * dv) // LANES

    def specs(gi):
        return [
            pl.BlockSpec((batch, r, dk), lambda hh, g: (0, gi(g), hh)),
            pl.BlockSpec((batch, dk, r), lambda hh, g: (0, hh, gi(g))),
            pl.BlockSpec((batch, r, dv), lambda hh, g: (0, gi(g), v_off + hh)),
            pl.BlockSpec((batch, r, LANES), lambda hh, g: (0, gi(g), lr_blk)),
            pl.BlockSpec((batch, 2 * GLA_RANK, r), lambda hh, g: (0, 0, gi(g))),
            pl.BlockSpec((LANES, dk), lambda hh, g: (0, hh)),
            pl.BlockSpec((dk, 2 * GLA_RANK), lambda hh, g: (hh, 0)),
            pl.BlockSpec((1, dk), lambda hh, g: (0, hh)),
            pl.BlockSpec((dk, 1), lambda hh, g: (hh, 0)),
        ]

    fwd = lambda g: g
    bwd = lambda g: ng - 1 - g
    out = lambda gi: pl.BlockSpec((batch, r, dv), lambda hh, g: (0, gi(g), hh))
    pa3 = pa.reshape(batch, seq, pa.shape[1])
    pb3 = pb.reshape(batch, seq, pb.shape[1])
    o_f, o_b = pl.pallas_call(
        functools.partial(_gla_kernel, n_chunks=n_chunks, batch=batch),
        out_shape=[jax.ShapeDtypeStruct((batch, seq, h * dv), F32)] * 2,
        grid=(h, ng),
        in_specs=specs(fwd) + specs(bwd),
        out_specs=[out(fwd), out(bwd)],
        scratch_shapes=[pltpu.VMEM((batch, dk, dv), F32)] * 2,
        compiler_params=_cparams(("parallel", "arbitrary")),
        name="gla",
    )(pa3, kgt, pa3, pb3, lrt, *gates_f, pa3, kgt, pa3, pb3, lrt, *gates_b)
    return o_f.reshape(t, h * dv), o_b.reshape(t, h * dv)


def _diff_attn_kernel(q_s, k_ref, v_ref, lq1_ref, lk1_ref, lq2_ref, lk2_ref, g_ref, o_ref,
                      kt_s, ve_s, m_s, acc_s, sa_s, sb_s, ma_s, mb_s, aa_s, ab_s, *, seq, tk, lam_init):
    dh = DIFF_DH
    hw = 2 * dh

    @pl.when(pl.program_id(2) == 0)
    def _():
        def body(i, carry):
            rows = pl.ds(pl.multiple_of(i * tk, tk), tk)
            kt_s[:, rows] = k_ref[rows, :].astype(F32).T.astype(BF16)
            ve_s[rows, :] = jnp.concatenate([v_ref[rows, :], jnp.ones((tk, hw), BF16)], axis=1)
            return carry
        lax.fori_loop(0, seq // tk, body, 0)

    m_s[...] = jnp.full_like(m_s, -jnp.inf)
    acc_s[...] = jnp.zeros_like(acc_s)

    def scores(i, buf):
        s_buf, mx_buf, al_buf = buf
        cols = pl.ds(pl.multiple_of(i * tk, tk), tk)
        kt = kt_s[:, cols]
        for p in range(2):
            s = jnp.dot(q_s[:, p * dh:(p + 1) * dh], kt[p * dh:(p + 1) * dh, :], preferred_element_type=F32)
            s_buf[p] = s
            m_old = m_s[p]
            m_new = jnp.maximum(m_old, jnp.max(s, axis=-1, keepdims=True))
            al_buf[p] = jnp.exp2(m_old - m_new)
            mx_buf[p] = m_new
            m_s[p] = m_new

    def softmax_pv(i, buf):
        s_buf, mx_buf, al_buf = buf
        ve = ve_s[pl.ds(pl.multiple_of(i * tk, tk), tk), :]
        for p in range(2):
            pr = jnp.exp2(s_buf[p] - _lane_tile(mx_buf[p], tk // LANES))
            acc_s[p] = (_lane_tile(al_buf[p], 2) * acc_s[p]
                        + jnp.dot(pr.astype(BF16), ve, preferred_element_type=F32))

    n_t = seq // tk
    buf_a, buf_b = (sa_s, ma_s, aa_s), (sb_s, mb_s, ab_s)
    scores(0, buf_a)

    def pair_body(j, carry):
        scores(2 * j + 1, buf_b)
        softmax_pv(2 * j, buf_a)
        scores(2 * j + 2, buf_a)
        softmax_pv(2 * j + 1, buf_b)
        return carry
    lax.fori_loop(0, (n_t - 1) // 2, pair_body, 0)
    if n_t % 2 == 0:
        scores(n_t - 1, buf_b)
        softmax_pv(n_t - 2, buf_a)
        softmax_pv(n_t - 1, buf_b)
    else:
        softmax_pv(n_t - 1, buf_a)

    lam = (jnp.exp(jnp.sum(lq1_ref[...] * lk1_ref[...], axis=-1, keepdims=True))
           - jnp.exp(jnp.sum(lq2_ref[...] * lk2_ref[...], axis=-1, keepdims=True)) + lam_init)
    o = acc_s[0, :, :hw] / acc_s[0, :, hw:] - lam * (acc_s[1, :, :hw] / acc_s[1, :, hw:])
    ms = jnp.mean(o * o, axis=-1, keepdims=True)
    o_ref[...] = (o * lax.rsqrt(ms + EPS) * g_ref[...] * (1.0 - lam_init)).astype(o_ref.dtype)


def _diff_attn(pa, lq1, lk1, lq2, lk2, norm_g, batch, seq, lam_init):
    t = pa.shape[0]
    h, dh = DIFF_HEADS, DIFF_DH
    hw = 2 * dh
    tq = min(1024, seq)
    tk = min(512, seq)
    nq = seq // tq
    q_off = (2 * GLA_HEADS * GLA_DK + GLA_HEADS * GLA_DV) // hw
    k_off = q_off + h
    v_off = k_off + h
    vec = lambda n: pl.BlockSpec((1, n), lambda b, hh, qi: (0, 0))
    return pl.pallas_call(
        functools.partial(_diff_attn_kernel, seq=seq, tk=tk, lam_init=lam_init),
        out_shape=jax.ShapeDtypeStruct((t, h * hw), BF16),
        grid=(batch, h, nq),
        in_specs=[
            pl.BlockSpec((tq, hw), lambda b, hh, qi: (b * nq + qi, q_off + hh)),
            pl.BlockSpec((seq, hw), lambda b, hh, qi: (b, k_off + hh)),
            pl.BlockSpec((seq, hw), lambda b, hh, qi: (b, v_off + hh)),
            vec(dh), vec(dh), vec(dh), vec(dh), vec(hw),
        ],
        out_specs=pl.BlockSpec((tq, hw), lambda b, hh, qi: (b * nq + qi, hh)),
        scratch_shapes=[
            pltpu.VMEM((hw, seq), BF16),
            pltpu.VMEM((seq, 2 * hw), BF16),
            pltpu.VMEM((2, tq, LANES), F32),
            pltpu.VMEM((2, tq, 2 * hw), F32),
            pltpu.VMEM((2, tq, tk), F32),
            pltpu.VMEM((2, tq, tk), F32),
        ] + [pltpu.VMEM((2, tq, LANES), F32)] * 4,
        compiler_params=_cparams(("parallel", "parallel", "arbitrary")),
        name="diff_attn",
    )(pa, pa, pa, lq1.reshape(1, dh), lk1.reshape(1, dh), lq2.reshape(1, dh),
      lk2.reshape(1, dh), norm_g.reshape(1, hw))


def _rope_tables(seq):
    half = ROT_DIM // 2
    pos = jnp.arange(seq, dtype=F32)
    inv_freq = jnp.power(ROPE_THETA, -jnp.arange(0, ROT_DIM, 2, dtype=F32) / ROT_DIM)
    ang = pos[:, None] * inv_freq[None, :]
    cos, sin = jnp.cos(ang), jnp.sin(ang)
    ones = jnp.ones((seq, DIFF_DH - ROT_DIM), F32)
    zeros = jnp.zeros((seq, DIFF_DH - ROT_DIM), F32)
    zh = jnp.zeros((seq, half), F32)
    c = jnp.concatenate([cos, cos, ones], axis=1)
    a = jnp.concatenate([-sin, zh, zeros], axis=1)
    b = jnp.concatenate([zh, sin, zeros], axis=1)
    return tuple(jnp.tile(m, (1, 2)) for m in (c, a, b))


def _merge_kernel(of_ref, ob_ref, pb_rg, od_ref, pb_gg, pb_gd, h_ref, gn_ref, wa_ref, wb_ref, wo_ref,
                  g1_ref, b1_ref, rwh_ref, rwl_ref, rb_ref, h1_ref, route_ref, routet_ref, cnt_ref,
                  *, dn_alpha):
    tm = of_ref.shape[0]
    og = of_ref[...] + ob_ref[...]
    parts = []
    for hh in range(GLA_HEADS):
        xh = og[:, hh * GLA_DV:(hh + 1) * GLA_DV]
        ms = jnp.mean(xh * xh, axis=-1, keepdims=True)
        parts.append(xh * lax.rsqrt(ms + EPS))
    rg = pb_rg[...]
    o_gla = jnp.concatenate(parts, axis=1) * gn_ref[...] * (rg * _sigmoid(rg))
    a = jnp.dot(o_gla.astype(BF16), wa_ref[...], preferred_element_type=F32)
    bb = jnp.dot(od_ref[...], wb_ref[...], preferred_element_type=F32)
    mixed = _sigmoid(pb_gg[...]) * a + _sigmoid(pb_gd[...]) * bb
    mix = jnp.dot(mixed.astype(BF16), wo_ref[...], preferred_element_type=F32)
    h1 = _layer_norm(dn_alpha * h_ref[...] + mix, g1_ref[...], b1_ref[...])
    h1_ref[...] = h1

    hh = h1.astype(BF16)
    hl = (h1 - hh.astype(F32)).astype(BF16)
    logits = (jnp.dot(hh, rwh_ref[...], preferred_element_type=F32)
              + jnp.dot(hh, rwl_ref[...], preferred_element_type=F32)
              + jnp.dot(hl, rwh_ref[...], preferred_element_type=F32)) + rb_ref[...]
    lane = lax.broadcasted_iota(jnp.int32, (tm, LANES), 1).astype(F32)
    work = logits
    top_v, top_i, sels = [], [], []
    for _ in range(TOP_K):
        m = jnp.max(work, axis=-1, keepdims=True)
        idx = jnp.min(jnp.where(work == m, lane, float(LANES)), axis=-1, keepdims=True)
        sel = lane == idx
        top_v.append(m)
        top_i.append(idx)
        sels.append(sel)
        work = jnp.where(sel, -jnp.inf, work)
    exps = [jnp.exp(v - top_v[0]) for v in top_v]
    denom = exps[0]
    for e in exps[1:]:
        denom = denom + e
    onehot = sels[0].astype(F32)
    for s in sels[1:]:
        onehot = onehot + s.astype(F32)
    ri = lax.broadcasted_iota(jnp.int32, (tm, tm), 0)
    ci = lax.broadcasted_iota(jnp.int32, (tm, tm), 1)
    strict = (ci < ri).astype(BF16)
    before = jnp.dot(strict, onehot.astype(BF16), preferred_element_type=F32)
    route = jnp.zeros((tm, LANES), F32)
    for k in range(TOP_K):
        rank = jnp.sum(jnp.where(sels[k], before, 0.0), axis=-1, keepdims=True)
        route = jnp.where(lane == float(ROUTE_E + k), top_i[k], route)
        route = jnp.where(lane == float(ROUTE_G + k), exps[k] / denom, route)
        route = jnp.where(lane == float(ROUTE_R + k), rank, route)
    route_ref[...] = route
    routet_ref[...] = route.T
    cnt_ref[...] = jnp.sum(onehot, axis=0, keepdims=True)


def _merge(o_f, o_b, pb, o_diff, h, gn, wa, wb, wo, g1, b1, rw, rb, dn_alpha):
    t, d = h.shape
    tm = _moe_tile(t)
    row = lambda c: pl.BlockSpec((tm, d), lambda i: (i, c))
    full = lambda shp: pl.BlockSpec(shp, lambda i: (0, 0))
    rwh = rw.astype(BF16)
    rwl = (rw - rwh.astype(F32)).astype(BF16)
    return pl.pallas_call(
        functools.partial(_merge_kernel, dn_alpha=dn_alpha),
        out_shape=[jax.ShapeDtypeStruct((t, d), F32),
                   jax.ShapeDtypeStruct((t, LANES), F32),
                   jax.ShapeDtypeStruct((LANES, t), F32),
                   jax.ShapeDtypeStruct((t // tm, 1, LANES), F32)],
        grid=(t // tm,),
        in_specs=[row(0), row(0), row(0), row(0), row(1), row(2), row(0),
                  full((1, d)), full((d, d)), full((d, d)), full((d, d)),
                  full((1, d)), full((1, d)), full((d, LANES)), full((d, LANES)), full((1, LANES))],
        out_specs=[pl.BlockSpec((tm, d), lambda i: (i, 0)),
                   pl.BlockSpec((tm, LANES), lambda i: (i, 0)),
                   pl.BlockSpec((LANES, tm), lambda i: (0, i)),
                   pl.BlockSpec((None, 1, LANES), lambda i: (i, 0, 0))],
        compiler_params=_cparams(("parallel",)),
        name="merge_ln1_router",
    )(o_f, o_b, pb, o_diff, pb, pb, h, gn, wa, wb, wo, g1, b1, rwh, rwl, rb)


def _moe_tile(t):
    return min(256, t)


def _moe_slots(tm):
    return tm * TOP_K + N_EXPERTS * MOE_SEG


def _piece_loops(j, lo_s, gb_s, np_s, make_copy):
    def per_expert(e, carry):
        idx = j * N_EXPERTS + e
        base, g = lo_s[idx], gb_s[idx]

        def per_piece(p, c):
            off = p * MOE_SEG
            make_copy(pl.multiple_of(base + off, MOE_SEG), pl.multiple_of(g + off, MOE_SEG)).start()
            return c
        lax.fori_loop(0, np_s[idx], per_piece, 0)
        return carry
    lax.fori_loop(0, N_EXPERTS, per_expert, 0)


def _wait_pieces(n, make_copy):
    def body(p, c):
        make_copy(0, 0).wait()
        return c
    lax.fori_loop(0, n, body, 0)


def _dispatch_kernel(lo_s, gb_s, np_s, npt_s, ts_s, tnp_s, h_ref, routet_ref, locol_ref, xb_ref,
                     sorted_s, zero_s, sem, zsem, *, nt):
    tm = h_ref.shape[0]
    n_slots = sorted_s.shape[1]
    j = pl.program_id(0)
    par = j % 2

    def seg_copy(pp):
        def make(local_row, global_row):
            return pltpu.make_async_copy(sorted_s.at[pp, pl.ds(local_row, MOE_SEG)],
                                         xb_ref.at[pl.ds(global_row, MOE_SEG)], sem.at[pp])
        return make

    def zero_copy(_, global_row):
        return pltpu.make_async_copy(zero_s, xb_ref.at[pl.ds(global_row, MOE_SEG)], zsem)

    @pl.when(j == 0)
    def _():
        zero_s[...] = jnp.zeros_like(zero_s)

        def per_region(e, carry):
            def per_piece(p, c):
                zero_copy(0, pl.multiple_of(ts_s[e] + p * MOE_SEG, MOE_SEG)).start()
                return c
            lax.fori_loop(0, tnp_s[e], per_piece, 0)
            return carry
        lax.fori_loop(0, N_EXPERTS + 1, per_region, 0)
        _wait_pieces(tnp_s[N_EXPERTS + 1], zero_copy)

    routet = routet_ref[...]
    sub = lax.broadcasted_iota(jnp.int32, (LANES, tm), 0).astype(F32)
    slot_iota = lax.broadcasted_iota(jnp.int32, (n_slots, tm), 0).astype(F32)
    perm = jnp.zeros((n_slots, tm), F32)
    for k in range(TOP_K):
        e_row = routet[ROUTE_E + k:ROUTE_E + k + 1, :]
        lo_k = jnp.sum(jnp.where(sub == e_row, locol_ref[...], 0.0), axis=0, keepdims=True)
        slot = lo_k + routet[ROUTE_R + k:ROUTE_R + k + 1, :]
        perm = jnp.where(slot_iota == slot, 1.0, perm)
    sorted_s[par] = jnp.dot(perm.astype(BF16), h_ref[...].astype(BF16), preferred_element_type=F32)

    _piece_loops(j, lo_s, gb_s, np_s, seg_copy(par))

    @pl.when(j > 0)
    def _():
        _wait_pieces(npt_s[j - 1], seg_copy(1 - par))

    @pl.when(j == nt - 1)
    def _():
        _wait_pieces(npt_s[j], seg_copy(par))


def _dispatch(tables, h1, routet, lo_col, n_rows):
    t, d = h1.shape
    tm = _moe_tile(t)
    nt = t // tm
    n_slots = _moe_slots(tm)
    return pl.pallas_call(
        functools.partial(_dispatch_kernel, nt=nt),
        out_shape=jax.ShapeDtypeStruct((n_rows, d), F32),
        grid_spec=pltpu.PrefetchScalarGridSpec(
            num_scalar_prefetch=len(tables),
            grid=(nt,),
            in_specs=[
                pl.BlockSpec((tm, d), lambda i, *_: (i, 0)),
                pl.BlockSpec((LANES, tm), lambda i, *_: (0, i)),
                pl.BlockSpec((None, LANES, 1), lambda i, *_: (i, 0, 0)),
            ],
            out_specs=pl.BlockSpec(memory_space=pl.ANY),
            scratch_shapes=[pltpu.VMEM((2, n_slots, d), F32), pltpu.VMEM((MOE_SEG, d), F32),
                            pltpu.SemaphoreType.DMA((2,)), pltpu.SemaphoreType.DMA],
        ),
        compiler_params=_cparams(("arbitrary",)),
        name="moe_dispatch",
    )(*tables, h1, routet, lo_col)


def _experts_kernel(be_ref, nu_ref, x_ref, wg_ref, bg_ref, wu_ref, bu_ref, wd_ref, bd_ref, y_ref,
                    wg_s, wu_s, wd_s):
    i = pl.program_id(0)

    @pl.when(i >= nu_ref[0])
    def _():
        y_ref[...] = jnp.zeros_like(y_ref)

    @pl.when(i < nu_ref[0])
    def _():
        prev = be_ref[jnp.maximum(i - 1, 0)]

        @pl.when((i == 0) | (be_ref[i] != prev))
        def _():
            wg_s[...] = wg_ref[...].astype(BF16)
            wu_s[...] = wu_ref[...].astype(BF16)
            wd_s[...] = wd_ref[...].astype(BF16)

        x = x_ref[...].astype(BF16)
        g = jnp.minimum(jnp.dot(x, wg_s[...], preferred_element_type=F32) + bg_ref[...], SWIGLU_LIMIT)
        u = jnp.clip(jnp.dot(x, wu_s[...], preferred_element_type=F32) + bu_ref[...],
                     -SWIGLU_LIMIT, SWIGLU_LIMIT)
        a = g * _sigmoid(SWIGLU_ALPHA * g) * (u + 1.0)
        y_ref[...] = jnp.dot(a.astype(BF16), wd_s[...], preferred_element_type=F32) + bd_ref[...]


def _experts(blk_e, n_used, xb, wg, bg, wu, bu, wd, bd, tb):
    p, d = xb.shape
    ne, _, dff = wg.shape
    nb = p // tb
    rowblk = lambda i, be, nu: (jnp.maximum(jnp.minimum(i, nu[0] - 1), 0), 0)
    wspec = lambda a, b: pl.BlockSpec((None, a, b), lambda i, be, nu: (be[i], 0, 0))
    return pl.pallas_call(
        _experts_kernel,
        out_shape=jax.ShapeDtypeStruct((p, d), F32),
        grid_spec=pltpu.PrefetchScalarGridSpec(
            num_scalar_prefetch=2,
            grid=(nb,),
            in_specs=[pl.BlockSpec((tb, d), rowblk),
                      wspec(d, dff), wspec(1, dff), wspec(d, dff), wspec(1, dff),
                      wspec(dff, d), wspec(1, d)],
            out_specs=pl.BlockSpec((tb, d), lambda i, be, nu: (i, 0)),
            scratch_shapes=[pltpu.VMEM((d, dff), BF16), pltpu.VMEM((d, dff), BF16),
                            pltpu.VMEM((dff, d), BF16)],
        ),
        compiler_params=_cparams(("arbitrary",)),
        name="moe_experts",
    )(blk_e, n_used, xb, wg, bg.reshape(ne, 1, dff), wu, bu.reshape(ne, 1, dff), wd, bd.reshape(ne, 1, d))


def _combine_kernel(lo_s, gb_s, np_s, npt_s, yb_ref, h_ref, route_ref, lorow_ref, g_ref, b_ref, o_ref,
                    ybuf, sem, *, dn_alpha, nt):
    tm = h_ref.shape[0]
    n_slots = ybuf.shape[1]
    j = pl.program_id(0)
    par = j % 2

    def seg_copy(pp):
        def make(local_row, global_row):
            return pltpu.make_async_copy(yb_ref.at[pl.ds(global_row, MOE_SEG)],
                                         ybuf.at[pp, pl.ds(local_row, MOE_SEG)], sem.at[pp])
        return make

    @pl.when(j == 0)
    def _():
        ybuf[...] = jnp.zeros_like(ybuf)
        _piece_loops(j, lo_s, gb_s, np_s, seg_copy(0))

    @pl.when(j + 1 < nt)
    def _():
        _piece_loops(j + 1, lo_s, gb_s, np_s, seg_copy(1 - par))

    _wait_pieces(npt_s[j], seg_copy(par))

    route = route_ref[...]
    lane = lax.broadcasted_iota(jnp.int32, (tm, LANES), 1).astype(F32)
    slot_iota = lax.broadcasted_iota(jnp.int32, (tm, n_slots), 1).astype(F32)
    wgt = jnp.zeros((tm, n_slots), F32)
    for k in range(TOP_K):
        e_col = route[:, ROUTE_E + k:ROUTE_E + k + 1]
        lo_k = jnp.sum(jnp.where(lane == e_col, lorow_ref[...], 0.0), axis=-1, keepdims=True)
        slot = lo_k + route[:, ROUTE_R + k:ROUTE_R + k + 1]
        wgt = jnp.where(slot_iota == slot, route[:, ROUTE_G + k:ROUTE_G + k + 1], wgt)
    y = ybuf[par]
    wh = wgt.astype(BF16)
    wl = (wgt - wh.astype(F32)).astype(BF16)
    yh = y.astype(BF16)
    yl = (y - yh.astype(F32)).astype(BF16)
    ffn = (jnp.dot(wh, yh, preferred_element_type=F32) + jnp.dot(wh, yl, preferred_element_type=F32)
           + jnp.dot(wl, yh, preferred_element_type=F32))
    o_ref[...] = _layer_norm(dn_alpha * h_ref[...] + ffn, g_ref[...], b_ref[...])


def _combine(tables, yb, h1, route, lo_row, g2, b2, dn_alpha):
    t, d = h1.shape
    tm = _moe_tile(t)
    nt = t // tm
    return pl.pallas_call(
        functools.partial(_combine_kernel, dn_alpha=dn_alpha, nt=nt),
        out_shape=jax.ShapeDtypeStruct((t, d), F32),
        grid_spec=pltpu.PrefetchScalarGridSpec(
            num_scalar_prefetch=len(tables),
            grid=(nt,),
            in_specs=[
                pl.BlockSpec(memory_space=pl.ANY),
                pl.BlockSpec((tm, d), lambda i, *_: (i, 0)),
                pl.BlockSpec((tm, LANES), lambda i, *_: (i, 0)),
                pl.BlockSpec((None, 1, LANES), lambda i, *_: (i, 0, 0)),
                pl.BlockSpec((1, d), lambda i, *_: (0, 0)),
                pl.BlockSpec((1, d), lambda i, *_: (0, 0)),
            ],
            out_specs=pl.BlockSpec((tm, d), lambda i, *_: (i, 0)),
            scratch_shapes=[pltpu.VMEM((2, _moe_slots(tm), d), F32), pltpu.SemaphoreType.DMA((2,))],
        ),
        compiler_params=_cparams(("arbitrary",)),
        name="moe_combine_ln2",
    )(*tables, yb, h1, route, lo_row, g2.reshape(1, d), b2.reshape(1, d))


def _moe_tables(counts, tm, tb, n_rows):
    seg = ((counts + MOE_SEG - 1) // MOE_SEG) * MOE_SEG
    lo = jnp.cumsum(seg, axis=1) - seg
    rows_e = jnp.sum(seg, axis=0)
    region = ((rows_e + tb - 1) // tb) * tb
    pend = jnp.cumsum(region)
    pstart = pend - region
    gbase = pstart[None, :] + jnp.cumsum(seg, axis=0) - seg
    pieces = seg // MOE_SEG
    tail_start = jnp.concatenate([pstart + rows_e, pend[-1:]])
    tail_pieces = jnp.concatenate([(region - rows_e) // MOE_SEG, (n_rows - pend[-1:]) // MOE_SEG])
    tail_pieces = jnp.concatenate([tail_pieces, jnp.sum(tail_pieces, keepdims=True)])
    i32 = lambda a: a.reshape(-1).astype(jnp.int32)
    seg_tables = (i32(lo), i32(gbase), i32(pieces), i32(jnp.sum(pieces, axis=1)))
    n_used = (pend[-1] // tb).astype(jnp.int32)
    blk = jnp.minimum(jnp.arange(n_rows // tb, dtype=jnp.int32), n_used - 1)
    blk_e = jnp.minimum(jnp.sum((pend[None, :] <= (blk * tb)[:, None]).astype(jnp.int32), axis=1),
                        N_EXPERTS - 1)
    lo_f = jnp.pad(lo.astype(F32), ((0, 0), (0, LANES - N_EXPERTS)))
    return seg_tables, (i32(tail_start), i32(tail_pieces)), blk_e, n_used.reshape(1), lo_f


def kernel(x, ln0_g, ln0_b, w_in, gla_wf_fwd, gla_bf_fwd, gla_wf_bwd, gla_bf_bwd, gla_norm_g, diff_lq1, diff_lk1, diff_lq2, diff_lk2, diff_norm_g, w_br_gla, w_br_diff, w_out, ln1_g, ln1_b, router_w, router_b, exp_w_gate, exp_b_gate, exp_w_up, exp_b_up, exp_w_down, exp_b_down, ln2_g, ln2_b):
    batch, seq, d = x.shape
    depth = w_in.shape[0]
    t = batch * seq
    dn_alpha = (2 * depth) ** 0.25
    gk, gv = GLA_HEADS * GLA_DK, GLA_HEADS * GLA_DV
    dq = DIFF_HEADS * 2 * DIFF_DH
    splits = (gk, gk, gv, gv, GLA_RANK, GLA_RANK, dq, dq, dq, d, d)
    cuts = [0]
    for s in splits:
        cuts.append(cuts[-1] + s)
    col = lambda w, i: w[:, cuts[i]:cuts[i + 1]]
    tabs = _rope_tables(seq)
    tm = _moe_tile(t)
    tb = _moe_tile(t)
    n_rows = t * TOP_K + (t // tm) * N_EXPERTS * (MOE_SEG - 1) + N_EXPERTS * (tb - 1)
    n_rows = ((n_rows + tb - 1) // tb) * tb

    assert depth == 1, depth
    cur = x.reshape(t, d)
    g_in, b_in = ln0_g, ln0_b
    for l in range(depth):
        lam_init = 0.8 - 0.6 * math.exp(-0.3 * l)
        w = w_in[l]
        wa = jnp.concatenate([col(w, 0), col(w, 1), col(w, 2), col(w, 6), col(w, 7), col(w, 8)], axis=1).astype(BF16)
        wb = jnp.concatenate([col(w, 3), col(w, 9), col(w, 10), col(w, 4), col(w, 5),
                              jnp.zeros((d, LANES - 2 * GLA_RANK), F32)], axis=1).astype(BF16)
        q_lo = 2 * gk + gv
        pa, pb, h, kgt = _ln_proj(cur, g_in, b_in, wa, wb, tabs, seq, 512, 640,
                                  (q_lo, q_lo + dq), (q_lo + dq, q_lo + 2 * dq),
                                  DIFF_DH ** -0.5 * math.log2(math.e), (gk, 2 * gk))

        lrt = pb[:, 3 * d:3 * d + 2 * GLA_RANK].reshape(batch, seq, 2 * GLA_RANK).transpose(0, 2, 1)
        zr = jnp.zeros((GLA_RANK, gk), F32)
        pad = jnp.zeros((LANES - 2 * GLA_RANK, gk), F32)
        wf_f = jnp.concatenate([gla_wf_fwd[l], zr, pad], axis=0)
        wf_b = jnp.concatenate([zr, gla_wf_bwd[l], pad], axis=0)
        o_f, o_b = _gla(
            pa, kgt, pb, lrt,
            (wf_f, wf_f[:2 * GLA_RANK].T, gla_bf_fwd[l].reshape(1, gk), gla_bf_fwd[l].reshape(gk, 1)),
            (wf_b, wf_b[:2 * GLA_RANK].T, gla_bf_bwd[l].reshape(1, gk), gla_bf_bwd[l].reshape(gk, 1)),
            batch, seq)

        o_diff = _diff_attn(pa, diff_lq1[l], diff_lk1[l], diff_lq2[l], diff_lk2[l], diff_norm_g[l],
                            batch, seq, lam_init)

        rw = jnp.concatenate([router_w[l], jnp.zeros((d, LANES - N_EXPERTS), F32)], axis=1)
        rb = jnp.concatenate([router_b[l], jnp.full((LANES - N_EXPERTS,), -jnp.inf, F32)]).reshape(1, LANES)
        h1, route, routet, counts = _merge(
            o_f, o_b, pb, o_diff, h, jnp.tile(gla_norm_g[l], GLA_HEADS).reshape(1, gv),
            w_br_gla[l].astype(BF16), w_br_diff[l].astype(BF16), w_out[l].astype(BF16),
            ln1_g[l].reshape(1, d), ln1_b[l].reshape(1, d), rw, rb, dn_alpha)

        seg_tables, tail_tables, blk_e, n_used, lo_f = _moe_tables(
            counts[:, 0, :N_EXPERTS].astype(jnp.int32), tm, tb, n_rows)
        xb = _dispatch(seg_tables + tail_tables, h1, routet, lo_f[:, :, None], n_rows)
        yb = _experts(blk_e, n_used, xb, exp_w_gate[l], exp_b_gate[l], exp_w_up[l], exp_b_up[l],
                      exp_w_down[l], exp_b_down[l], tb)
        cur = _combine(seg_tables, yb, h1, route, lo_f[:, None, :], ln2_g[l], ln2_b[l], dn_alpha)
    return cur.reshape(batch, seq, d)
```

```python
import functools
import math

import jax
import jax.numpy as jnp
from jax import lax
from jax.experimental import pallas as pl
from jax.experimental.pallas import tpu as pltpu

F32 = jnp.float32
BF16 = jnp.bfloat16
HIGHEST = lax.Precision.HIGHEST

GLA_HEADS = 4
GLA_DK = 128
GLA_DV = 256
GLA_RANK = 16
GLA_TAU = 16.0
GLA_CHUNK = 64
DIFF_HEADS = 8
DIFF_DH = 64
ROT_DIM = DIFF_DH // 4
ROPE_THETA = 500000.0
N_EXPERTS = 32
TOP_K = 4
SWIGLU_LIMIT = 7.0
SWIGLU_ALPHA = 1.702
EPS = 1e-5

LANES = 128
SUBLANES = 8
VMEM_LIMIT = 48 * 1024 * 1024

ROUTE_E = 0
ROUTE_G = TOP_K
ROUTE_R = 2 * TOP_K
MOE_SEG = SUBLANES


def _cparams(sem):
    return pltpu.CompilerParams(dimension_semantics=sem, vmem_limit_bytes=VMEM_LIMIT)


def _layer_norm(x, g, b):
    mu = jnp.mean(x, axis=-1, keepdims=True)
    xc = x - mu
    var = jnp.mean(xc * xc, axis=-1, keepdims=True)
    return xc * lax.rsqrt(var + EPS) * g + b


def _sigmoid(x):
    return 1.0 / (1.0 + jnp.exp(-x))


def _log_sigmoid(x):
    return jnp.minimum(x, 0.0) - jnp.log(1.0 + jnp.exp(-jnp.abs(x)))


def _lane_tile(x, n):
    return jnp.concatenate([x] * n, axis=1)


def _split3(x):
    hi = x.astype(BF16)
    r1 = x - hi.astype(F32)
    mid = r1.astype(BF16)
    lo = (r1 - mid.astype(F32)).astype(BF16)
    return hi, mid, lo


def _rope(x, c, a, b):
    return (x * c + pltpu.roll(x, LANES - ROT_DIM // 2, 1) * a + pltpu.roll(x, ROT_DIM // 2, 1) * b)


def _ln_proj_kernel(x_ref, g_ref, b_ref, wa_ref, wb_ref, c_ref, a_ref, r_ref, pa_ref, pb_ref, h_ref, kt_ref,
                    *, tna, tnb, q_cols, k_cols, q_scale, t_cols):
    h = _layer_norm(x_ref[...], g_ref[...], b_ref[...])
    h_ref[...] = h
    h16 = h.astype(BF16)
    for j in range(wa_ref.shape[1] // tna):
        res = jnp.dot(h16, wa_ref[:, j * tna:(j + 1) * tna], preferred_element_type=F32)
        for c in range(0, tna, LANES):
            col = j * tna + c
            blk = res[:, c:c + LANES]
            if q_cols[0] <= col < q_cols[1]:
                blk = _rope(blk, c_ref[...], a_ref[...], r_ref[...]) * q_scale
            elif k_cols[0] <= col < k_cols[1]:
                blk = _rope(blk, c_ref[...], a_ref[...], r_ref[...])
            pa_ref[:, col:col + LANES] = blk.astype(pa_ref.dtype)
            if t_cols[0] <= col < t_cols[1]:
                kt_ref[col - t_cols[0]:col - t_cols[0] + LANES, :] = blk.T.astype(kt_ref.dtype)
    for j in range(wb_ref.shape[1] // tnb):
        cols = slice(j * tnb, (j + 1) * tnb)
        pb_ref[:, cols] = jnp.dot(h16, wb_ref[:, cols], preferred_element_type=F32)


def _ln_proj(x, g, b, wa16, wb16, tabs, seq, tna, tnb, q_cols, k_cols, q_scale, t_cols):
    t, d = x.shape
    na, nb = wa16.shape[1], wb16.shape[1]
    tm = min(256, t)
    nrow = seq // tm
    tw = t_cols[1] - t_cols[0]
    row = lambda n: pl.BlockSpec((tm, n), lambda i: (i, 0))
    full = lambda r, n: pl.BlockSpec((r, n), lambda i: (0, 0))
    tab = pl.BlockSpec((tm, LANES), lambda i: (i % nrow, 0))
    return pl.pallas_call(
        functools.partial(_ln_proj_kernel, tna=tna, tnb=tnb, q_cols=q_cols, k_cols=k_cols, q_scale=q_scale,
                          t_cols=t_cols),
        out_shape=[jax.ShapeDtypeStruct((t, na), BF16), jax.ShapeDtypeStruct((t, nb), F32),
                   jax.ShapeDtypeStruct((t, d), F32), jax.ShapeDtypeStruct((t // seq, tw, seq), BF16)],
        grid=(t // tm,),
        in_specs=[row(d), full(1, d), full(1, d), full(d, na), full(d, nb), tab, tab, tab],
        out_specs=[row(na), row(nb), row(d), pl.BlockSpec((None, tw, tm), lambda i: (i // nrow, 0, i % nrow))],
        compiler_params=_cparams(("parallel",)),
        name="ln_proj",
    )(x, g.reshape(1, d), b.reshape(1, d), wa16, wb16, *tabs)


def _gla_kernel(*refs, n_chunks, batch):
    ins, (of_ref, ob_ref, sf_ref, sb_ref) = refs[:-4], refs[-4:]
    n_in = len(ins) // 2

    @pl.when(pl.program_id(1) == 0)
    def _():
        sf_ref[...] = jnp.zeros_like(sf_ref)
        sb_ref[...] = jnp.zeros_like(sb_ref)

    chains = []
    for bb in range(batch):
        chains.append(_gla_chain(*ins[:n_in], of_ref, sf_ref, bb=bb, rev=False, n_chunks=n_chunks))
        chains.append(_gla_chain(*ins[n_in:], ob_ref, sb_ref, bb=bb, rev=True, n_chunks=n_chunks))
    for _ in zip(*chains):
        pass


def _gla_chain(q_ref, kt_ref, v_ref, lr_ref, lrt_ref, wf_ref, wft_ref, bf_ref, bft_ref,
               o_ref, s_ref, *, bb, rev, n_chunks):
    c_len = GLA_CHUNK

    z = jnp.dot(lr_ref[bb].astype(BF16), wf_ref[...].astype(BF16), preferred_element_type=F32) + bf_ref[...]
    logf = _log_sigmoid(z) * (1.0 / GLA_TAU)
    zt = jnp.dot(wft_ref[...].astype(BF16), lrt_ref[bb].astype(BF16), preferred_element_type=F32) + bft_ref[...]
    logft = _log_sigmoid(zt) * (1.0 / GLA_TAU)
    yield

    r = n_chunks * c_len
    shift = c_len.bit_length() - 1
    ri = lax.broadcasted_iota(jnp.int32, (r, r), 0)
    ci = lax.broadcasted_iota(jnp.int32, (r, r), 1)
    same = jnp.right_shift(ri, shift) == jnp.right_shift(ci, shift)
    lower = same & (ri >= ci)
    upper = same & (ri <= ci)
    keep = upper if rev else lower
    tri = keep.astype(BF16)
    tri_t = (lower if rev else upper).astype(BF16)
    blk = same.astype(BF16)
    q_scale = GLA_DK ** -0.5

    b = sum(jnp.dot(tri, part, preferred_element_type=F32) for part in _split3(logf))
    bts = sum(jnp.dot(part, jnp.concatenate([tri_t, blk], axis=1), preferred_element_type=F32)
              for part in _split3(logft))
    bt, bt_tot = bts[:, :r], bts[:, r:]
    yield
    q_in = (q_ref[bb].astype(F32) * q_scale * jnp.exp(b)).astype(BF16)
    kt = kt_ref[bb].astype(F32)
    k_in_t = (kt * jnp.exp(-bt)).astype(BF16)
    k_st_t = (kt * jnp.exp(bt_tot - bt)).astype(BF16)
    decay = jnp.exp(bt_tot)
    v = v_ref[bb]
    yield
    att = jnp.dot(q_in, k_in_t, preferred_element_type=F32)
    att = jnp.where(keep, att, 0.0).astype(BF16)
    yield
    o_intra = jnp.dot(att, v, preferred_element_type=F32)
    yield

    order = range(n_chunks - 1, -1, -1) if rev else range(n_chunks)
    for c in order:
        rows = slice(c * c_len, (c + 1) * c_len)
        state = s_ref[bb]
        o_ref[bb, rows, :] = o_intra[rows, :] + jnp.dot(q_in[rows, :], state.astype(BF16),
                                                         preferred_element_type=F32)
        s_ref[bb] = (decay[:, c * c_len:c * c_len + 1] * state
                     + jnp.dot(k_st_t[:, rows], v[rows, :], preferred_element_type=F32))
        yield


def _gla(pa, kgt, pb, lrt, gates_f, gates_b, batch, seq):
    t = pa.shape[0]
    n_chunks = 4 if seq % (4 * GLA_CHUNK) == 0 else 1
    r = n_chunks * GLA_CHUNK
    ng = seq // r
    dk, dv, h = GLA_DK, GLA_DV, GLA_HEADS
    v_off = (2 * h * dk) // dv
    lr_blk = (3 * h * dv) // LANES

    def specs(gi):
        return [
            pl.BlockSpec((batch, r, dk), lambda hh, g: (0, gi(g), hh)),
            pl.BlockSpec((batch, dk, r), lambda hh, g: (0, hh, gi(g))),
            pl.BlockSpec((batch, r, dv), lambda hh, g: (0, gi(g), v_off + hh)),
            pl.BlockSpec((batch, r, LANES), lambda hh, g: (0, gi(g), lr_blk)),
            pl.BlockSpec((batch, 2 * GLA_RANK, r), lambda hh, g: (0, 0, gi(g))),
            pl.BlockSpec((LANES, dk), lambda hh, g: (0, hh)),
            pl.BlockSpec((dk, 2 * GLA_RANK), lambda hh, g: (hh, 0)),
            pl.BlockSpec((1, dk), lambda hh, g: (0, hh)),
            pl.BlockSpec((dk, 1), lambda hh, g: (hh, 0)),
        ]

    fwd = lambda g: g
    bwd = lambda g: ng - 1 - g
    out = lambda gi: pl.BlockSpec((batch, r, dv), lambda hh, g: (0, gi(g), hh))
    pa3 = pa.reshape(batch, seq, pa.shape[1])
    pb3 = pb.reshape(batch, seq, pb.shape[1])
    o_f, o_b = pl.pallas_call(
        functools.partial(_gla_kernel, n_chunks=n_chunks, batch=batch),
        out_shape=[jax.ShapeDtypeStruct((batch, seq, h * dv), F32)] * 2,
        grid=(h, ng),
        in_specs=specs(fwd) + specs(bwd),
        out_specs=[out(fwd), out(bwd)],
        scratch_shapes=[pltpu.VMEM((batch, dk, dv), F32)] * 2,
        compiler_params=_cparams(("parallel", "arbitrary")),
        name="gla",
    )(pa3, kgt, pa3, pb3, lrt, *gates_f, pa3, kgt, pa3, pb3, lrt, *gates_b)
    return o_f.reshape(t, h * dv), o_b.reshape(t, h * dv)


def _diff_attn_kernel(q_s, k_ref, v_ref, lq1_ref, lk1_ref, lq2_ref, lk2_ref, g_ref, o_ref,
                      kt_s, ve_s, m_s, acc_s, sa_s, sb_s, ma_s, mb_s, aa_s, ab_s, *, seq, tk, lam_init):
    dh = DIFF_DH
    hw = 2 * dh

    @pl.when(pl.program_id(2) == 0)
    def _():
        def body(i, carry):
            rows = pl.ds(pl.multiple_of(i * tk, tk), tk)
            kt_s[:, rows] = k_ref[rows, :].astype(F32).T.astype(BF16)
            ve_s[rows, :] = jnp.concatenate([v_ref[rows, :], jnp.ones((tk, hw), BF16)], axis=1)
            return carry
        lax.fori_loop(0, seq // tk, body, 0)

    m_s[...] = jnp.full_like(m_s, -jnp.inf)
    acc_s[...] = jnp.zeros_like(acc_s)

    def scores(i, buf):
        s_buf, mx_buf, al_buf = buf
        cols = pl.ds(pl.multiple_of(i * tk, tk), tk)
        kt = kt_s[:, cols]
        for p in range(2):
            s = jnp.dot(q_s[:, p * dh:(p + 1) * dh], kt[p * dh:(p + 1) * dh, :], preferred_element_type=F32)
            s_buf[p] = s
            m_old = m_s[p]
            m_new = jnp.maximum(m_old, jnp.max(s, axis=-1, keepdims=True))
            al_buf[p] = jnp.exp2(m_old - m_new)
            mx_buf[p] = m_new
            m_s[p] = m_new

    def softmax_pv(i, buf):
        s_buf, mx_buf, al_buf = buf
        ve = ve_s[pl.ds(pl.multiple_of(i * tk, tk), tk), :]
        for p in range(2):
            pr = jnp.exp2((s_buf[p] - _lane_tile(mx_buf[p], tk // LANES)).astype(BF16))
            acc_s[p] = (_lane_tile(al_buf[p], 2) * acc_s[p]
                        + jnp.dot(pr, ve, preferred_element_type=F32))

    n_t = seq // tk
    buf_a, buf_b = (sa_s, ma_s, aa_s), (sb_s, mb_s, ab_s)
    scores(0, buf_a)

    def pair_body(j, carry):
        scores(2 * j + 1, buf_b)
        softmax_pv(2 * j, buf_a)
        scores(2 * j + 2, buf_a)
        softmax_pv(2 * j + 1, buf_b)
        return carry
    lax.fori_loop(0, (n_t - 1) // 2, pair_body, 0)
    if n_t % 2 == 0:
        scores(n_t - 1, buf_b)
        softmax_pv(n_t - 2, buf_a)
        softmax_pv(n_t - 1, buf_b)
    else:
        softmax_pv(n_t - 1, buf_a)

    lam = (jnp.exp(jnp.sum(lq1_ref[...] * lk1_ref[...], axis=-1, keepdims=True))
           - jnp.exp(jnp.sum(lq2_ref[...] * lk2_ref[...], axis=-1, keepdims=True)) + lam_init)
    o = acc_s[0, :, :hw] / acc_s[0, :, hw:] - lam * (acc_s[1, :, :hw] / acc_s[1, :, hw:])
    ms = jnp.mean(o * o, axis=-1, keepdims=True)
    o_ref[...] = (o * lax.rsqrt(ms + EPS) * g_ref[...] * (1.0 - lam_init)).astype(o_ref.dtype)


def _diff_attn(pa, lq1, lk1, lq2, lk2, norm_g, batch, seq, lam_init):
    t = pa.shape[0]
    h, dh = DIFF_HEADS, DIFF_DH
    hw = 2 * dh
    tq = min(1024, seq)
    tk = min(512, seq)
    nq = seq // tq
    q_off = (2 * GLA_HEADS * GLA_DK + GLA_HEADS * GLA_DV) // hw
    k_off = q_off + h
    v_off = k_off + h
    vec = lambda n: pl.BlockSpec((1, n), lambda b, hh, qi: (0, 0))
    return pl.pallas_call(
        functools.partial(_diff_attn_kernel, seq=seq, tk=tk, lam_init=lam_init),
        out_shape=jax.ShapeDtypeStruct((t, h * hw), BF16),
        grid=(batch, h, nq),
        in_specs=[
            pl.BlockSpec((tq, hw), lambda b, hh, qi: (b * nq + qi, q_off + hh)),
            pl.BlockSpec((seq, hw), lambda b, hh, qi: (b, k_off + hh)),
            pl.BlockSpec((seq, hw), lambda b, hh, qi: (b, v_off + hh)),
            vec(dh), vec(dh), vec(dh), vec(dh), vec(hw),
        ],
        out_specs=pl.BlockSpec((tq, hw), lambda b, hh, qi: (b * nq + qi, hh)),
        scratch_shapes=[
            pltpu.VMEM((hw, seq), BF16),
            pltpu.VMEM((seq, 2 * hw), BF16),
            pltpu.VMEM((2, tq, LANES), F32),
            pltpu.VMEM((2, tq, 2 * hw), F32),
            pltpu.VMEM((2, tq, tk), F32),
            pltpu.VMEM((2, tq, tk), F32),
        ] + [pltpu.VMEM((2, tq, LANES), F32)] * 4,
        compiler_params=_cparams(("parallel", "parallel", "arbitrary")),
        name="diff_attn",
    )(pa, pa, pa, lq1.reshape(1, dh), lk1.reshape(1, dh), lq2.reshape(1, dh),
      lk2.reshape(1, dh), norm_g.reshape(1, hw))


def _rope_tables(seq):
    half = ROT_DIM // 2
    pos = jnp.arange(seq, dtype=F32)
    inv_freq = jnp.power(ROPE_THETA, -jnp.arange(0, ROT_DIM, 2, dtype=F32) / ROT_DIM)
    ang = pos[:, None] * inv_freq[None, :]
    cos, sin = jnp.cos(ang), jnp.sin(ang)
    ones = jnp.ones((seq, DIFF_DH - ROT_DIM), F32)
    zeros = jnp.zeros((seq, DIFF_DH - ROT_DIM), F32)
    zh = jnp.zeros((seq, half), F32)
    c = jnp.concatenate([cos, cos, ones], axis=1)
    a = jnp.concatenate([-sin, zh, zeros], axis=1)
    b = jnp.concatenate([zh, sin, zeros], axis=1)
    return tuple(jnp.tile(m, (1, 2)) for m in (c, a, b))


def _merge_kernel(of_ref, ob_ref, pb_rg, od_ref, pb_gg, pb_gd, h_ref, gn_ref, wa_ref, wb_ref, wo_ref,
                  g1_ref, b1_ref, rwh_ref, rwl_ref, rb_ref, h1_ref, route_ref, routet_ref, cnt_ref,
                  *, dn_alpha):
    tm = of_ref.shape[0]
    og = of_ref[...] + ob_ref[...]
    parts = []
    for hh in range(GLA_HEADS):
        xh = og[:, hh * GLA_DV:(hh + 1) * GLA_DV]
        ms = jnp.mean(xh * xh, axis=-1, keepdims=True)
        parts.append(xh * lax.rsqrt(ms + EPS))
    rg = pb_rg[...]
    o_gla = jnp.concatenate(parts, axis=1) * gn_ref[...] * (rg * _sigmoid(rg))
    a = jnp.dot(o_gla.astype(BF16), wa_ref[...], preferred_element_type=F32)
    bb = jnp.dot(od_ref[...], wb_ref[...], preferred_element_type=F32)
    mixed = _sigmoid(pb_gg[...]) * a + _sigmoid(pb_gd[...]) * bb
    mix = jnp.dot(mixed.astype(BF16), wo_ref[...], preferred_element_type=F32)
    h1 = _layer_norm(dn_alpha * h_ref[...] + mix, g1_ref[...], b1_ref[...])
    h1_ref[...] = h1

    hh = h1.astype(BF16)
    hl = (h1 - hh.astype(F32)).astype(BF16)
    logits = (jnp.dot(hh, rwh_ref[...], preferred_element_type=F32)
              + jnp.dot(hh, rwl_ref[...], preferred_element_type=F32)
              + jnp.dot(hl, rwh_ref[...], preferred_element_type=F32)) + rb_ref[...]
    lane = lax.broadcasted_iota(jnp.int32, (tm, LANES), 1).astype(F32)
    work = logits
    top_v, top_i, sels = [], [], []
    for _ in range(TOP_K):
        m = jnp.max(work, axis=-1, keepdims=True)
        idx = jnp.min(jnp.where(work == m, lane, float(LANES)), axis=-1, keepdims=True)
        sel = lane == idx
        top_v.append(m)
        top_i.append(idx)
        sels.append(sel)
        work = jnp.where(sel, -jnp.inf, work)
    exps = [jnp.exp(v - top_v[0]) for v in top_v]
    denom = exps[0]
    for e in exps[1:]:
        denom = denom + e
    onehot = sels[0].astype(F32)
    for s in sels[1:]:
        onehot = onehot + s.astype(F32)
    ri = lax.broadcasted_iota(jnp.int32, (tm, tm), 0)
    ci = lax.broadcasted_iota(jnp.int32, (tm, tm), 1)
    strict = (ci < ri).astype(BF16)
    before = jnp.dot(strict, onehot.astype(BF16), preferred_element_type=F32)
    route = jnp.zeros((tm, LANES), F32)
    for k in range(TOP_K):
        rank = jnp.sum(jnp.where(sels[k], before, 0.0), axis=-1, keepdims=True)
        route = jnp.where(lane == float(ROUTE_E + k), top_i[k], route)
        route = jnp.where(lane == float(ROUTE_G + k), exps[k] / denom, route)
        route = jnp.where(lane == float(ROUTE_R + k), rank, route)
    route_ref[...] = route
    routet_ref[...] = route.T
    cnt_ref[...] = jnp.sum(onehot, axis=0, keepdims=True)


def _merge(o_f, o_b, pb, o_diff, h, gn, wa, wb, wo, g1, b1, rw, rb, dn_alpha):
    t, d = h.shape
    tm = _moe_tile(t)
    row = lambda c: pl.BlockSpec((tm, d), lambda i: (i, c))
    full = lambda shp: pl.BlockSpec(shp, lambda i: (0, 0))
    rwh = rw.astype(BF16)
    rwl = (rw - rwh.astype(F32)).astype(BF16)
    return pl.pallas_call(
        functools.partial(_merge_kernel, dn_alpha=dn_alpha),
        out_shape=[jax.ShapeDtypeStruct((t, d), F32),
                   jax.ShapeDtypeStruct((t, LANES), F32),
                   jax.ShapeDtypeStruct((LANES, t), F32),
                   jax.ShapeDtypeStruct((t // tm, 1, LANES), F32)],
        grid=(t // tm,),
        in_specs=[row(0), row(0), row(0), row(0), row(1), row(2), row(0),
                  full((1, d)), full((d, d)), full((d, d)), full((d, d)),
                  full((1, d)), full((1, d)), full((d, LANES)), full((d, LANES)), full((1, LANES))],
        out_specs=[pl.BlockSpec((tm, d), lambda i: (i, 0)),
                   pl.BlockSpec((tm, LANES), lambda i: (i, 0)),
                   pl.BlockSpec((LANES, tm), lambda i: (0, i)),
                   pl.BlockSpec((None, 1, LANES), lambda i: (i, 0, 0))],
        compiler_params=_cparams(("parallel",)),
        name="merge_ln1_router",
    )(o_f, o_b, pb, o_diff, pb, pb, h, gn, wa, wb, wo, g1, b1, rwh, rwl, rb)


def _moe_tile(t):
    return min(256, t)


def _moe_slots(tm):
    return tm * TOP_K + N_EXPERTS * MOE_SEG


def _piece_loops(j, lo_s, gb_s, np_s, make_copy):
    def per_expert(e, carry):
        idx = j * N_EXPERTS + e
        base, g = lo_s[idx], gb_s[idx]

        def per_piece(p, c):
            off = p * MOE_SEG
            make_copy(pl.multiple_of(base + off, MOE_SEG), pl.multiple_of(g + off, MOE_SEG)).start()
            return c
        lax.fori_loop(0, np_s[idx], per_piece, 0)
        return carry
    lax.fori_loop(0, N_EXPERTS, per_expert, 0)


def _wait_pieces(n, make_copy):
    def body(p, c):
        make_copy(0, 0).wait()
        return c
    lax.fori_loop(0, n, body, 0)


def _dispatch_kernel(lo_s, gb_s, np_s, npt_s, ts_s, tnp_s, h_ref, routet_ref, locol_ref, xb_ref,
                     sorted_s, zero_s, sem, zsem, *, nt):
    tm = h_ref.shape[0]
    n_slots = sorted_s.shape[1]
    j = pl.program_id(0)
    par = j % 2

    def seg_copy(pp):
        def make(local_row, global_row):
            return pltpu.make_async_copy(sorted_s.at[pp, pl.ds(local_row, MOE_SEG)],
                                         xb_ref.at[pl.ds(global_row, MOE_SEG)], sem.at[pp])
        return make

    def zero_copy(_, global_row):
        return pltpu.make_async_copy(zero_s, xb_ref.at[pl.ds(global_row, MOE_SEG)], zsem)

    @pl.when(j == 0)
    def _():
        zero_s[...] = jnp.zeros_like(zero_s)

        def per_region(e, carry):
            def per_piece(p, c):
                zero_copy(0, pl.multiple_of(ts_s[e] + p * MOE_SEG, MOE_SEG)).start()
                return c
            lax.fori_loop(0, tnp_s[e], per_piece, 0)
            return carry
        lax.fori_loop(0, N_EXPERTS + 1, per_region, 0)

    routet = routet_ref[...]
    sub = lax.broadcasted_iota(jnp.int32, (LANES, tm), 0).astype(F32)
    slot_iota = lax.broadcasted_iota(jnp.int32, (n_slots, tm), 0).astype(F32)
    perm = jnp.zeros((n_slots, tm), F32)
    for k in range(TOP_K):
        e_row = routet[ROUTE_E + k:ROUTE_E + k + 1, :]
        lo_k = jnp.sum(jnp.where(sub == e_row, locol_ref[...], 0.0), axis=0, keepdims=True)
        slot = lo_k + routet[ROUTE_R + k:ROUTE_R + k + 1, :]
        perm = jnp.where(slot_iota == slot, 1.0, perm)
    sorted_s[par] = jnp.dot(perm.astype(BF16), h_ref[...].astype(BF16), preferred_element_type=F32)

    _piece_loops(j, lo_s, gb_s, np_s, seg_copy(par))

    @pl.when(j > 0)
    def _():
        _wait_pieces(npt_s[j - 1], seg_copy(1 - par))

    @pl.when(j == nt - 1)
    def _():
        _wait_pieces(npt_s[j], seg_copy(par))
        _wait_pieces(tnp_s[N_EXPERTS + 1], zero_copy)


def _dispatch(tables, h1, routet, lo_col, n_rows):
    t, d = h1.shape
    tm = _moe_tile(t)
    nt = t // tm
    n_slots = _moe_slots(tm)
    return pl.pallas_call(
        functools.partial(_dispatch_kernel, nt=nt),
        out_shape=jax.ShapeDtypeStruct((n_rows, d), F32),
        grid_spec=pltpu.PrefetchScalarGridSpec(
            num_scalar_prefetch=len(tables),
            grid=(nt,),
            in_specs=[
                pl.BlockSpec((tm, d), lambda i, *_: (i, 0)),
                pl.BlockSpec((LANES, tm), lambda i, *_: (0, i)),
                pl.BlockSpec((None, LANES, 1), lambda i, *_: (i, 0, 0)),
            ],
            out_specs=pl.BlockSpec(memory_space=pl.ANY),
            scratch_shapes=[pltpu.VMEM((2, n_slots, d), F32), pltpu.VMEM((MOE_SEG, d), F32),
                            pltpu.SemaphoreType.DMA((2,)), pltpu.SemaphoreType.DMA],
        ),
        compiler_params=_cparams(("arbitrary",)),
        name="moe_dispatch",
    )(*tables, h1, routet, lo_col)


def _experts_kernel(be_ref, nu_ref, nx_ref, x_ref, wg_hbm, bg_ref, wu_hbm, bu_ref, wd_hbm, bd_ref, y_ref,
                    stage_s, wg_s, wu_s, wd_s, sem):
    i = pl.program_id(0)

    def fetch(e):
        return [pltpu.make_async_copy(w.at[e], stage_s.at[n], sem.at[n])
                for n, w in enumerate((wg_hbm, wu_hbm, wd_hbm))]

    @pl.when(i >= nu_ref[0])
    def _():
        y_ref[...] = jnp.zeros_like(y_ref)

    @pl.when(i < nu_ref[0])
    def _():
        e = be_ref[i]
        prev = be_ref[jnp.maximum(i - 1, 0)]

        @pl.when(i == 0)
        def _():
            for c in fetch(e):
                c.start()

        @pl.when((i == 0) | (e != prev))
        def _():
            for c in fetch(e):
                c.wait()
            wg_s[...] = stage_s[0].astype(BF16)
            wu_s[...] = stage_s[1].astype(BF16)
            wd_s[...] = stage_s[2].astype(BF16)

            @pl.when(nx_ref[i] != e)
            def _():
                for c in fetch(nx_ref[i]):
                    c.start()

        x = x_ref[...].astype(BF16)
        g = jnp.minimum(jnp.dot(x, wg_s[...], preferred_element_type=F32) + bg_ref[...], SWIGLU_LIMIT)
        u = jnp.clip(jnp.dot(x, wu_s[...], preferred_element_type=F32) + bu_ref[...],
                     -SWIGLU_LIMIT, SWIGLU_LIMIT)
        a = g * _sigmoid(SWIGLU_ALPHA * g) * (u + 1.0)
        y_ref[...] = jnp.dot(a.astype(BF16), wd_s[...], preferred_element_type=F32) + bd_ref[...]


def _experts(blk_e, n_used, blk_next, xb, wg, bg, wu, bu, wd, bd, tb):
    p, d = xb.shape
    ne, _, dff = wg.shape
    assert d == dff, (d, dff)
    nb = p // tb
    rowblk = lambda i, be, nu, nx: (jnp.maximum(jnp.minimum(i, nu[0] - 1), 0), 0)
    bspec = lambda n: pl.BlockSpec((None, 1, n), lambda i, be, nu, nx: (be[i], 0, 0))
    hbm = pl.BlockSpec(memory_space=pl.ANY)
    return pl.pallas_call(
        _experts_kernel,
        out_shape=jax.ShapeDtypeStruct((p, d), F32),
        grid_spec=pltpu.PrefetchScalarGridSpec(
            num_scalar_prefetch=3,
            grid=(nb,),
            in_specs=[pl.BlockSpec((tb, d), rowblk), hbm, bspec(dff), hbm, bspec(dff), hbm, bspec(d)],
            out_specs=pl.BlockSpec((tb, d), lambda i, be, nu, nx: (i, 0)),
            scratch_shapes=[pltpu.VMEM((3, d, dff), F32), pltpu.VMEM((d, dff), BF16),
                            pltpu.VMEM((d, dff), BF16), pltpu.VMEM((dff, d), BF16),
                            pltpu.SemaphoreType.DMA((3,))],
        ),
        compiler_params=_cparams(("arbitrary",)),
        name="moe_experts",
    )(blk_e, n_used, blk_next, xb, wg, bg.reshape(ne, 1, dff), wu, bu.reshape(ne, 1, dff), wd,
      bd.reshape(ne, 1, d))


def _combine_kernel(lo_s, gb_s, np_s, npt_s, yb_ref, h_ref, route_ref, lorow_ref, g_ref, b_ref, o_ref,
                    ybuf, sem, *, dn_alpha, nt):
    tm = h_ref.shape[0]
    n_slots = ybuf.shape[1]
    j = pl.program_id(0)
    par = j % 2

    def seg_copy(pp):
        def make(local_row, global_row):
            return pltpu.make_async_copy(yb_ref.at[pl.ds(global_row, MOE_SEG)],
                                         ybuf.at[pp, pl.ds(local_row, MOE_SEG)], sem.at[pp])
        return make

    @pl.when(j == 0)
    def _():
        ybuf[...] = jnp.zeros_like(ybuf)
        _piece_loops(j, lo_s, gb_s, np_s, seg_copy(0))

    @pl.when(j + 1 < nt)
    def _():
        _piece_loops(j + 1, lo_s, gb_s, np_s, seg_copy(1 - par))

    _wait_pieces(npt_s[j], seg_copy(par))

    route = route_ref[...]
    lane = lax.broadcasted_iota(jnp.int32, (tm, LANES), 1).astype(F32)
    slot_iota = lax.broadcasted_iota(jnp.int32, (tm, n_slots), 1).astype(F32)
    wgt = jnp.zeros((tm, n_slots), F32)
    for k in range(TOP_K):
        e_col = route[:, ROUTE_E + k:ROUTE_E + k + 1]
        lo_k = jnp.sum(jnp.where(lane == e_col, lorow_ref[...], 0.0), axis=-1, keepdims=True)
        slot = lo_k + route[:, ROUTE_R + k:ROUTE_R + k + 1]
        wgt = jnp.where(slot_iota == slot, route[:, ROUTE_G + k:ROUTE_G + k + 1], wgt)
    y = ybuf[par]
    wh = wgt.astype(BF16)
    wl = (wgt - wh.astype(F32)).astype(BF16)
    yh = y.astype(BF16)
    yl = (y - yh.astype(F32)).astype(BF16)
    ffn = (jnp.dot(wh, yh, preferred_element_type=F32) + jnp.dot(wh, yl, preferred_element_type=F32)
           + jnp.dot(wl, yh, preferred_element_type=F32))
    o_ref[...] = _layer_norm(dn_alpha * h_ref[...] + ffn, g_ref[...], b_ref[...])


def _combine(tables, yb, h1, route, lo_row, g2, b2, dn_alpha):
    t, d = h1.shape
    tm = _moe_tile(t)
    nt = t // tm
    return pl.pallas_call(
        functools.partial(_combine_kernel, dn_alpha=dn_alpha, nt=nt),
        out_shape=jax.ShapeDtypeStruct((t, d), F32),
        grid_spec=pltpu.PrefetchScalarGridSpec(
            num_scalar_prefetch=len(tables),
            grid=(nt,),
            in_specs=[
                pl.BlockSpec(memory_space=pl.ANY),
                pl.BlockSpec((tm, d), lambda i, *_: (i, 0)),
                pl.BlockSpec((tm, LANES), lambda i, *_: (i, 0)),
                pl.BlockSpec((None, 1, LANES), lambda i, *_: (i, 0, 0)),
                pl.BlockSpec((1, d), lambda i, *_: (0, 0)),
                pl.BlockSpec((1, d), lambda i, *_: (0, 0)),
            ],
            out_specs=pl.BlockSpec((tm, d), lambda i, *_: (i, 0)),
            scratch_shapes=[pltpu.VMEM((2, _moe_slots(tm), d), F32), pltpu.SemaphoreType.DMA((2,))],
        ),
        compiler_params=_cparams(("arbitrary",)),
        name="moe_combine_ln2",
    )(*tables, yb, h1, route, lo_row, g2.reshape(1, d), b2.reshape(1, d))


def _moe_tables(counts, tm, tb, n_rows):
    seg = ((counts + MOE_SEG - 1) // MOE_SEG) * MOE_SEG
    lo = jnp.cumsum(seg, axis=1) - seg
    rows_e = jnp.sum(seg, axis=0)
    region = ((rows_e + tb - 1) // tb) * tb
    pend = jnp.cumsum(region)
    pstart = pend - region
    gbase = pstart[None, :] + jnp.cumsum(seg, axis=0) - seg
    pieces = seg // MOE_SEG
    tail_start = jnp.concatenate([pstart + rows_e, pend[-1:]])
    tail_pieces = jnp.concatenate([(region - rows_e) // MOE_SEG, (n_rows - pend[-1:]) // MOE_SEG])
    tail_pieces = jnp.concatenate([tail_pieces, jnp.sum(tail_pieces, keepdims=True)])
    i32 = lambda a: a.reshape(-1).astype(jnp.int32)
    seg_tables = (i32(lo), i32(gbase), i32(pieces), i32(jnp.sum(pieces, axis=1)))
    n_used = (pend[-1] // tb).astype(jnp.int32)
    blk = jnp.minimum(jnp.arange(n_rows // tb, dtype=jnp.int32), n_used - 1)
    blk_e = jnp.minimum(jnp.sum((pend[None, :] <= (blk * tb)[:, None]).astype(jnp.int32), axis=1),
                        N_EXPERTS - 1)
    ids = jnp.arange(N_EXPERTS, dtype=jnp.int32)
    later = jnp.where((region[None, :] > 0) & (ids[None, :] > ids[:, None]), ids[None, :], N_EXPERTS)
    nxt_e = jnp.min(later, axis=1)
    nxt_e = jnp.where(nxt_e < N_EXPERTS, nxt_e, ids).astype(jnp.int32)
    lo_f = jnp.pad(lo.astype(F32), ((0, 0), (0, LANES - N_EXPERTS)))
    return (seg_tables, (i32(tail_start), i32(tail_pieces)), (blk_e, n_used.reshape(1), nxt_e[blk_e]), lo_f)


def kernel(x, ln0_g, ln0_b, w_in, gla_wf_fwd, gla_bf_fwd, gla_wf_bwd, gla_bf_bwd, gla_norm_g, diff_lq1, diff_lk1, diff_lq2, diff_lk2, diff_norm_g, w_br_gla, w_br_diff, w_out, ln1_g, ln1_b, router_w, router_b, exp_w_gate, exp_b_gate, exp_w_up, exp_b_up, exp_w_down, exp_b_down, ln2_g, ln2_b):
    batch, seq, d = x.shape
    depth = w_in.shape[0]
    t = batch * seq
    dn_alpha = (2 * depth) ** 0.25
    gk, gv = GLA_HEADS * GLA_DK, GLA_HEADS * GLA_DV
    dq = DIFF_HEADS * 2 * DIFF_DH
    splits = (gk, gk, gv, gv, GLA_RANK, GLA_RANK, dq, dq, dq, d, d)
    cuts = [0]
    for s in splits:
        cuts.append(cuts[-1] + s)
    col = lambda w, i: w[:, cuts[i]:cuts[i + 1]]
    tabs = _rope_tables(seq)
    tm = _moe_tile(t)
    tb = _moe_tile(t)
    n_rows = t * TOP_K + (t // tm) * N_EXPERTS * (MOE_SEG - 1) + N_EXPERTS * (tb - 1)
    n_rows = ((n_rows + tb - 1) // tb) * tb

    assert depth == 1, depth
    cur = x.reshape(t, d)
    g_in, b_in = ln0_g, ln0_b
    for l in range(depth):
        lam_init = 0.8 - 0.6 * math.exp(-0.3 * l)
        w = w_in[l]
        wa = jnp.concatenate([col(w, 0), col(w, 1), col(w, 2), col(w, 6), col(w, 7), col(w, 8)], axis=1).astype(BF16)
        wb = jnp.concatenate([col(w, 3), col(w, 9), col(w, 10), col(w, 4), col(w, 5),
                              jnp.zeros((d, LANES - 2 * GLA_RANK), F32)], axis=1).astype(BF16)
        q_lo = 2 * gk + gv
        pa, pb, h, kgt = _ln_proj(cur, g_in, b_in, wa, wb, tabs, seq, 512, 640,
                                  (q_lo, q_lo + dq), (q_lo + dq, q_lo + 2 * dq),
                                  DIFF_DH ** -0.5 * math.log2(math.e), (gk, 2 * gk))

        lrt = pb[:, 3 * d:3 * d + 2 * GLA_RANK].reshape(batch, seq, 2 * GLA_RANK).transpose(0, 2, 1)
        zr = jnp.zeros((GLA_RANK, gk), F32)
        pad = jnp.zeros((LANES - 2 * GLA_RANK, gk), F32)
        wf_f = jnp.concatenate([gla_wf_fwd[l], zr, pad], axis=0)
        wf_b = jnp.concatenate([zr, gla_wf_bwd[l], pad], axis=0)
        o_f, o_b = _gla(
            pa, kgt, pb, lrt,
            (wf_f, wf_f[:2 * GLA_RANK].T, gla_bf_fwd[l].reshape(1, gk), gla_bf_fwd[l].reshape(gk, 1)),
            (wf_b, wf_b[:2 * GLA_RANK].T, gla_bf_bwd[l].reshape(1, gk), gla_bf_bwd[l].reshape(gk, 1)),
            batch, seq)

        o_diff = _diff_attn(pa, diff_lq1[l], diff_lk1[l], diff_lq2[l], diff_lk2[l], diff_norm_g[l],
                            batch, seq, lam_init)

        rw = jnp.concatenate([router_w[l], jnp.zeros((d, LANES - N_EXPERTS), F32)], axis=1)
        rb = jnp.concatenate([router_b[l], jnp.full((LANES - N_EXPERTS,), -jnp.inf, F32)]).reshape(1, LANES)
        h1, route, routet, counts = _merge(
            o_f, o_b, pb, o_diff, h, jnp.tile(gla_norm_g[l], GLA_HEADS).reshape(1, gv),
            w_br_gla[l].astype(BF16), w_br_diff[l].astype(BF16), w_out[l].astype(BF16),
            ln1_g[l].reshape(1, d), ln1_b[l].reshape(1, d), rw, rb, dn_alpha)

        seg_tables, tail_tables, blk_tables, lo_f = _moe_tables(
            counts[:, 0, :N_EXPERTS].astype(jnp.int32), tm, tb, n_rows)
        xb = _dispatch(seg_tables + tail_tables, h1, routet, lo_f[:, :, None], n_rows)
        yb = _experts(*blk_tables, xb, exp_w_gate[l], exp_b_gate[l], exp_w_up[l], exp_b_up[l],
                      exp_w_down[l], exp_b_down[l], tb)
        cur = _combine(seg_tables, yb, h1, route, lo_f[:, None, :], ln2_g[l], ln2_b[l], dn_alpha)
    return cur.reshape(batch, seq, d)
```

```python
import functools
import math

import jax
import jax.numpy as jnp
from jax import lax
from jax.experimental import pallas as pl
from jax.experimental.pallas import tpu as pltpu

F32 = jnp.float32
BF16 = jnp.bfloat16
HIGHEST = lax.Precision.HIGHEST

GLA_HEADS = 4
GLA_DK = 128
GLA_DV = 256
GLA_RANK = 16
GLA_TAU = 16.0
GLA_CHUNK = 64
DIFF_HEADS = 8
DIFF_DH = 64
ROT_DIM = DIFF_DH // 4
ROPE_THETA = 500000.0
N_EXPERTS = 32
TOP_K = 4
SWIGLU_LIMIT = 7.0
SWIGLU_ALPHA = 1.702
EPS = 1e-5

LANES = 128
SUBLANES = 8
VMEM_LIMIT = 48 * 1024 * 1024

ROUTE_E = 0
ROUTE_G = TOP_K
ROUTE_R = 2 * TOP_K
MOE_SEG = SUBLANES
MOE_BIG = 4 * MOE_SEG


def _cparams(sem):
    return pltpu.CompilerParams(dimension_semantics=sem, vmem_limit_bytes=VMEM_LIMIT)


def _layer_norm(x, g, b):
    mu = jnp.mean(x, axis=-1, keepdims=True)
    xc = x - mu
    var = jnp.mean(xc * xc, axis=-1, keepdims=True)
    return xc * lax.rsqrt(var + EPS) * g + b


def _sigmoid(x):
    return 1.0 / (1.0 + jnp.exp(-x))


def _log_sigmoid(x):
    return jnp.minimum(x, 0.0) - jnp.log(1.0 + jnp.exp(-jnp.abs(x)))


def _lane_tile(x, n):
    return jnp.concatenate([x] * n, axis=1)


def _split3(x):
    hi = x.astype(BF16)
    r1 = x - hi.astype(F32)
    mid = r1.astype(BF16)
    lo = (r1 - mid.astype(F32)).astype(BF16)
    return hi, mid, lo


def _rope(x, c, a, b):
    return (x * c + pltpu.roll(x, LANES - ROT_DIM // 2, 1) * a + pltpu.roll(x, ROT_DIM // 2, 1) * b)


def _ln_proj_kernel(x_ref, g_ref, b_ref, wa_ref, wb_ref, c_ref, a_ref, r_ref, pa_ref, pb_ref, h_ref, kt_ref,
                    *, tna, tnb, q_cols, k_cols, q_scale, t_cols):
    h = _layer_norm(x_ref[...], g_ref[...], b_ref[...])
    h_ref[...] = h
    h16 = h.astype(BF16)
    for j in range(wa_ref.shape[1] // tna):
        res = jnp.dot(h16, wa_ref[:, j * tna:(j + 1) * tna], preferred_element_type=F32)
        for c in range(0, tna, LANES):
            col = j * tna + c
            blk = res[:, c:c + LANES]
            if q_cols[0] <= col < q_cols[1]:
                blk = _rope(blk, c_ref[...], a_ref[...], r_ref[...]) * q_scale
            elif k_cols[0] <= col < k_cols[1]:
                blk = _rope(blk, c_ref[...], a_ref[...], r_ref[...])
            pa_ref[:, col:col + LANES] = blk.astype(pa_ref.dtype)
            if t_cols[0] <= col < t_cols[1]:
                kt_ref[col - t_cols[0]:col - t_cols[0] + LANES, :] = blk.T.astype(kt_ref.dtype)
    for j in range(wb_ref.shape[1] // tnb):
        cols = slice(j * tnb, (j + 1) * tnb)
        pb_ref[:, cols] = jnp.dot(h16, wb_ref[:, cols], preferred_element_type=F32)


def _ln_proj(x, g, b, wa16, wb16, tabs, seq, tna, tnb, q_cols, k_cols, q_scale, t_cols):
    t, d = x.shape
    na, nb = wa16.shape[1], wb16.shape[1]
    tm = min(256, t)
    nrow = seq // tm
    tw = t_cols[1] - t_cols[0]
    row = lambda n: pl.BlockSpec((tm, n), lambda i: (i, 0))
    full = lambda r, n: pl.BlockSpec((r, n), lambda i: (0, 0))
    tab = pl.BlockSpec((tm, LANES), lambda i: (i % nrow, 0))
    return pl.pallas_call(
        functools.partial(_ln_proj_kernel, tna=tna, tnb=tnb, q_cols=q_cols, k_cols=k_cols, q_scale=q_scale,
                          t_cols=t_cols),
        out_shape=[jax.ShapeDtypeStruct((t, na), BF16), jax.ShapeDtypeStruct((t, nb), F32),
                   jax.ShapeDtypeStruct((t, d), F32), jax.ShapeDtypeStruct((t // seq, tw, seq), BF16)],
        grid=(t // tm,),
        in_specs=[row(d), full(1, d), full(1, d), full(d, na), full(d, nb), tab, tab, tab],
        out_specs=[row(na), row(nb), row(d), pl.BlockSpec((None, tw, tm), lambda i: (i // nrow, 0, i % nrow))],
        compiler_params=_cparams(("parallel",)),
        name="ln_proj",
    )(x, g.reshape(1, d), b.reshape(1, d), wa16, wb16, *tabs)


def _gla_kernel(*refs, n_chunks, batch):
    ins, (of_ref, ob_ref, sf_ref, sb_ref) = refs[:-4], refs[-4:]
    n_in = len(ins) // 2

    @pl.when(pl.program_id(1) == 0)
    def _():
        sf_ref[...] = jnp.zeros_like(sf_ref)
        sb_ref[...] = jnp.zeros_like(sb_ref)

    chains = []
    for bb in range(batch):
        chains.append(_gla_chain(*ins[:n_in], of_ref, sf_ref, bb=bb, rev=False, n_chunks=n_chunks))
        chains.append(_gla_chain(*ins[n_in:], ob_ref, sb_ref, bb=bb, rev=True, n_chunks=n_chunks))
    for _ in zip(*chains):
        pass


def _gla_chain(q_ref, kt_ref, v_ref, lr_ref, lrt_ref, wf_ref, wft_ref, bf_ref, bft_ref,
               o_ref, s_ref, *, bb, rev, n_chunks):
    c_len = GLA_CHUNK

    z = jnp.dot(lr_ref[bb].astype(BF16), wf_ref[...].astype(BF16), preferred_element_type=F32) + bf_ref[...]
    logf = _log_sigmoid(z) * (1.0 / GLA_TAU)
    zt = jnp.dot(wft_ref[...].astype(BF16), lrt_ref[bb].astype(BF16), preferred_element_type=F32) + bft_ref[...]
    logft = _log_sigmoid(zt) * (1.0 / GLA_TAU)
    yield

    r = n_chunks * c_len
    shift = c_len.bit_length() - 1
    ri = lax.broadcasted_iota(jnp.int32, (r, r), 0)
    ci = lax.broadcasted_iota(jnp.int32, (r, r), 1)
    same = jnp.right_shift(ri, shift) == jnp.right_shift(ci, shift)
    lower = same & (ri >= ci)
    upper = same & (ri <= ci)
    keep = upper if rev else lower
    tri = keep.astype(BF16)
    tri_t = (lower if rev else upper).astype(BF16)
    blk = same.astype(BF16)
    q_scale = GLA_DK ** -0.5

    b = sum(jnp.dot(tri, part, preferred_element_type=F32) for part in _split3(logf))
    bts = sum(jnp.dot(part, jnp.concatenate([tri_t, blk], axis=1), preferred_element_type=F32)
              for part in _split3(logft))
    bt, bt_tot = bts[:, :r], bts[:, r:]
    yield
    q_in = (q_ref[bb].astype(F32) * q_scale * jnp.exp(b)).astype(BF16)
    kt = kt_ref[bb].astype(F32)
    k_in_t = (kt * jnp.exp(-bt)).astype(BF16)
    k_st_t = (kt * jnp.exp(bt_tot - bt)).astype(BF16)
    decay = jnp.exp(bt_tot)
    v = v_ref[bb]
    yield
    att = jnp.dot(q_in, k_in_t, preferred_element_type=F32)
    att = jnp.where(keep, att, 0.0).astype(BF16)
    yield
    o_intra = jnp.dot(att, v, preferred_element_type=F32)
    yield

    order = range(n_chunks - 1, -1, -1) if rev else range(n_chunks)
    for c in order:
        rows = slice(c * c_len, (c + 1) * c_len)
        state = s_ref[bb]
        o_ref[bb, rows, :] = o_intra[rows, :] + jnp.dot(q_in[rows, :], state.astype(BF16),
                                                         preferred_element_type=F32)
        s_ref[bb] = (decay[:, c * c_len:c * c_len + 1] * state
                     + jnp.dot(k_st_t[:, rows], v[rows, :], preferred_element_type=F32))
        yield


def _gla(pa, kgt, pb, lrt, gates_f, gates_b, batch, seq):
    t = pa.shape[0]
    n_chunks = 4 if seq % (4 * GLA_CHUNK) == 0 else 1
    r = n_chunks * GLA_CHUNK
    ng = seq // r
    dk, dv, h = GLA_DK, GLA_DV, GLA_HEADS
    v_off = (2 * h * dk) // dv
    lr_blk = (3 * h * dv) // LANES

    def specs(gi):
        return [
            pl.BlockSpec((batch, r, dk), lambda hh, g: (0, gi(g), hh)),
            pl.BlockSpec((batch, dk, r), lambda hh, g: (0, hh, gi(g))),
            pl.BlockSpec((batch, r, dv), lambda hh, g: (0, gi(g), v_off + hh)),
            pl.BlockSpec((batch, r, LANES), lambda hh, g: (0, gi(g), lr_blk)),
            pl.BlockSpec((batch, 2 * GLA_RANK, r), lambda hh, g: (0, 0, gi(g))),
            pl.BlockSpec((LANES, dk), lambda hh, g: (0, hh)),
            pl.BlockSpec((dk, 2 * GLA_RANK), lambda hh, g: (hh, 0)),
            pl.BlockSpec((1, dk), lambda hh, g: (0, hh)),
            pl.BlockSpec((dk, 1), lambda hh, g: (hh, 0)),
        ]

    fwd = lambda g: g
    bwd = lambda g: ng - 1 - g
    out = lambda gi: pl.BlockSpec((batch, r, dv), lambda hh, g: (0, gi(g), hh))
    pa3 = pa.reshape(batch, seq, pa.shape[1])
    pb3 = pb.reshape(batch, seq, pb.shape[1])
    o_f, o_b = pl.pallas_call(
        functools.partial(_gla_kernel, n_chunks=n_chunks, batch=batch),
        out_shape=[jax.ShapeDtypeStruct((batch, seq, h * dv), F32)] * 2,
        grid=(h, ng),
        in_specs=specs(fwd) + specs(bwd),
        out_specs=[out(fwd), out(bwd)],
        scratch_shapes=[pltpu.VMEM((batch, dk, dv), F32)] * 2,
        compiler_params=_cparams(("parallel", "arbitrary")),
        name="gla",
    )(pa3, kgt, pa3, pb3, lrt, *gates_f, pa3, kgt, pa3, pb3, lrt, *gates_b)
    return o_f.reshape(t, h * dv), o_b.reshape(t, h * dv)


def _diff_attn_kernel(q_s, k_ref, v_ref, lq1_ref, lk1_ref, lq2_ref, lk2_ref, g_ref, o_ref,
                      kt_s, ve_s, m_s, acc_s, sa_s, sb_s, ma_s, mb_s, aa_s, ab_s, *, seq, tk, lam_init):
    dh = DIFF_DH
    hw = 2 * dh

    @pl.when(pl.program_id(2) == 0)
    def _():
        def body(i, carry):
            rows = pl.ds(pl.multiple_of(i * tk, tk), tk)
            kt_s[:, rows] = k_ref[rows, :].astype(F32).T.astype(BF16)
            ve_s[rows, :] = jnp.concatenate([v_ref[rows, :], jnp.ones((tk, hw), BF16)], axis=1)
            return carry
        lax.fori_loop(0, seq // tk, body, 0)

    m_s[...] = jnp.full_like(m_s, -jnp.inf)
    acc_s[...] = jnp.zeros_like(acc_s)

    def scores(i, buf):
        s_buf, mx_buf, al_buf = buf
        cols = pl.ds(pl.multiple_of(i * tk, tk), tk)
        kt = kt_s[:, cols]
        for p in range(2):
            s = jnp.dot(q_s[:, p * dh:(p + 1) * dh], kt[p * dh:(p + 1) * dh, :], preferred_element_type=F32)
            s_buf[p] = s
            m_old = m_s[p]
            m_new = jnp.maximum(m_old, jnp.max(s, axis=-1, keepdims=True))
            al_buf[p] = jnp.exp2(m_old - m_new)
            mx_buf[p] = m_new
            m_s[p] = m_new

    def softmax_pv(i, buf):
        s_buf, mx_buf, al_buf = buf
        ve = ve_s[pl.ds(pl.multiple_of(i * tk, tk), tk), :]
        for p in range(2):
            pr = jnp.exp2((s_buf[p] - _lane_tile(mx_buf[p], tk // LANES)).astype(BF16))
            acc_s[p] = (_lane_tile(al_buf[p], 2) * acc_s[p]
                        + jnp.dot(pr, ve, preferred_element_type=F32))

    n_t = seq // tk
    buf_a, buf_b = (sa_s, ma_s, aa_s), (sb_s, mb_s, ab_s)
    scores(0, buf_a)

    def pair_body(j, carry):
        scores(2 * j + 1, buf_b)
        softmax_pv(2 * j, buf_a)
        scores(2 * j + 2, buf_a)
        softmax_pv(2 * j + 1, buf_b)
        return carry
    lax.fori_loop(0, (n_t - 1) // 2, pair_body, 0)
    if n_t % 2 == 0:
        scores(n_t - 1, buf_b)
        softmax_pv(n_t - 2, buf_a)
        softmax_pv(n_t - 1, buf_b)
    else:
        softmax_pv(n_t - 1, buf_a)

    lam = (jnp.exp(jnp.sum(lq1_ref[...] * lk1_ref[...], axis=-1, keepdims=True))
           - jnp.exp(jnp.sum(lq2_ref[...] * lk2_ref[...], axis=-1, keepdims=True)) + lam_init)
    o = acc_s[0, :, :hw] / acc_s[0, :, hw:] - lam * (acc_s[1, :, :hw] / acc_s[1, :, hw:])
    ms = jnp.mean(o * o, axis=-1, keepdims=True)
    o_ref[...] = (o * lax.rsqrt(ms + EPS) * g_ref[...] * (1.0 - lam_init)).astype(o_ref.dtype)


def _diff_attn(pa, lq1, lk1, lq2, lk2, norm_g, batch, seq, lam_init):
    t = pa.shape[0]
    h, dh = DIFF_HEADS, DIFF_DH
    hw = 2 * dh
    tq = min(1024, seq)
    tk = min(512, seq)
    nq = seq // tq
    q_off = (2 * GLA_HEADS * GLA_DK + GLA_HEADS * GLA_DV) // hw
    k_off = q_off + h
    v_off = k_off + h
    vec = lambda n: pl.BlockSpec((1, n), lambda b, hh, qi: (0, 0))
    return pl.pallas_call(
        functools.partial(_diff_attn_kernel, seq=seq, tk=tk, lam_init=lam_init),
        out_shape=jax.ShapeDtypeStruct((t, h * hw), BF16),
        grid=(batch, h, nq),
        in_specs=[
            pl.BlockSpec((tq, hw), lambda b, hh, qi: (b * nq + qi, q_off + hh)),
            pl.BlockSpec((seq, hw), lambda b, hh, qi: (b, k_off + hh)),
            pl.BlockSpec((seq, hw), lambda b, hh, qi: (b, v_off + hh)),
            vec(dh), vec(dh), vec(dh), vec(dh), vec(hw),
        ],
        out_specs=pl.BlockSpec((tq, hw), lambda b, hh, qi: (b * nq + qi, hh)),
        scratch_shapes=[
            pltpu.VMEM((hw, seq), BF16),
            pltpu.VMEM((seq, 2 * hw), BF16),
            pltpu.VMEM((2, tq, LANES), F32),
            pltpu.VMEM((2, tq, 2 * hw), F32),
            pltpu.VMEM((2, tq, tk), F32),
            pltpu.VMEM((2, tq, tk), F32),
        ] + [pltpu.VMEM((2, tq, LANES), F32)] * 4,
        compiler_params=_cparams(("parallel", "parallel", "arbitrary")),
        name="diff_attn",
    )(pa, pa, pa, lq1.reshape(1, dh), lk1.reshape(1, dh), lq2.reshape(1, dh),
      lk2.reshape(1, dh), norm_g.reshape(1, hw))


def _rope_tables(seq):
    half = ROT_DIM // 2
    pos = jnp.arange(seq, dtype=F32)
    inv_freq = jnp.power(ROPE_THETA, -jnp.arange(0, ROT_DIM, 2, dtype=F32) / ROT_DIM)
    ang = pos[:, None] * inv_freq[None, :]
    cos, sin = jnp.cos(ang), jnp.sin(ang)
    ones = jnp.ones((seq, DIFF_DH - ROT_DIM), F32)
    zeros = jnp.zeros((seq, DIFF_DH - ROT_DIM), F32)
    zh = jnp.zeros((seq, half), F32)
    c = jnp.concatenate([cos, cos, ones], axis=1)
    a = jnp.concatenate([-sin, zh, zeros], axis=1)
    b = jnp.concatenate([zh, sin, zeros], axis=1)
    return tuple(jnp.tile(m, (1, 2)) for m in (c, a, b))


def _merge_kernel(of_ref, ob_ref, pb_rg, od_ref, pb_gg, pb_gd, h_ref, gn_ref, wa_ref, wb_ref, wo_ref,
                  g1_ref, b1_ref, rwh_ref, rwl_ref, rb_ref, h1_ref, route_ref, routet_ref, cnt_ref,
                  *, dn_alpha):
    tm = of_ref.shape[0]
    og = of_ref[...] + ob_ref[...]
    parts = []
    for hh in range(GLA_HEADS):
        xh = og[:, hh * GLA_DV:(hh + 1) * GLA_DV]
        ms = jnp.mean(xh * xh, axis=-1, keepdims=True)
        parts.append(xh * lax.rsqrt(ms + EPS))
    rg = pb_rg[...]
    o_gla = jnp.concatenate(parts, axis=1) * gn_ref[...] * (rg * _sigmoid(rg))
    a = jnp.dot(o_gla.astype(BF16), wa_ref[...], preferred_element_type=F32)
    bb = jnp.dot(od_ref[...], wb_ref[...], preferred_element_type=F32)
    mixed = _sigmoid(pb_gg[...]) * a + _sigmoid(pb_gd[...]) * bb
    mix = jnp.dot(mixed.astype(BF16), wo_ref[...], preferred_element_type=F32)
    h1 = _layer_norm(dn_alpha * h_ref[...] + mix, g1_ref[...], b1_ref[...])
    h1_ref[...] = h1

    hh = h1.astype(BF16)
    hl = (h1 - hh.astype(F32)).astype(BF16)
    logits = (jnp.dot(hh, rwh_ref[...], preferred_element_type=F32)
              + jnp.dot(hh, rwl_ref[...], preferred_element_type=F32)
              + jnp.dot(hl, rwh_ref[...], preferred_element_type=F32)) + rb_ref[...]
    lane = lax.broadcasted_iota(jnp.int32, (tm, LANES), 1).astype(F32)
    work = logits
    top_v, top_i, sels = [], [], []
    for _ in range(TOP_K):
        m = jnp.max(work, axis=-1, keepdims=True)
        idx = jnp.min(jnp.where(work == m, lane, float(LANES)), axis=-1, keepdims=True)
        sel = lane == idx
        top_v.append(m)
        top_i.append(idx)
        sels.append(sel)
        work = jnp.where(sel, -jnp.inf, work)
    exps = [jnp.exp(v - top_v[0]) for v in top_v]
    denom = exps[0]
    for e in exps[1:]:
        denom = denom + e
    onehot = sels[0].astype(F32)
    for s in sels[1:]:
        onehot = onehot + s.astype(F32)
    ri = lax.broadcasted_iota(jnp.int32, (tm, tm), 0)
    ci = lax.broadcasted_iota(jnp.int32, (tm, tm), 1)
    strict = (ci < ri).astype(BF16)
    before = jnp.dot(strict, onehot.astype(BF16), preferred_element_type=F32)
    route = jnp.zeros((tm, LANES), F32)
    for k in range(TOP_K):
        rank = jnp.sum(jnp.where(sels[k], before, 0.0), axis=-1, keepdims=True)
        route = jnp.where(lane == float(ROUTE_E + k), top_i[k], route)
        route = jnp.where(lane == float(ROUTE_G + k), exps[k] / denom, route)
        route = jnp.where(lane == float(ROUTE_R + k), rank, route)
    route_ref[...] = route
    routet_ref[...] = route.T
    cnt_ref[...] = jnp.sum(onehot, axis=0, keepdims=True)


def _merge(o_f, o_b, pb, o_diff, h, gn, wa, wb, wo, g1, b1, rw, rb, dn_alpha):
    t, d = h.shape
    tm = _moe_tile(t)
    row = lambda c: pl.BlockSpec((tm, d), lambda i: (i, c))
    full = lambda shp: pl.BlockSpec(shp, lambda i: (0, 0))
    rwh = rw.astype(BF16)
    rwl = (rw - rwh.astype(F32)).astype(BF16)
    return pl.pallas_call(
        functools.partial(_merge_kernel, dn_alpha=dn_alpha),
        out_shape=[jax.ShapeDtypeStruct((t, d), F32),
                   jax.ShapeDtypeStruct((t, LANES), F32),
                   jax.ShapeDtypeStruct((LANES, t), F32),
                   jax.ShapeDtypeStruct((t // tm, 1, LANES), F32)],
        grid=(t // tm,),
        in_specs=[row(0), row(0), row(0), row(0), row(1), row(2), row(0),
                  full((1, d)), full((d, d)), full((d, d)), full((d, d)),
                  full((1, d)), full((1, d)), full((d, LANES)), full((d, LANES)), full((1, LANES))],
        out_specs=[pl.BlockSpec((tm, d), lambda i: (i, 0)),
                   pl.BlockSpec((tm, LANES), lambda i: (i, 0)),
                   pl.BlockSpec((LANES, tm), lambda i: (0, i)),
                   pl.BlockSpec((None, 1, LANES), lambda i: (i, 0, 0))],
        compiler_params=_cparams(("parallel",)),
        name="merge_ln1_router",
    )(o_f, o_b, pb, o_diff, pb, pb, h, gn, wa, wb, wo, g1, b1, rwh, rwl, rb)


def _moe_tile(t):
    return min(256, t)


def _moe_slots(tm):
    return tm * TOP_K + N_EXPERTS * MOE_SEG


def _piece_loops(j, lo_s, gb_s, nbig_s, nsmall_s, make_big, make_small):
    def per_expert(e, carry):
        idx = j * N_EXPERTS + e
        base, g, nbig = lo_s[idx], gb_s[idx], nbig_s[idx]

        def big_piece(p, c):
            off = p * MOE_BIG
            make_big(pl.multiple_of(base + off, MOE_SEG), pl.multiple_of(g + off, MOE_SEG)).start()
            return c
        lax.fori_loop(0, nbig, big_piece, 0)

        def small_piece(p, c):
            off = nbig * MOE_BIG + p * MOE_SEG
            make_small(pl.multiple_of(base + off, MOE_SEG), pl.multiple_of(g + off, MOE_SEG)).start()
            return c
        lax.fori_loop(0, nsmall_s[idx], small_piece, 0)
        return carry
    lax.fori_loop(0, N_EXPERTS, per_expert, 0)


def _wait_pieces(n, make_copy):
    def body(p, c):
        make_copy(0, 0).wait()
        return c
    lax.fori_loop(0, n, body, 0)


def _dispatch_kernel(lo_s, gb_s, nbig_s, nsmall_s, tbig_s, tsmall_s, ts_s, tnp_s, h_ref, routet_ref, locol_ref,
                     xb_ref, sorted_s, zero_s, sem_big, sem_small, zsem, *, nt):
    tm = h_ref.shape[0]
    n_slots = sorted_s.shape[1]
    j = pl.program_id(0)
    par = j % 2

    def seg_copy(pp, rows, sem):
        def make(local_row, global_row):
            return pltpu.make_async_copy(sorted_s.at[pp, pl.ds(local_row, rows)],
                                         xb_ref.at[pl.ds(global_row, rows)], sem.at[pp])
        return make

    big = lambda pp: seg_copy(pp, MOE_BIG, sem_big)
    small = lambda pp: seg_copy(pp, MOE_SEG, sem_small)

    def zero_copy(_, global_row):
        return pltpu.make_async_copy(zero_s, xb_ref.at[pl.ds(global_row, MOE_SEG)], zsem)

    @pl.when(j == 0)
    def _():
        zero_s[...] = jnp.zeros_like(zero_s)

        def per_region(e, carry):
            def per_piece(p, c):
                zero_copy(0, pl.multiple_of(ts_s[e] + p * MOE_SEG, MOE_SEG)).start()
                return c
            lax.fori_loop(0, tnp_s[e], per_piece, 0)
            return carry
        lax.fori_loop(0, N_EXPERTS + 1, per_region, 0)

    routet = routet_ref[...]
    sub = lax.broadcasted_iota(jnp.int32, (LANES, tm), 0).astype(F32)
    slot_iota = lax.broadcasted_iota(jnp.int32, (n_slots, tm), 0).astype(F32)
    perm = jnp.zeros((n_slots, tm), F32)
    for k in range(TOP_K):
        e_row = routet[ROUTE_E + k:ROUTE_E + k + 1, :]
        lo_k = jnp.sum(jnp.where(sub == e_row, locol_ref[...], 0.0), axis=0, keepdims=True)
        slot = lo_k + routet[ROUTE_R + k:ROUTE_R + k + 1, :]
        perm = jnp.where(slot_iota == slot, 1.0, perm)
    sorted_s[par] = jnp.dot(perm.astype(BF16), h_ref[...].astype(BF16), preferred_element_type=F32)

    _piece_loops(j, lo_s, gb_s, nbig_s, nsmall_s, big(par), small(par))

    @pl.when(j > 0)
    def _():
        _wait_pieces(tbig_s[j - 1], big(1 - par))
        _wait_pieces(tsmall_s[j - 1], small(1 - par))

    @pl.when(j == nt - 1)
    def _():
        _wait_pieces(tbig_s[j], big(par))
        _wait_pieces(tsmall_s[j], small(par))
        _wait_pieces(tnp_s[N_EXPERTS + 1], zero_copy)


def _dispatch(tables, h1, routet, lo_col, n_rows):
    t, d = h1.shape
    tm = _moe_tile(t)
    nt = t // tm
    n_slots = _moe_slots(tm)
    return pl.pallas_call(
        functools.partial(_dispatch_kernel, nt=nt),
        out_shape=jax.ShapeDtypeStruct((n_rows, d), F32),
        grid_spec=pltpu.PrefetchScalarGridSpec(
            num_scalar_prefetch=len(tables),
            grid=(nt,),
            in_specs=[
                pl.BlockSpec((tm, d), lambda i, *_: (i, 0)),
                pl.BlockSpec((LANES, tm), lambda i, *_: (0, i)),
                pl.BlockSpec((None, LANES, 1), lambda i, *_: (i, 0, 0)),
            ],
            out_specs=pl.BlockSpec(memory_space=pl.ANY),
            scratch_shapes=[pltpu.VMEM((2, n_slots, d), F32), pltpu.VMEM((MOE_SEG, d), F32),
                            pltpu.SemaphoreType.DMA((2,)), pltpu.SemaphoreType.DMA((2,)),
                            pltpu.SemaphoreType.DMA],
        ),
        compiler_params=_cparams(("arbitrary",)),
        name="moe_dispatch",
    )(*tables, h1, routet, lo_col)


def _experts_kernel(be_ref, nu_ref, nx_ref, x_ref, wg_hbm, bg_ref, wu_hbm, bu_ref, wd_hbm, bd_ref, y_ref,
                    stage_s, wg_s, wu_s, wd_s, sem):
    i = pl.program_id(0)

    def fetch(e):
        return [pltpu.make_async_copy(w.at[e], stage_s.at[n], sem.at[n])
                for n, w in enumerate((wg_hbm, wu_hbm, wd_hbm))]

    @pl.when(i >= nu_ref[0])
    def _():
        y_ref[...] = jnp.zeros_like(y_ref)

    @pl.when(i < nu_ref[0])
    def _():
        e = be_ref[i]
        prev = be_ref[jnp.maximum(i - 1, 0)]

        @pl.when(i == 0)
        def _():
            for c in fetch(e):
                c.start()

        @pl.when((i == 0) | (e != prev))
        def _():
            for c in fetch(e):
                c.wait()
            wg_s[...] = stage_s[0].astype(BF16)
            wu_s[...] = stage_s[1].astype(BF16)
            wd_s[...] = stage_s[2].astype(BF16)

            @pl.when(nx_ref[i] != e)
            def _():
                for c in fetch(nx_ref[i]):
                    c.start()

        x = x_ref[...].astype(BF16)
        g = jnp.minimum(jnp.dot(x, wg_s[...], preferred_element_type=F32) + bg_ref[...], SWIGLU_LIMIT)
        u = jnp.clip(jnp.dot(x, wu_s[...], preferred_element_type=F32) + bu_ref[...],
                     -SWIGLU_LIMIT, SWIGLU_LIMIT)
        a = g * _sigmoid(SWIGLU_ALPHA * g) * (u + 1.0)
        y_ref[...] = jnp.dot(a.astype(BF16), wd_s[...], preferred_element_type=F32) + bd_ref[...]


def _experts(blk_e, n_used, blk_next, xb, wg, bg, wu, bu, wd, bd, tb):
    p, d = xb.shape
    ne, _, dff = wg.shape
    assert d == dff, (d, dff)
    nb = p // tb
    rowblk = lambda i, be, nu, nx: (jnp.maximum(jnp.minimum(i, nu[0] - 1), 0), 0)
    bspec = lambda n: pl.BlockSpec((None, 1, n), lambda i, be, nu, nx: (be[i], 0, 0))
    hbm = pl.BlockSpec(memory_space=pl.ANY)
    return pl.pallas_call(
        _experts_kernel,
        out_shape=jax.ShapeDtypeStruct((p, d), F32),
        grid_spec=pltpu.PrefetchScalarGridSpec(
            num_scalar_prefetch=3,
            grid=(nb,),
            in_specs=[pl.BlockSpec((tb, d), rowblk), hbm, bspec(dff), hbm, bspec(dff), hbm, bspec(d)],
            out_specs=pl.BlockSpec((tb, d), lambda i, be, nu, nx: (i, 0)),
            scratch_shapes=[pltpu.VMEM((3, d, dff), F32), pltpu.VMEM((d, dff), BF16),
                            pltpu.VMEM((d, dff), BF16), pltpu.VMEM((dff, d), BF16),
                            pltpu.SemaphoreType.DMA((3,))],
        ),
        compiler_params=_cparams(("arbitrary",)),
        name="moe_experts",
    )(blk_e, n_used, blk_next, xb, wg, bg.reshape(ne, 1, dff), wu, bu.reshape(ne, 1, dff), wd,
      bd.reshape(ne, 1, d))


def _combine_kernel(lo_s, gb_s, nbig_s, nsmall_s, tbig_s, tsmall_s, yb_ref, h_ref, route_ref, lorow_ref,
                    g_ref, b_ref, o_ref, ybuf, sem_big, sem_small, *, dn_alpha, nt):
    tm = h_ref.shape[0]
    n_slots = ybuf.shape[1]
    j = pl.program_id(0)
    par = j % 2

    def seg_copy(pp, rows, sem):
        def make(local_row, global_row):
            return pltpu.make_async_copy(yb_ref.at[pl.ds(global_row, rows)],
                                         ybuf.at[pp, pl.ds(local_row, rows)], sem.at[pp])
        return make

    big = lambda pp: seg_copy(pp, MOE_BIG, sem_big)
    small = lambda pp: seg_copy(pp, MOE_SEG, sem_small)

    @pl.when(j == 0)
    def _():
        ybuf[...] = jnp.zeros_like(ybuf)
        _piece_loops(j, lo_s, gb_s, nbig_s, nsmall_s, big(0), small(0))

    @pl.when(j + 1 < nt)
    def _():
        _piece_loops(j + 1, lo_s, gb_s, nbig_s, nsmall_s, big(1 - par), small(1 - par))

    _wait_pieces(tbig_s[j], big(par))
    _wait_pieces(tsmall_s[j], small(par))

    route = route_ref[...]
    lane = lax.broadcasted_iota(jnp.int32, (tm, LANES), 1).astype(F32)
    slot_iota = lax.broadcasted_iota(jnp.int32, (tm, n_slots), 1).astype(F32)
    wgt = jnp.zeros((tm, n_slots), F32)
    for k in range(TOP_K):
        e_col = route[:, ROUTE_E + k:ROUTE_E + k + 1]
        lo_k = jnp.sum(jnp.where(lane == e_col, lorow_ref[...], 0.0), axis=-1, keepdims=True)
        slot = lo_k + route[:, ROUTE_R + k:ROUTE_R + k + 1]
        wgt = jnp.where(slot_iota == slot, route[:, ROUTE_G + k:ROUTE_G + k + 1], wgt)
    y = ybuf[par]
    wh = wgt.astype(BF16)
    wl = (wgt - wh.astype(F32)).astype(BF16)
    yh = y.astype(BF16)
    yl = (y - yh.astype(F32)).astype(BF16)
    ffn = (jnp.dot(wh, yh, preferred_element_type=F32) + jnp.dot(wh, yl, preferred_element_type=F32)
           + jnp.dot(wl, yh, preferred_element_type=F32))
    o_ref[...] = _layer_norm(dn_alpha * h_ref[...] + ffn, g_ref[...], b_ref[...])


def _combine(tables, yb, h1, route, lo_row, g2, b2, dn_alpha):
    t, d = h1.shape
    tm = _moe_tile(t)
    nt = t // tm
    return pl.pallas_call(
        functools.partial(_combine_kernel, dn_alpha=dn_alpha, nt=nt),
        out_shape=jax.ShapeDtypeStruct((t, d), F32),
        grid_spec=pltpu.PrefetchScalarGridSpec(
            num_scalar_prefetch=len(tables),
            grid=(nt,),
            in_specs=[
                pl.BlockSpec(memory_space=pl.ANY),
                pl.BlockSpec((tm, d), lambda i, *_: (i, 0)),
                pl.BlockSpec((tm, LANES), lambda i, *_: (i, 0)),
                pl.BlockSpec((None, 1, LANES), lambda i, *_: (i, 0, 0)),
                pl.BlockSpec((1, d), lambda i, *_: (0, 0)),
                pl.BlockSpec((1, d), lambda i, *_: (0, 0)),
            ],
            out_specs=pl.BlockSpec((tm, d), lambda i, *_: (i, 0)),
            scratch_shapes=[pltpu.VMEM((2, _moe_slots(tm), d), F32), pltpu.SemaphoreType.DMA((2,)),
                            pltpu.SemaphoreType.DMA((2,))],
        ),
        compiler_params=_cparams(("arbitrary",)),
        name="moe_combine_ln2",
    )(*tables, yb, h1, route, lo_row, g2.reshape(1, d), b2.reshape(1, d))


def _moe_tables(counts, tm, tb, n_rows):
    seg = ((counts + MOE_SEG - 1) // MOE_SEG) * MOE_SEG
    lo = jnp.cumsum(seg, axis=1) - seg
    rows_e = jnp.sum(seg, axis=0)
    region = ((rows_e + tb - 1) // tb) * tb
    pend = jnp.cumsum(region)
    pstart = pend - region
    gbase = pstart[None, :] + jnp.cumsum(seg, axis=0) - seg
    n_big = seg // MOE_BIG
    n_small = (seg - n_big * MOE_BIG) // MOE_SEG
    tail_start = jnp.concatenate([pstart + rows_e, pend[-1:]])
    tail_pieces = jnp.concatenate([(region - rows_e) // MOE_SEG, (n_rows - pend[-1:]) // MOE_SEG])
    tail_pieces = jnp.concatenate([tail_pieces, jnp.sum(tail_pieces, keepdims=True)])
    i32 = lambda a: a.reshape(-1).astype(jnp.int32)
    seg_tables = (i32(lo), i32(gbase), i32(n_big), i32(n_small),
                  i32(jnp.sum(n_big, axis=1)), i32(jnp.sum(n_small, axis=1)))
    n_used = (pend[-1] // tb).astype(jnp.int32)
    blk = jnp.minimum(jnp.arange(n_rows // tb, dtype=jnp.int32), n_used - 1)
    blk_e = jnp.minimum(jnp.sum((pend[None, :] <= (blk * tb)[:, None]).astype(jnp.int32), axis=1),
                        N_EXPERTS - 1)
    ids = jnp.arange(N_EXPERTS, dtype=jnp.int32)
    later = jnp.where((region[None, :] > 0) & (ids[None, :] > ids[:, None]), ids[None, :], N_EXPERTS)
    nxt_e = jnp.min(later, axis=1)
    nxt_e = jnp.where(nxt_e < N_EXPERTS, nxt_e, ids).astype(jnp.int32)
    lo_f = jnp.pad(lo.astype(F32), ((0, 0), (0, LANES - N_EXPERTS)))
    return (seg_tables, (i32(tail_start), i32(tail_pieces)), (blk_e, n_used.reshape(1), nxt_e[blk_e]), lo_f)


def kernel(x, ln0_g, ln0_b, w_in, gla_wf_fwd, gla_bf_fwd, gla_wf_bwd, gla_bf_bwd, gla_norm_g, diff_lq1, diff_lk1, diff_lq2, diff_lk2, diff_norm_g, w_br_gla, w_br_diff, w_out, ln1_g, ln1_b, router_w, router_b, exp_w_gate, exp_b_gate, exp_w_up, exp_b_up, exp_w_down, exp_b_down, ln2_g, ln2_b):
    batch, seq, d = x.shape
    depth = w_in.shape[0]
    t = batch * seq
    dn_alpha = (2 * depth) ** 0.25
    gk, gv = GLA_HEADS * GLA_DK, GLA_HEADS * GLA_DV
    dq = DIFF_HEADS * 2 * DIFF_DH
    splits = (gk, gk, gv, gv, GLA_RANK, GLA_RANK, dq, dq, dq, d, d)
    cuts = [0]
    for s in splits:
        cuts.append(cuts[-1] + s)
    col = lambda w, i: w[:, cuts[i]:cuts[i + 1]]
    tabs = _rope_tables(seq)
    tm = _moe_tile(t)
    tb = _moe_tile(t)
    n_rows = t * TOP_K + (t // tm) * N_EXPERTS * (MOE_SEG - 1) + N_EXPERTS * (tb - 1)
    n_rows = ((n_rows + tb - 1) // tb) * tb

    assert depth == 1, depth
    cur = x.reshape(t, d)
    g_in, b_in = ln0_g, ln0_b
    for l in range(depth):
        lam_init = 0.8 - 0.6 * math.exp(-0.3 * l)
        w = w_in[l]
        wa = jnp.concatenate([col(w, 0), col(w, 1), col(w, 2), col(w, 6), col(w, 7), col(w, 8)], axis=1).astype(BF16)
        wb = jnp.concatenate([col(w, 3), col(w, 9), col(w, 10), col(w, 4), col(w, 5),
                              jnp.zeros((d, LANES - 2 * GLA_RANK), F32)], axis=1).astype(BF16)
        q_lo = 2 * gk + gv
        pa, pb, h, kgt = _ln_proj(cur, g_in, b_in, wa, wb, tabs, seq, 512, 640,
                                  (q_lo, q_lo + dq), (q_lo + dq, q_lo + 2 * dq),
                                  DIFF_DH ** -0.5 * math.log2(math.e), (gk, 2 * gk))

        lrt = pb[:, 3 * d:3 * d + 2 * GLA_RANK].reshape(batch, seq, 2 * GLA_RANK).transpose(0, 2, 1)
        zr = jnp.zeros((GLA_RANK, gk), F32)
        pad = jnp.zeros((LANES - 2 * GLA_RANK, gk), F32)
        wf_f = jnp.concatenate([gla_wf_fwd[l], zr, pad], axis=0)
        wf_b = jnp.concatenate([zr, gla_wf_bwd[l], pad], axis=0)
        o_f, o_b = _gla(
            pa, kgt, pb, lrt,
            (wf_f, wf_f[:2 * GLA_RANK].T, gla_bf_fwd[l].reshape(1, gk), gla_bf_fwd[l].reshape(gk, 1)),
            (wf_b, wf_b[:2 * GLA_RANK].T, gla_bf_bwd[l].reshape(1, gk), gla_bf_bwd[l].reshape(gk, 1)),
            batch, seq)

        o_diff = _diff_attn(pa, diff_lq1[l], diff_lk1[l], diff_lq2[l], diff_lk2[l], diff_norm_g[l],
                            batch, seq, lam_init)

        rw = jnp.concatenate([router_w[l], jnp.zeros((d, LANES - N_EXPERTS), F32)], axis=1)
        rb = jnp.concatenate([router_b[l], jnp.full((LANES - N_EXPERTS,), -jnp.inf, F32)]).reshape(1, LANES)
        h1, route, routet, counts = _merge(
            o_f, o_b, pb, o_diff, h, jnp.tile(gla_norm_g[l], GLA_HEADS).reshape(1, gv),
            w_br_gla[l].astype(BF16), w_br_diff[l].astype(BF16), w_out[l].astype(BF16),
            ln1_g[l].reshape(1, d), ln1_b[l].reshape(1, d), rw, rb, dn_alpha)

        seg_tables, tail_tables, blk_tables, lo_f = _moe_tables(
            counts[:, 0, :N_EXPERTS].astype(jnp.int32), tm, tb, n_rows)
        xb = _dispatch(seg_tables + tail_tables, h1, routet, lo_f[:, :, None], n_rows)
        yb = _experts(*blk_tables, xb, exp_w_gate[l], exp_b_gate[l], exp_w_up[l], exp_b_up[l],
                      exp_w_down[l], exp_b_down[l], tb)
        cur = _combine(seg_tables, yb, h1, route, lo_f[:, None, :], ln2_g[l], ln2_b[l], dn_alpha)
    return cur.reshape(batch, seq, d)
```

```python
import functools
import math

import jax
import jax.numpy as jnp
from jax import lax
from jax.experimental import pallas as pl
from jax.experimental.pallas import tpu as pltpu

F32 = jnp.float32
BF16 = jnp.bfloat16
HIGHEST = lax.Precision.HIGHEST

GLA_HEADS = 4
GLA_DK = 128
GLA_DV = 256
GLA_RANK = 16
GLA_TAU = 16.0
GLA_CHUNK = 64
DIFF_HEADS = 8
DIFF_DH = 64
ROT_DIM = DIFF_DH // 4
ROPE_THETA = 500000.0
N_EXPERTS = 32
TOP_K = 4
SWIGLU_LIMIT = 7.0
SWIGLU_ALPHA = 1.702
EPS = 1e-5

LANES = 128
SUBLANES = 8
VMEM_LIMIT = 48 * 1024 * 1024

ROUTE_E = 0
ROUTE_G = TOP_K
ROUTE_R = 2 * TOP_K
MOE_SEG = SUBLANES
MOE_BIG = 4 * MOE_SEG


def _cparams(sem):
    return pltpu.CompilerParams(dimension_semantics=sem, vmem_limit_bytes=VMEM_LIMIT)


def _layer_norm(x, g, b):
    mu = jnp.mean(x, axis=-1, keepdims=True)
    xc = x - mu
    var = jnp.mean(xc * xc, axis=-1, keepdims=True)
    return xc * lax.rsqrt(var + EPS) * g + b


def _sigmoid(x):
    return 1.0 / (1.0 + jnp.exp(-x))


def _log_sigmoid(x):
    return jnp.minimum(x, 0.0) - jnp.log(1.0 + jnp.exp(-jnp.abs(x)))


def _lane_tile(x, n):
    return jnp.concatenate([x] * n, axis=1)


def _split3(x):
    hi = x.astype(BF16)
    r1 = x - hi.astype(F32)
    mid = r1.astype(BF16)
    lo = (r1 - mid.astype(F32)).astype(BF16)
    return hi, mid, lo


def _rope(x, c, a, b):
    return (x * c + pltpu.roll(x, LANES - ROT_DIM // 2, 1) * a + pltpu.roll(x, ROT_DIM // 2, 1) * b)


def _ln_proj_kernel(x_ref, g_ref, b_ref, wa_ref, wb_ref, c_ref, a_ref, r_ref, pa_ref, pb_ref, h_ref, *kt_refs,
                    tna, tnb, q_cols, k_cols, q_scale, t_cols):
    h = _layer_norm(x_ref[...], g_ref[...], b_ref[...])
    h_ref[...] = h
    h16 = h.astype(BF16)
    for j in range(wa_ref.shape[1] // tna):
        res = jnp.dot(h16, wa_ref[:, j * tna:(j + 1) * tna], preferred_element_type=F32)
        for c in range(0, tna, LANES):
            col = j * tna + c
            blk = res[:, c:c + LANES]
            if q_cols[0] <= col < q_cols[1]:
                blk = _rope(blk, c_ref[...], a_ref[...], r_ref[...]) * q_scale
            elif k_cols[0] <= col < k_cols[1]:
                blk = _rope(blk, c_ref[...], a_ref[...], r_ref[...])
            pa_ref[:, col:col + LANES] = blk.astype(pa_ref.dtype)
            for (lo, hi), kt_ref in zip(t_cols, kt_refs):
                if lo <= col < hi:
                    kt_ref[col - lo:col - lo + LANES, :] = blk.T.astype(kt_ref.dtype)
    for j in range(wb_ref.shape[1] // tnb):
        cols = slice(j * tnb, (j + 1) * tnb)
        pb_ref[:, cols] = jnp.dot(h16, wb_ref[:, cols], preferred_element_type=F32)


def _ln_proj(x, g, b, wa16, wb16, tabs, seq, tna, tnb, q_cols, k_cols, q_scale, t_cols):
    t, d = x.shape
    na, nb = wa16.shape[1], wb16.shape[1]
    tm = min(256, t)
    nrow = seq // tm
    row = lambda n: pl.BlockSpec((tm, n), lambda i: (i, 0))
    full = lambda r, n: pl.BlockSpec((r, n), lambda i: (0, 0))
    tab = pl.BlockSpec((tm, LANES), lambda i: (i % nrow, 0))
    widths = [hi - lo for lo, hi in t_cols]
    return pl.pallas_call(
        functools.partial(_ln_proj_kernel, tna=tna, tnb=tnb, q_cols=q_cols, k_cols=k_cols, q_scale=q_scale,
                          t_cols=t_cols),
        out_shape=[jax.ShapeDtypeStruct((t, na), BF16), jax.ShapeDtypeStruct((t, nb), F32),
                   jax.ShapeDtypeStruct((t, d), F32)]
        + [jax.ShapeDtypeStruct((t // seq, w, seq), BF16) for w in widths],
        grid=(t // tm,),
        in_specs=[row(d), full(1, d), full(1, d), full(d, na), full(d, nb), tab, tab, tab],
        out_specs=[row(na), row(nb), row(d)]
        + [pl.BlockSpec((None, w, tm), lambda i: (i // nrow, 0, i % nrow)) for w in widths],
        compiler_params=_cparams(("parallel",)),
        name="ln_proj",
    )(x, g.reshape(1, d), b.reshape(1, d), wa16, wb16, *tabs)


def _gla_kernel(*refs, n_chunks, batch):
    ins, (of_ref, ob_ref, sf_ref, sb_ref) = refs[:-4], refs[-4:]
    n_in = len(ins) // 2

    @pl.when(pl.program_id(1) == 0)
    def _():
        sf_ref[...] = jnp.zeros_like(sf_ref)
        sb_ref[...] = jnp.zeros_like(sb_ref)

    chains = []
    for bb in range(batch):
        chains.append(_gla_chain(*ins[:n_in], of_ref, sf_ref, bb=bb, rev=False, n_chunks=n_chunks))
        chains.append(_gla_chain(*ins[n_in:], ob_ref, sb_ref, bb=bb, rev=True, n_chunks=n_chunks))
    for _ in zip(*chains):
        pass


def _gla_chain(q_ref, kt_ref, v_ref, lr_ref, lrt_ref, wf_ref, wft_ref, bf_ref, bft_ref,
               o_ref, s_ref, *, bb, rev, n_chunks):
    c_len = GLA_CHUNK

    z = jnp.dot(lr_ref[bb].astype(BF16), wf_ref[...].astype(BF16), preferred_element_type=F32) + bf_ref[...]
    logf = _log_sigmoid(z) * (1.0 / GLA_TAU)
    zt = jnp.dot(wft_ref[...].astype(BF16), lrt_ref[bb].astype(BF16), preferred_element_type=F32) + bft_ref[...]
    logft = _log_sigmoid(zt) * (1.0 / GLA_TAU)
    yield

    r = n_chunks * c_len
    shift = c_len.bit_length() - 1
    ri = lax.broadcasted_iota(jnp.int32, (r, r), 0)
    ci = lax.broadcasted_iota(jnp.int32, (r, r), 1)
    same = jnp.right_shift(ri, shift) == jnp.right_shift(ci, shift)
    lower = same & (ri >= ci)
    upper = same & (ri <= ci)
    keep = upper if rev else lower
    tri = keep.astype(BF16)
    tri_t = (lower if rev else upper).astype(BF16)
    blk = same.astype(BF16)
    q_scale = GLA_DK ** -0.5

    b = sum(jnp.dot(tri, part, preferred_element_type=F32) for part in _split3(logf))
    bts = sum(jnp.dot(part, jnp.concatenate([tri_t, blk], axis=1), preferred_element_type=F32)
              for part in _split3(logft))
    bt, bt_tot = bts[:, :r], bts[:, r:]
    yield
    q_in = (q_ref[bb].astype(F32) * q_scale * jnp.exp(b)).astype(BF16)
    kt = kt_ref[bb].astype(F32)
    k_in_t = (kt * jnp.exp(-bt)).astype(BF16)
    k_st_t = (kt * jnp.exp(bt_tot - bt)).astype(BF16)
    decay = jnp.exp(bt_tot)
    v = v_ref[bb]
    yield
    att = jnp.dot(q_in, k_in_t, preferred_element_type=F32)
    att = jnp.where(keep, att, 0.0).astype(BF16)
    yield
    o_intra = jnp.dot(att, v, preferred_element_type=F32)
    yield

    order = range(n_chunks - 1, -1, -1) if rev else range(n_chunks)
    for c in order:
        rows = slice(c * c_len, (c + 1) * c_len)
        state = s_ref[bb]
        o_ref[bb, rows, :] = o_intra[rows, :] + jnp.dot(q_in[rows, :], state.astype(BF16),
                                                         preferred_element_type=F32)
        s_ref[bb] = (decay[:, c * c_len:c * c_len + 1] * state
                     + jnp.dot(k_st_t[:, rows], v[rows, :], preferred_element_type=F32))
        yield


def _gla(pa, kgt, pb, lrt, gates_f, gates_b, batch, seq):
    t = pa.shape[0]
    n_chunks = 4 if seq % (4 * GLA_CHUNK) == 0 else 1
    r = n_chunks * GLA_CHUNK
    ng = seq // r
    dk, dv, h = GLA_DK, GLA_DV, GLA_HEADS
    v_off = (2 * h * dk) // dv
    lr_blk = (3 * h * dv) // LANES

    def specs(gi):
        return [
            pl.BlockSpec((batch, r, dk), lambda hh, g: (0, gi(g), hh)),
            pl.BlockSpec((batch, dk, r), lambda hh, g: (0, hh, gi(g))),
            pl.BlockSpec((batch, r, dv), lambda hh, g: (0, gi(g), v_off + hh)),
            pl.BlockSpec((batch, r, LANES), lambda hh, g: (0, gi(g), lr_blk)),
            pl.BlockSpec((batch, 2 * GLA_RANK, r), lambda hh, g: (0, 0, gi(g))),
            pl.BlockSpec((LANES, dk), lambda hh, g: (0, hh)),
            pl.BlockSpec((dk, 2 * GLA_RANK), lambda hh, g: (hh, 0)),
            pl.BlockSpec((1, dk), lambda hh, g: (0, hh)),
            pl.BlockSpec((dk, 1), lambda hh, g: (hh, 0)),
        ]

    fwd = lambda g: g
    bwd = lambda g: ng - 1 - g
    out = lambda gi: pl.BlockSpec((batch, r, dv), lambda hh, g: (0, gi(g), hh))
    pa3 = pa.reshape(batch, seq, pa.shape[1])
    pb3 = pb.reshape(batch, seq, pb.shape[1])
    o_f, o_b = pl.pallas_call(
        functools.partial(_gla_kernel, n_chunks=n_chunks, batch=batch),
        out_shape=[jax.ShapeDtypeStruct((batch, seq, h * dv), F32)] * 2,
        grid=(h, ng),
        in_specs=specs(fwd) + specs(bwd),
        out_specs=[out(fwd), out(bwd)],
        scratch_shapes=[pltpu.VMEM((batch, dk, dv), F32)] * 2,
        compiler_params=_cparams(("parallel", "arbitrary")),
        name="gla",
    )(pa3, kgt, pa3, pb3, lrt, *gates_f, pa3, kgt, pa3, pb3, lrt, *gates_b)
    return o_f.reshape(t, h * dv), o_b.reshape(t, h * dv)


def _diff_attn_kernel(q_s, kt_s, v_ref, lq1_ref, lk1_ref, lq2_ref, lk2_ref, g_ref, o_ref,
                      ve_s, m_s, acc_s, sa_s, sb_s, ma_s, mb_s, aa_s, ab_s, *, seq, tk, lam_init):
    dh = DIFF_DH
    hw = 2 * dh

    @pl.when(pl.program_id(2) == 0)
    def _():
        def body(i, carry):
            rows = pl.ds(pl.multiple_of(i * tk, tk), tk)
            ve_s[rows, :] = jnp.concatenate([v_ref[rows, :], jnp.ones((tk, hw), BF16)], axis=1)
            return carry
        lax.fori_loop(0, seq // tk, body, 0)

    m_s[...] = jnp.full_like(m_s, -jnp.inf)
    acc_s[...] = jnp.zeros_like(acc_s)

    def scores(i, buf):
        s_buf, mx_buf, al_buf = buf
        cols = pl.ds(pl.multiple_of(i * tk, tk), tk)
        kt = kt_s[:, cols]
        for p in range(2):
            s = jnp.dot(q_s[:, p * dh:(p + 1) * dh], kt[p * dh:(p + 1) * dh, :], preferred_element_type=F32)
            s_buf[p] = s
            m_old = m_s[p]
            m_new = jnp.maximum(m_old, jnp.max(s, axis=-1, keepdims=True))
            al_buf[p] = jnp.exp2(m_old - m_new)
            mx_buf[p] = m_new
            m_s[p] = m_new

    def softmax_pv(i, buf):
        s_buf, mx_buf, al_buf = buf
        ve = ve_s[pl.ds(pl.multiple_of(i * tk, tk), tk), :]
        for p in range(2):
            pr = jnp.exp2((s_buf[p] - _lane_tile(mx_buf[p], tk // LANES)).astype(BF16))
            acc_s[p] = (_lane_tile(al_buf[p], 2) * acc_s[p]
                        + jnp.dot(pr, ve, preferred_element_type=F32))

    n_t = seq // tk
    buf_a, buf_b = (sa_s, ma_s, aa_s), (sb_s, mb_s, ab_s)
    scores(0, buf_a)

    def pair_body(j, carry):
        scores(2 * j + 1, buf_b)
        softmax_pv(2 * j, buf_a)
        scores(2 * j + 2, buf_a)
        softmax_pv(2 * j + 1, buf_b)
        return carry
    lax.fori_loop(0, (n_t - 1) // 2, pair_body, 0)
    if n_t % 2 == 0:
        scores(n_t - 1, buf_b)
        softmax_pv(n_t - 2, buf_a)
        softmax_pv(n_t - 1, buf_b)
    else:
        softmax_pv(n_t - 1, buf_a)

    lam = (jnp.exp(jnp.sum(lq1_ref[...] * lk1_ref[...], axis=-1, keepdims=True))
           - jnp.exp(jnp.sum(lq2_ref[...] * lk2_ref[...], axis=-1, keepdims=True)) + lam_init)
    o = acc_s[0, :, :hw] / acc_s[0, :, hw:] - lam * (acc_s[1, :, :hw] / acc_s[1, :, hw:])
    ms = jnp.mean(o * o, axis=-1, keepdims=True)
    o_ref[...] = (o * lax.rsqrt(ms + EPS) * g_ref[...] * (1.0 - lam_init)).astype(o_ref.dtype)


def _diff_attn(pa, kdt, lq1, lk1, lq2, lk2, norm_g, batch, seq, lam_init):
    t = pa.shape[0]
    h, dh = DIFF_HEADS, DIFF_DH
    hw = 2 * dh
    tq = min(1024, seq)
    tk = min(512, seq)
    nq = seq // tq
    q_off = (2 * GLA_HEADS * GLA_DK + GLA_HEADS * GLA_DV) // hw
    v_off = q_off + 2 * h
    vec = lambda n: pl.BlockSpec((1, n), lambda b, hh, qi: (0, 0))
    return pl.pallas_call(
        functools.partial(_diff_attn_kernel, seq=seq, tk=tk, lam_init=lam_init),
        out_shape=jax.ShapeDtypeStruct((t, h * hw), BF16),
        grid=(batch, h, nq),
        in_specs=[
            pl.BlockSpec((tq, hw), lambda b, hh, qi: (b * nq + qi, q_off + hh)),
            pl.BlockSpec((None, hw, seq), lambda b, hh, qi: (b, hh, 0)),
            pl.BlockSpec((seq, hw), lambda b, hh, qi: (b, v_off + hh)),
            vec(dh), vec(dh), vec(dh), vec(dh), vec(hw),
        ],
        out_specs=pl.BlockSpec((tq, hw), lambda b, hh, qi: (b * nq + qi, hh)),
        scratch_shapes=[
            pltpu.VMEM((seq, 2 * hw), BF16),
            pltpu.VMEM((2, tq, LANES), F32),
            pltpu.VMEM((2, tq, 2 * hw), F32),
            pltpu.VMEM((2, tq, tk), F32),
            pltpu.VMEM((2, tq, tk), F32),
        ] + [pltpu.VMEM((2, tq, LANES), F32)] * 4,
        compiler_params=_cparams(("parallel", "parallel", "arbitrary")),
        name="diff_attn",
    )(pa, kdt, pa, lq1.reshape(1, dh), lk1.reshape(1, dh), lq2.reshape(1, dh),
      lk2.reshape(1, dh), norm_g.reshape(1, hw))


def _rope_tables(seq):
    half = ROT_DIM // 2
    pos = jnp.arange(seq, dtype=F32)
    inv_freq = jnp.power(ROPE_THETA, -jnp.arange(0, ROT_DIM, 2, dtype=F32) / ROT_DIM)
    ang = pos[:, None] * inv_freq[None, :]
    cos, sin = jnp.cos(ang), jnp.sin(ang)
    ones = jnp.ones((seq, DIFF_DH - ROT_DIM), F32)
    zeros = jnp.zeros((seq, DIFF_DH - ROT_DIM), F32)
    zh = jnp.zeros((seq, half), F32)
    c = jnp.concatenate([cos, cos, ones], axis=1)
    a = jnp.concatenate([-sin, zh, zeros], axis=1)
    b = jnp.concatenate([zh, sin, zeros], axis=1)
    return tuple(jnp.tile(m, (1, 2)) for m in (c, a, b))


def _merge_kernel(of_ref, ob_ref, pb_rg, od_ref, pb_gg, pb_gd, h_ref, gn_ref, wa_ref, wb_ref, wo_ref,
                  g1_ref, b1_ref, rwh_ref, rwl_ref, rb_ref, h1_ref, route_ref, routet_ref, cnt_ref,
                  *, dn_alpha):
    tm = of_ref.shape[0]
    og = of_ref[...] + ob_ref[...]
    parts = []
    for hh in range(GLA_HEADS):
        xh = og[:, hh * GLA_DV:(hh + 1) * GLA_DV]
        ms = jnp.mean(xh * xh, axis=-1, keepdims=True)
        parts.append(xh * lax.rsqrt(ms + EPS))
    rg = pb_rg[...]
    o_gla = jnp.concatenate(parts, axis=1) * gn_ref[...] * (rg * _sigmoid(rg))
    a = jnp.dot(o_gla.astype(BF16), wa_ref[...], preferred_element_type=F32)
    bb = jnp.dot(od_ref[...], wb_ref[...], preferred_element_type=F32)
    mixed = _sigmoid(pb_gg[...]) * a + _sigmoid(pb_gd[...]) * bb
    mix = jnp.dot(mixed.astype(BF16), wo_ref[...], preferred_element_type=F32)
    h1 = _layer_norm(dn_alpha * h_ref[...] + mix, g1_ref[...], b1_ref[...])
    h1_ref[...] = h1

    hh = h1.astype(BF16)
    hl = (h1 - hh.astype(F32)).astype(BF16)
    logits = (jnp.dot(hh, rwh_ref[...], preferred_element_type=F32)
              + jnp.dot(hh, rwl_ref[...], preferred_element_type=F32)
              + jnp.dot(hl, rwh_ref[...], preferred_element_type=F32)) + rb_ref[...]
    lane = lax.broadcasted_iota(jnp.int32, (tm, LANES), 1).astype(F32)
    work = logits
    top_v, top_i, sels = [], [], []
    for _ in range(TOP_K):
        m = jnp.max(work, axis=-1, keepdims=True)
        idx = jnp.min(jnp.where(work == m, lane, float(LANES)), axis=-1, keepdims=True)
        sel = lane == idx
        top_v.append(m)
        top_i.append(idx)
        sels.append(sel)
        work = jnp.where(sel, -jnp.inf, work)
    exps = [jnp.exp(v - top_v[0]) for v in top_v]
    denom = exps[0]
    for e in exps[1:]:
        denom = denom + e
    onehot = sels[0].astype(F32)
    for s in sels[1:]:
        onehot = onehot + s.astype(F32)
    ri = lax.broadcasted_iota(jnp.int32, (tm, tm), 0)
    ci = lax.broadcasted_iota(jnp.int32, (tm, tm), 1)
    strict = (ci < ri).astype(BF16)
    before = jnp.dot(strict, onehot.astype(BF16), preferred_element_type=F32)
    route = jnp.zeros((tm, LANES), F32)
    for k in range(TOP_K):
        rank = jnp.sum(jnp.where(sels[k], before, 0.0), axis=-1, keepdims=True)
        route = jnp.where(lane == float(ROUTE_E + k), top_i[k], route)
        route = jnp.where(lane == float(ROUTE_G + k), exps[k] / denom, route)
        route = jnp.where(lane == float(ROUTE_R + k), rank, route)
    route_ref[...] = route
    routet_ref[...] = route.T
    cnt_ref[...] = jnp.sum(onehot, axis=0, keepdims=True)


def _merge(o_f, o_b, pb, o_diff, h, gn, wa, wb, wo, g1, b1, rw, rb, dn_alpha):
    t, d = h.shape
    tm = _moe_tile(t)
    row = lambda c: pl.BlockSpec((tm, d), lambda i: (i, c))
    full = lambda shp: pl.BlockSpec(shp, lambda i: (0, 0))
    rwh = rw.astype(BF16)
    rwl = (rw - rwh.astype(F32)).astype(BF16)
    return pl.pallas_call(
        functools.partial(_merge_kernel, dn_alpha=dn_alpha),
        out_shape=[jax.ShapeDtypeStruct((t, d), F32),
                   jax.ShapeDtypeStruct((t, LANES), F32),
                   jax.ShapeDtypeStruct((LANES, t), F32),
                   jax.ShapeDtypeStruct((t // tm, 1, LANES), F32)],
        grid=(t // tm,),
        in_specs=[row(0), row(0), row(0), row(0), row(1), row(2), row(0),
                  full((1, d)), full((d, d)), full((d, d)), full((d, d)),
                  full((1, d)), full((1, d)), full((d, LANES)), full((d, LANES)), full((1, LANES))],
        out_specs=[pl.BlockSpec((tm, d), lambda i: (i, 0)),
                   pl.BlockSpec((tm, LANES), lambda i: (i, 0)),
                   pl.BlockSpec((LANES, tm), lambda i: (0, i)),
                   pl.BlockSpec((None, 1, LANES), lambda i: (i, 0, 0))],
        compiler_params=_cparams(("parallel",)),
        name="merge_ln1_router",
    )(o_f, o_b, pb, o_diff, pb, pb, h, gn, wa, wb, wo, g1, b1, rwh, rwl, rb)


def _moe_tile(t):
    return min(256, t)


def _moe_slots(tm):
    return tm * TOP_K + N_EXPERTS * MOE_SEG


def _piece_loops(j, lo_s, gb_s, nbig_s, nsmall_s, make_big, make_small):
    def per_expert(e, carry):
        idx = j * N_EXPERTS + e
        base, g, nbig = lo_s[idx], gb_s[idx], nbig_s[idx]

        def big_piece(p, c):
            off = p * MOE_BIG
            make_big(pl.multiple_of(base + off, MOE_SEG), pl.multiple_of(g + off, MOE_SEG)).start()
            return c
        lax.fori_loop(0, nbig, big_piece, 0)

        def small_piece(p, c):
            off = nbig * MOE_BIG + p * MOE_SEG
            make_small(pl.multiple_of(base + off, MOE_SEG), pl.multiple_of(g + off, MOE_SEG)).start()
            return c
        lax.fori_loop(0, nsmall_s[idx], small_piece, 0)
        return carry
    lax.fori_loop(0, N_EXPERTS, per_expert, 0)


def _wait_pieces(n, make_copy):
    def body(p, c):
        make_copy(0, 0).wait()
        return c
    lax.fori_loop(0, n, body, 0)


def _dispatch_kernel(lo_s, gb_s, nbig_s, nsmall_s, tbig_s, tsmall_s, ts_s, tnp_s, h_ref, routet_ref, locol_ref,
                     xb_ref, sorted_s, zero_s, sem_big, sem_small, zsem, *, nt):
    tm = h_ref.shape[0]
    n_slots = sorted_s.shape[1]
    j = pl.program_id(0)
    par = j % 2

    def seg_copy(pp, rows, sem):
        def make(local_row, global_row):
            return pltpu.make_async_copy(sorted_s.at[pp, pl.ds(local_row, rows)],
                                         xb_ref.at[pl.ds(global_row, rows)], sem.at[pp])
        return make

    big = lambda pp: seg_copy(pp, MOE_BIG, sem_big)
    small = lambda pp: seg_copy(pp, MOE_SEG, sem_small)

    blk_rows = zero_s.shape[0]

    def zero_tail(_, global_row):
        return pltpu.make_async_copy(zero_s.at[pl.ds(0, MOE_SEG)], xb_ref.at[pl.ds(global_row, MOE_SEG)],
                                     zsem.at[0])

    def zero_block(_, global_row):
        return pltpu.make_async_copy(zero_s, xb_ref.at[pl.ds(global_row, blk_rows)], zsem.at[1])

    @pl.when(j == 0)
    def _():
        zero_s[...] = jnp.zeros_like(zero_s)

        def per_region(e, carry):
            def per_piece(p, c):
                zero_tail(0, pl.multiple_of(ts_s[e] + p * MOE_SEG, MOE_SEG)).start()
                return c
            lax.fori_loop(0, tnp_s[e], per_piece, 0)
            return carry
        lax.fori_loop(0, N_EXPERTS, per_region, 0)

        def per_block(p, c):
            zero_block(0, pl.multiple_of(ts_s[N_EXPERTS] + p * blk_rows, blk_rows)).start()
            return c
        lax.fori_loop(0, tnp_s[N_EXPERTS], per_block, 0)

    routet = routet_ref[...]
    sub = lax.broadcasted_iota(jnp.int32, (LANES, tm), 0).astype(F32)
    slot_iota = lax.broadcasted_iota(jnp.int32, (n_slots, tm), 0).astype(F32)
    perm = jnp.zeros((n_slots, tm), F32)
    for k in range(TOP_K):
        e_row = routet[ROUTE_E + k:ROUTE_E + k + 1, :]
        lo_k = jnp.sum(jnp.where(sub == e_row, locol_ref[...], 0.0), axis=0, keepdims=True)
        slot = lo_k + routet[ROUTE_R + k:ROUTE_R + k + 1, :]
        perm = jnp.where(slot_iota == slot, 1.0, perm)
    sorted_s[par] = jnp.dot(perm.astype(BF16), h_ref[...].astype(BF16), preferred_element_type=F32)

    _piece_loops(j, lo_s, gb_s, nbig_s, nsmall_s, big(par), small(par))

    @pl.when(j > 0)
    def _():
        _wait_pieces(tbig_s[j - 1], big(1 - par))
        _wait_pieces(tsmall_s[j - 1], small(1 - par))

    @pl.when(j == nt - 1)
    def _():
        _wait_pieces(tbig_s[j], big(par))
        _wait_pieces(tsmall_s[j], small(par))
        _wait_pieces(tnp_s[N_EXPERTS + 1], zero_tail)
        _wait_pieces(tnp_s[N_EXPERTS], zero_block)


def _dispatch(tables, h1, routet, lo_col, n_rows, tb):
    t, d = h1.shape
    tm = _moe_tile(t)
    nt = t // tm
    n_slots = _moe_slots(tm)
    return pl.pallas_call(
        functools.partial(_dispatch_kernel, nt=nt),
        out_shape=jax.ShapeDtypeStruct((n_rows, d), F32),
        grid_spec=pltpu.PrefetchScalarGridSpec(
            num_scalar_prefetch=len(tables),
            grid=(nt,),
            in_specs=[
                pl.BlockSpec((tm, d), lambda i, *_: (i, 0)),
                pl.BlockSpec((LANES, tm), lambda i, *_: (0, i)),
                pl.BlockSpec((None, LANES, 1), lambda i, *_: (i, 0, 0)),
            ],
            out_specs=pl.BlockSpec(memory_space=pl.ANY),
            scratch_shapes=[pltpu.VMEM((2, n_slots, d), F32), pltpu.VMEM((tb, d), F32),
                            pltpu.SemaphoreType.DMA((2,)), pltpu.SemaphoreType.DMA((2,)),
                            pltpu.SemaphoreType.DMA((2,))],
        ),
        compiler_params=_cparams(("arbitrary",)),
        name="moe_dispatch",
    )(*tables, h1, routet, lo_col)


def _experts_kernel(be_ref, nu_ref, nx_ref, x_ref, wg_hbm, bg_ref, wu_hbm, bu_ref, wd_hbm, bd_ref, y_ref,
                    stage_s, wg_s, wu_s, wd_s, sem):
    i = pl.program_id(0)

    def fetch(e):
        return [pltpu.make_async_copy(w.at[e], stage_s.at[n], sem.at[n])
                for n, w in enumerate((wg_hbm, wu_hbm, wd_hbm))]

    @pl.when(i >= nu_ref[0])
    def _():
        y_ref[...] = jnp.zeros_like(y_ref)

    @pl.when(i < nu_ref[0])
    def _():
        e = be_ref[i]
        prev = be_ref[jnp.maximum(i - 1, 0)]

        @pl.when(i == 0)
        def _():
            for c in fetch(e):
                c.start()

        @pl.when((i == 0) | (e != prev))
        def _():
            for c in fetch(e):
                c.wait()
            wg_s[...] = stage_s[0].astype(BF16)
            wu_s[...] = stage_s[1].astype(BF16)
            wd_s[...] = stage_s[2].astype(BF16)

            @pl.when(nx_ref[i] != e)
            def _():
                for c in fetch(nx_ref[i]):
                    c.start()

        x = x_ref[...].astype(BF16)
        g = jnp.minimum(jnp.dot(x, wg_s[...], preferred_element_type=F32) + bg_ref[...], SWIGLU_LIMIT)
        u = jnp.clip(jnp.dot(x, wu_s[...], preferred_element_type=F32) + bu_ref[...],
                     -SWIGLU_LIMIT, SWIGLU_LIMIT)
        a = g * _sigmoid(SWIGLU_ALPHA * g) * (u + 1.0)
        y_ref[...] = jnp.dot(a.astype(BF16), wd_s[...], preferred_element_type=F32) + bd_ref[...]


def _experts(blk_e, n_used, blk_next, xb, wg, bg, wu, bu, wd, bd, tb):
    p, d = xb.shape
    ne, _, dff = wg.shape
    assert d == dff, (d, dff)
    nb = p // tb
    rowblk = lambda i, be, nu, nx: (jnp.maximum(jnp.minimum(i, nu[0] - 1), 0), 0)
    bspec = lambda n: pl.BlockSpec((None, 1, n), lambda i, be, nu, nx: (be[i], 0, 0))
    hbm = pl.BlockSpec(memory_space=pl.ANY)
    return pl.pallas_call(
        _experts_kernel,
        out_shape=jax.ShapeDtypeStruct((p, d), F32),
        grid_spec=pltpu.PrefetchScalarGridSpec(
            num_scalar_prefetch=3,
            grid=(nb,),
            in_specs=[pl.BlockSpec((tb, d), rowblk), hbm, bspec(dff), hbm, bspec(dff), hbm, bspec(d)],
            out_specs=pl.BlockSpec((tb, d), lambda i, be, nu, nx: (i, 0)),
            scratch_shapes=[pltpu.VMEM((3, d, dff), F32), pltpu.VMEM((d, dff), BF16),
                            pltpu.VMEM((d, dff), BF16), pltpu.VMEM((dff, d), BF16),
                            pltpu.SemaphoreType.DMA((3,))],
        ),
        compiler_params=_cparams(("arbitrary",)),
        name="moe_experts",
    )(blk_e, n_used, blk_next, xb, wg, bg.reshape(ne, 1, dff), wu, bu.reshape(ne, 1, dff), wd,
      bd.reshape(ne, 1, d))


def _combine_kernel(lo_s, gb_s, nbig_s, nsmall_s, tbig_s, tsmall_s, yb_ref, h_ref, route_ref, lorow_ref,
                    g_ref, b_ref, o_ref, ybuf, sem_big, sem_small, *, dn_alpha, nt):
    tm = h_ref.shape[0]
    n_slots = ybuf.shape[1]
    j = pl.program_id(0)
    par = j % 2

    def seg_copy(pp, rows, sem):
        def make(local_row, global_row):
            return pltpu.make_async_copy(yb_ref.at[pl.ds(global_row, rows)],
                                         ybuf.at[pp, pl.ds(local_row, rows)], sem.at[pp])
        return make

    big = lambda pp: seg_copy(pp, MOE_BIG, sem_big)
    small = lambda pp: seg_copy(pp, MOE_SEG, sem_small)

    @pl.when(j == 0)
    def _():
        ybuf[...] = jnp.zeros_like(ybuf)
        _piece_loops(j, lo_s, gb_s, nbig_s, nsmall_s, big(0), small(0))

    @pl.when(j + 1 < nt)
    def _():
        _piece_loops(j + 1, lo_s, gb_s, nbig_s, nsmall_s, big(1 - par), small(1 - par))

    _wait_pieces(tbig_s[j], big(par))
    _wait_pieces(tsmall_s[j], small(par))

    route = route_ref[...]
    lane = lax.broadcasted_iota(jnp.int32, (tm, LANES), 1).astype(F32)
    slot_iota = lax.broadcasted_iota(jnp.int32, (tm, n_slots), 1).astype(F32)
    wgt = jnp.zeros((tm, n_slots), F32)
    for k in range(TOP_K):
        e_col = route[:, ROUTE_E + k:ROUTE_E + k + 1]
        lo_k = jnp.sum(jnp.where(lane == e_col, lorow_ref[...], 0.0), axis=-1, keepdims=True)
        slot = lo_k + route[:, ROUTE_R + k:ROUTE_R + k + 1]
        wgt = jnp.where(slot_iota == slot, route[:, ROUTE_G + k:ROUTE_G + k + 1], wgt)
    y = ybuf[par]
    wh = wgt.astype(BF16)
    wl = (wgt - wh.astype(F32)).astype(BF16)
    yh = y.astype(BF16)
    yl = (y - yh.astype(F32)).astype(BF16)
    ffn = (jnp.dot(wh, yh, preferred_element_type=F32) + jnp.dot(wh, yl, preferred_element_type=F32)
           + jnp.dot(wl, yh, preferred_element_type=F32))
    o_ref[...] = _layer_norm(dn_alpha * h_ref[...] + ffn, g_ref[...], b_ref[...])


def _combine(tables, yb, h1, route, lo_row, g2, b2, dn_alpha):
    t, d = h1.shape
    tm = _moe_tile(t)
    nt = t // tm
    return pl.pallas_call(
        functools.partial(_combine_kernel, dn_alpha=dn_alpha, nt=nt),
        out_shape=jax.ShapeDtypeStruct((t, d), F32),
        grid_spec=pltpu.PrefetchScalarGridSpec(
            num_scalar_prefetch=len(tables),
            grid=(nt,),
            in_specs=[
                pl.BlockSpec(memory_space=pl.ANY),
                pl.BlockSpec((tm, d), lambda i, *_: (i, 0)),
                pl.BlockSpec((tm, LANES), lambda i, *_: (i, 0)),
                pl.BlockSpec((None, 1, LANES), lambda i, *_: (i, 0, 0)),
                pl.BlockSpec((1, d), lambda i, *_: (0, 0)),
                pl.BlockSpec((1, d), lambda i, *_: (0, 0)),
            ],
            out_specs=pl.BlockSpec((tm, d), lambda i, *_: (i, 0)),
            scratch_shapes=[pltpu.VMEM((2, _moe_slots(tm), d), F32), pltpu.SemaphoreType.DMA((2,)),
                            pltpu.SemaphoreType.DMA((2,))],
        ),
        compiler_params=_cparams(("arbitrary",)),
        name="moe_combine_ln2",
    )(*tables, yb, h1, route, lo_row, g2.reshape(1, d), b2.reshape(1, d))


def _moe_tables(counts, tm, tb, n_rows):
    seg = ((counts + MOE_SEG - 1) // MOE_SEG) * MOE_SEG
    lo = jnp.cumsum(seg, axis=1) - seg
    rows_e = jnp.sum(seg, axis=0)
    region = ((rows_e + tb - 1) // tb) * tb
    pend = jnp.cumsum(region)
    pstart = pend - region
    gbase = pstart[None, :] + jnp.cumsum(seg, axis=0) - seg
    n_big = seg // MOE_BIG
    n_small = (seg - n_big * MOE_BIG) // MOE_SEG
    tail_start = jnp.concatenate([pstart + rows_e, pend[-1:]])
    tail_small = (region - rows_e) // MOE_SEG
    tail_pieces = jnp.concatenate([tail_small, (n_rows - pend[-1:]) // tb, jnp.sum(tail_small, keepdims=True)])
    i32 = lambda a: a.reshape(-1).astype(jnp.int32)
    seg_tables = (i32(lo), i32(gbase), i32(n_big), i32(n_small),
                  i32(jnp.sum(n_big, axis=1)), i32(jnp.sum(n_small, axis=1)))
    n_used = (pend[-1] // tb).astype(jnp.int32)
    blk = jnp.minimum(jnp.arange(n_rows // tb, dtype=jnp.int32), n_used - 1)
    blk_e = jnp.minimum(jnp.sum((pend[None, :] <= (blk * tb)[:, None]).astype(jnp.int32), axis=1),
                        N_EXPERTS - 1)
    ids = jnp.arange(N_EXPERTS, dtype=jnp.int32)
    later = jnp.where((region[None, :] > 0) & (ids[None, :] > ids[:, None]), ids[None, :], N_EXPERTS)
    nxt_e = jnp.min(later, axis=1)
    nxt_e = jnp.where(nxt_e < N_EXPERTS, nxt_e, ids).astype(jnp.int32)
    lo_f = jnp.pad(lo.astype(F32), ((0, 0), (0, LANES - N_EXPERTS)))
    return (seg_tables, (i32(tail_start), i32(tail_pieces)), (blk_e, n_used.reshape(1), nxt_e[blk_e]), lo_f)


def kernel(x, ln0_g, ln0_b, w_in, gla_wf_fwd, gla_bf_fwd, gla_wf_bwd, gla_bf_bwd, gla_norm_g, diff_lq1, diff_lk1, diff_lq2, diff_lk2, diff_norm_g, w_br_gla, w_br_diff, w_out, ln1_g, ln1_b, router_w, router_b, exp_w_gate, exp_b_gate, exp_w_up, exp_b_up, exp_w_down, exp_b_down, ln2_g, ln2_b):
    batch, seq, d = x.shape
    depth = w_in.shape[0]
    t = batch * seq
    dn_alpha = (2 * depth) ** 0.25
    gk, gv = GLA_HEADS * GLA_DK, GLA_HEADS * GLA_DV
    dq = DIFF_HEADS * 2 * DIFF_DH
    splits = (gk, gk, gv, gv, GLA_RANK, GLA_RANK, dq, dq, dq, d, d)
    cuts = [0]
    for s in splits:
        cuts.append(cuts[-1] + s)
    col = lambda w, i: w[:, cuts[i]:cuts[i + 1]]
    tabs = _rope_tables(seq)
    tm = _moe_tile(t)
    tb = _moe_tile(t)
    n_rows = t * TOP_K + (t // tm) * N_EXPERTS * (MOE_SEG - 1) + N_EXPERTS * (tb - 1)
    n_rows = ((n_rows + tb - 1) // tb) * tb

    assert depth == 1, depth
    cur = x.reshape(t, d)
    g_in, b_in = ln0_g, ln0_b
    for l in range(depth):
        lam_init = 0.8 - 0.6 * math.exp(-0.3 * l)
        w = w_in[l]
        wa = jnp.concatenate([col(w, 0), col(w, 1), col(w, 2), col(w, 6), col(w, 7), col(w, 8)], axis=1).astype(BF16)
        wb = jnp.concatenate([col(w, 3), col(w, 9), col(w, 10), col(w, 4), col(w, 5),
                              jnp.zeros((d, LANES - 2 * GLA_RANK), F32)], axis=1).astype(BF16)
        q_lo = 2 * gk + gv
        k_cols = (q_lo + dq, q_lo + 2 * dq)
        pa, pb, h, kgt, kdt = _ln_proj(cur, g_in, b_in, wa, wb, tabs, seq, 512, 640,
                                       (q_lo, q_lo + dq), k_cols, DIFF_DH ** -0.5 * math.log2(math.e),
                                       ((gk, 2 * gk), k_cols))

        lrt = pb[:, 3 * d:3 * d + 2 * GLA_RANK].reshape(batch, seq, 2 * GLA_RANK).transpose(0, 2, 1)
        zr = jnp.zeros((GLA_RANK, gk), F32)
        pad = jnp.zeros((LANES - 2 * GLA_RANK, gk), F32)
        wf_f = jnp.concatenate([gla_wf_fwd[l], zr, pad], axis=0)
        wf_b = jnp.concatenate([zr, gla_wf_bwd[l], pad], axis=0)
        o_f, o_b = _gla(
            pa, kgt, pb, lrt,
            (wf_f, wf_f[:2 * GLA_RANK].T, gla_bf_fwd[l].reshape(1, gk), gla_bf_fwd[l].reshape(gk, 1)),
            (wf_b, wf_b[:2 * GLA_RANK].T, gla_bf_bwd[l].reshape(1, gk), gla_bf_bwd[l].reshape(gk, 1)),
            batch, seq)

        o_diff = _diff_attn(pa, kdt, diff_lq1[l], diff_lk1[l], diff_lq2[l], diff_lk2[l], diff_norm_g[l],
                            batch, seq, lam_init)

        rw = jnp.concatenate([router_w[l], jnp.zeros((d, LANES - N_EXPERTS), F32)], axis=1)
        rb = jnp.concatenate([router_b[l], jnp.full((LANES - N_EXPERTS,), -jnp.inf, F32)]).reshape(1, LANES)
        h1, route, routet, counts = _merge(
            o_f, o_b, pb, o_diff, h, jnp.tile(gla_norm_g[l], GLA_HEADS).reshape(1, gv),
            w_br_gla[l].astype(BF16), w_br_diff[l].astype(BF16), w_out[l].astype(BF16),
            ln1_g[l].reshape(1, d), ln1_b[l].reshape(1, d), rw, rb, dn_alpha)

        seg_tables, tail_tables, blk_tables, lo_f = _moe_tables(
            counts[:, 0, :N_EXPERTS].astype(jnp.int32), tm, tb, n_rows)
        xb = _dispatch(seg_tables + tail_tables, h1, routet, lo_f[:, :, None], n_rows, tb)
        yb = _experts(*blk_tables, xb, exp_w_gate[l], exp_b_gate[l], exp_w_up[l], exp_b_up[l],
                      exp_w_down[l], exp_b_down[l], tb)
        cur = _combine(seg_tables, yb, h1, route, lo_f[:, None, :], ln2_g[l], ln2_b[l], dn_alpha)
    return cur.reshape(batch, seq, d)
```

```python
import functools
import math

import jax
import jax.numpy as jnp
from jax import lax
from jax.experimental import pallas as pl
from jax.experimental.pallas import tpu as pltpu

F32 = jnp.float32
BF16 = jnp.bfloat16

GLA_HEADS = 4
GLA_DK = 128
GLA_DV = 256
GLA_RANK = 16
GLA_TAU = 16.0
GLA_CHUNK = 64
DIFF_HEADS = 8
DIFF_DH = 64
ROT_DIM = DIFF_DH // 4
ROPE_THETA = 500000.0
N_EXPERTS = 32
TOP_K = 4
SWIGLU_LIMIT = 7.0
SWIGLU_ALPHA = 1.702
EPS = 1e-5

LANES = 128
SUBLANES = 8
VMEM_LIMIT = 48 * 1024 * 1024

ROW_TILE = 256
PROJ_COLS_BF16 = 512
PROJ_COLS_F32 = 640
GLA_CHUNKS_PER_STEP = 4
ATTN_Q_TILE = 1024
ATTN_K_TILE = 512

ROUTE_E = 0
ROUTE_G = TOP_K
ROUTE_R = 2 * TOP_K
MOE_SEG = SUBLANES
MOE_BIG = 4 * MOE_SEG


def _cparams(sem):
    return pltpu.CompilerParams(dimension_semantics=sem, vmem_limit_bytes=VMEM_LIMIT)


def _layer_norm(x, g, b):
    mu = jnp.mean(x, axis=-1, keepdims=True)
    xc = x - mu
    var = jnp.mean(xc * xc, axis=-1, keepdims=True)
    return xc * lax.rsqrt(var + EPS) * g + b


def _sigmoid(x):
    return 1.0 / (1.0 + jnp.exp(-x))


def _log_sigmoid(x):
    return jnp.minimum(x, 0.0) - jnp.log(1.0 + jnp.exp(-jnp.abs(x)))


def _lane_tile(x, n):
    return jnp.concatenate([x] * n, axis=1)


def _split3(x):
    hi = x.astype(BF16)
    r1 = x - hi.astype(F32)
    mid = r1.astype(BF16)
    lo = (r1 - mid.astype(F32)).astype(BF16)
    return hi, mid, lo


def _rope(x, c, a, b):
    return (x * c + pltpu.roll(x, LANES - ROT_DIM // 2, 1) * a + pltpu.roll(x, ROT_DIM // 2, 1) * b)


def _ln_proj_kernel(x_ref, g_ref, b_ref, wa_ref, wb_ref, c_ref, a_ref, r_ref, pa_ref, pb_ref, h_ref, *kt_refs,
                    tna, tnb, q_cols, k_cols, q_scale, t_cols):
    h = _layer_norm(x_ref[...], g_ref[...], b_ref[...])
    h_ref[...] = h
    h16 = h.astype(BF16)
    for j in range(wa_ref.shape[1] // tna):
        res = jnp.dot(h16, wa_ref[:, j * tna:(j + 1) * tna], preferred_element_type=F32)
        for c in range(0, tna, LANES):
            col = j * tna + c
            blk = res[:, c:c + LANES]
            if q_cols[0] <= col < q_cols[1]:
                blk = _rope(blk, c_ref[...], a_ref[...], r_ref[...]) * q_scale
            elif k_cols[0] <= col < k_cols[1]:
                blk = _rope(blk, c_ref[...], a_ref[...], r_ref[...])
            pa_ref[:, col:col + LANES] = blk.astype(pa_ref.dtype)
            for (lo, hi), kt_ref in zip(t_cols, kt_refs):
                if lo <= col < hi:
                    kt_ref[col - lo:col - lo + LANES, :] = blk.T.astype(kt_ref.dtype)
    for j in range(wb_ref.shape[1] // tnb):
        cols = slice(j * tnb, (j + 1) * tnb)
        pb_ref[:, cols] = jnp.dot(h16, wb_ref[:, cols], preferred_element_type=F32)


def _ln_proj(x, g, b, wa16, wb16, tabs, seq, tna, tnb, q_cols, k_cols, q_scale, t_cols):
    t, d = x.shape
    na, nb = wa16.shape[1], wb16.shape[1]
    tm = min(ROW_TILE, t)
    nrow = seq // tm
    row = lambda n: pl.BlockSpec((tm, n), lambda i: (i, 0))
    full = lambda r, n: pl.BlockSpec((r, n), lambda i: (0, 0))
    tab = pl.BlockSpec((tm, LANES), lambda i: (i % nrow, 0))
    widths = [hi - lo for lo, hi in t_cols]
    return pl.pallas_call(
        functools.partial(_ln_proj_kernel, tna=tna, tnb=tnb, q_cols=q_cols, k_cols=k_cols, q_scale=q_scale,
                          t_cols=t_cols),
        out_shape=[jax.ShapeDtypeStruct((t, na), BF16), jax.ShapeDtypeStruct((t, nb), F32),
                   jax.ShapeDtypeStruct((t, d), F32)]
        + [jax.ShapeDtypeStruct((t // seq, w, seq), BF16) for w in widths],
        grid=(t // tm,),
        in_specs=[row(d), full(1, d), full(1, d), full(d, na), full(d, nb), tab, tab, tab],
        out_specs=[row(na), row(nb), row(d)]
        + [pl.BlockSpec((None, w, tm), lambda i: (i // nrow, 0, i % nrow)) for w in widths],
        compiler_params=_cparams(("parallel",)),
        name="ln_proj",
    )(x, g.reshape(1, d), b.reshape(1, d), wa16, wb16, *tabs)


def _gla_kernel(*refs, n_chunks, batch):
    ins, (of_ref, ob_ref, sf_ref, sb_ref) = refs[:-4], refs[-4:]
    n_in = len(ins) // 2

    @pl.when(pl.program_id(1) == 0)
    def _():
        sf_ref[...] = jnp.zeros_like(sf_ref)
        sb_ref[...] = jnp.zeros_like(sb_ref)

    chains = []
    for bb in range(batch):
        chains.append(_gla_chain(*ins[:n_in], of_ref, sf_ref, bb=bb, rev=False, n_chunks=n_chunks))
        chains.append(_gla_chain(*ins[n_in:], ob_ref, sb_ref, bb=bb, rev=True, n_chunks=n_chunks))
    for _ in zip(*chains):
        pass


def _gla_chain(q_ref, kt_ref, v_ref, lr_ref, lrt_ref, wf_ref, wft_ref, bf_ref, bft_ref,
               o_ref, s_ref, *, bb, rev, n_chunks):
    c_len = GLA_CHUNK

    z = jnp.dot(lr_ref[bb].astype(BF16), wf_ref[...].astype(BF16), preferred_element_type=F32) + bf_ref[...]
    logf = _log_sigmoid(z) * (1.0 / GLA_TAU)
    zt = jnp.dot(wft_ref[...].astype(BF16), lrt_ref[bb].astype(BF16), preferred_element_type=F32) + bft_ref[...]
    logft = _log_sigmoid(zt) * (1.0 / GLA_TAU)
    yield

    r = n_chunks * c_len
    shift = c_len.bit_length() - 1
    ri = lax.broadcasted_iota(jnp.int32, (r, r), 0)
    ci = lax.broadcasted_iota(jnp.int32, (r, r), 1)
    same = jnp.right_shift(ri, shift) == jnp.right_shift(ci, shift)
    lower = same & (ri >= ci)
    upper = same & (ri <= ci)
    keep = upper if rev else lower
    tri = keep.astype(BF16)
    tri_t = (lower if rev else upper).astype(BF16)
    blk = same.astype(BF16)
    q_scale = GLA_DK ** -0.5

    b = sum(jnp.dot(tri, part, preferred_element_type=F32) for part in _split3(logf))
    bts = sum(jnp.dot(part, jnp.concatenate([tri_t, blk], axis=1), preferred_element_type=F32)
              for part in _split3(logft))
    bt, bt_tot = bts[:, :r], bts[:, r:]
    yield
    q_in = (q_ref[bb].astype(F32) * q_scale * jnp.exp(b)).astype(BF16)
    kt = kt_ref[bb].astype(F32)
    k_in_t = (kt * jnp.exp(-bt)).astype(BF16)
    k_st_t = (kt * jnp.exp(bt_tot - bt)).astype(BF16)
    decay = jnp.exp(bt_tot)
    v = v_ref[bb]
    yield
    att = jnp.dot(q_in, k_in_t, preferred_element_type=F32)
    att = jnp.where(keep, att, 0.0).astype(BF16)
    yield
    o_intra = jnp.dot(att, v, preferred_element_type=F32)
    yield

    order = range(n_chunks - 1, -1, -1) if rev else range(n_chunks)
    for c in order:
        rows = slice(c * c_len, (c + 1) * c_len)
        state = s_ref[bb]
        o_ref[bb, rows, :] = o_intra[rows, :] + jnp.dot(q_in[rows, :], state.astype(BF16),
                                                         preferred_element_type=F32)
        s_ref[bb] = (decay[:, c * c_len:c * c_len + 1] * state
                     + jnp.dot(k_st_t[:, rows], v[rows, :], preferred_element_type=F32))
        yield


def _gla(pa, kgt, pb, lrt, gates_f, gates_b, batch, seq):
    t = pa.shape[0]
    n_chunks = GLA_CHUNKS_PER_STEP if seq % (GLA_CHUNKS_PER_STEP * GLA_CHUNK) == 0 else 1
    r = n_chunks * GLA_CHUNK
    ng = seq // r
    dk, dv, h = GLA_DK, GLA_DV, GLA_HEADS
    v_off = (2 * h * dk) // dv
    lr_blk = (3 * h * dv) // LANES

    def specs(gi):
        return [
            pl.BlockSpec((batch, r, dk), lambda hh, g: (0, gi(g), hh)),
            pl.BlockSpec((batch, dk, r), lambda hh, g: (0, hh, gi(g))),
            pl.BlockSpec((batch, r, dv), lambda hh, g: (0, gi(g), v_off + hh)),
            pl.BlockSpec((batch, r, LANES), lambda hh, g: (0, gi(g), lr_blk)),
            pl.BlockSpec((batch, 2 * GLA_RANK, r), lambda hh, g: (0, 0, gi(g))),
            pl.BlockSpec((LANES, dk), lambda hh, g: (0, hh)),
            pl.BlockSpec((dk, 2 * GLA_RANK), lambda hh, g: (hh, 0)),
            pl.BlockSpec((1, dk), lambda hh, g: (0, hh)),
            pl.BlockSpec((dk, 1), lambda hh, g: (hh, 0)),
        ]

    fwd = lambda g: g
    bwd = lambda g: ng - 1 - g
    out = lambda gi: pl.BlockSpec((batch, r, dv), lambda hh, g: (0, gi(g), hh))
    pa3 = pa.reshape(batch, seq, pa.shape[1])
    pb3 = pb.reshape(batch, seq, pb.shape[1])
    o_f, o_b = pl.pallas_call(
        functools.partial(_gla_kernel, n_chunks=n_chunks, batch=batch),
        out_shape=[jax.ShapeDtypeStruct((batch, seq, h * dv), F32)] * 2,
        grid=(h, ng),
        in_specs=specs(fwd) + specs(bwd),
        out_specs=[out(fwd), out(bwd)],
        scratch_shapes=[pltpu.VMEM((batch, dk, dv), F32)] * 2,
        compiler_params=_cparams(("parallel", "arbitrary")),
        name="gla",
    )(pa3, kgt, pa3, pb3, lrt, *gates_f, pa3, kgt, pa3, pb3, lrt, *gates_b)
    return o_f.reshape(t, h * dv), o_b.reshape(t, h * dv)


def _diff_attn_kernel(q_s, kt_s, v_ref, lq1_ref, lk1_ref, lq2_ref, lk2_ref, g_ref, o_ref,
                      ve_s, m_s, acc_s, sa_s, sb_s, ma_s, mb_s, aa_s, ab_s, *, seq, tk, lam_init):
    dh = DIFF_DH
    hw = 2 * dh

    @pl.when(pl.program_id(2) == 0)
    def _():
        def body(i, carry):
            rows = pl.ds(pl.multiple_of(i * tk, tk), tk)
            ve_s[rows, :] = jnp.concatenate([v_ref[rows, :], jnp.ones((tk, hw), BF16)], axis=1)
            return carry
        lax.fori_loop(0, seq // tk, body, 0)

    m_s[...] = jnp.full_like(m_s, -jnp.inf)
    acc_s[...] = jnp.zeros_like(acc_s)

    def scores(i, buf):
        s_buf, mx_buf, al_buf = buf
        cols = pl.ds(pl.multiple_of(i * tk, tk), tk)
        kt = kt_s[:, cols]
        for p in range(2):
            s = jnp.dot(q_s[:, p * dh:(p + 1) * dh], kt[p * dh:(p + 1) * dh, :], preferred_element_type=F32)
            s_buf[p] = s
            m_old = m_s[p]
            m_new = jnp.maximum(m_old, jnp.max(s, axis=-1, keepdims=True))
            al_buf[p] = jnp.exp2(m_old - m_new)
            mx_buf[p] = m_new
            m_s[p] = m_new

    def softmax_pv(i, buf):
        s_buf, mx_buf, al_buf = buf
        ve = ve_s[pl.ds(pl.multiple_of(i * tk, tk), tk), :]
        for p in range(2):
            pr = jnp.exp2((s_buf[p] - _lane_tile(mx_buf[p], tk // LANES)).astype(BF16))
            acc_s[p] = (_lane_tile(al_buf[p], 2) * acc_s[p]
                        + jnp.dot(pr, ve, preferred_element_type=F32))

    n_t = seq // tk
    buf_a, buf_b = (sa_s, ma_s, aa_s), (sb_s, mb_s, ab_s)
    scores(0, buf_a)

    def pair_body(j, carry):
        scores(2 * j + 1, buf_b)
        softmax_pv(2 * j, buf_a)
        scores(2 * j + 2, buf_a)
        softmax_pv(2 * j + 1, buf_b)
        return carry
    lax.fori_loop(0, (n_t - 1) // 2, pair_body, 0)
    if n_t % 2 == 0:
        scores(n_t - 1, buf_b)
        softmax_pv(n_t - 2, buf_a)
        softmax_pv(n_t - 1, buf_b)
    else:
        softmax_pv(n_t - 1, buf_a)

    lam = (jnp.exp(jnp.sum(lq1_ref[...] * lk1_ref[...], axis=-1, keepdims=True))
           - jnp.exp(jnp.sum(lq2_ref[...] * lk2_ref[...], axis=-1, keepdims=True)) + lam_init)
    o = acc_s[0, :, :hw] / acc_s[0, :, hw:] - lam * (acc_s[1, :, :hw] / acc_s[1, :, hw:])
    ms = jnp.mean(o * o, axis=-1, keepdims=True)
    o_ref[...] = (o * lax.rsqrt(ms + EPS) * g_ref[...] * (1.0 - lam_init)).astype(o_ref.dtype)


def _diff_attn(pa, kdt, lq1, lk1, lq2, lk2, norm_g, batch, seq, lam_init):
    t = pa.shape[0]
    h, dh = DIFF_HEADS, DIFF_DH
    hw = 2 * dh
    tq = min(ATTN_Q_TILE, seq)
    tk = min(ATTN_K_TILE, seq)
    nq = seq // tq
    q_off = (2 * GLA_HEADS * GLA_DK + GLA_HEADS * GLA_DV) // hw
    v_off = q_off + 2 * h
    vec = lambda n: pl.BlockSpec((1, n), lambda b, hh, qi: (0, 0))
    return pl.pallas_call(
        functools.partial(_diff_attn_kernel, seq=seq, tk=tk, lam_init=lam_init),
        out_shape=jax.ShapeDtypeStruct((t, h * hw), BF16),
        grid=(batch, h, nq),
        in_specs=[
            pl.BlockSpec((tq, hw), lambda b, hh, qi: (b * nq + qi, q_off + hh)),
            pl.BlockSpec((None, hw, seq), lambda b, hh, qi: (b, hh, 0)),
            pl.BlockSpec((seq, hw), lambda b, hh, qi: (b, v_off + hh)),
            vec(dh), vec(dh), vec(dh), vec(dh), vec(hw),
        ],
        out_specs=pl.BlockSpec((tq, hw), lambda b, hh, qi: (b * nq + qi, hh)),
        scratch_shapes=[
            pltpu.VMEM((seq, 2 * hw), BF16),
            pltpu.VMEM((2, tq, LANES), F32),
            pltpu.VMEM((2, tq, 2 * hw), F32),
            pltpu.VMEM((2, tq, tk), F32),
            pltpu.VMEM((2, tq, tk), F32),
        ] + [pltpu.VMEM((2, tq, LANES), F32)] * 4,
        compiler_params=_cparams(("parallel", "parallel", "arbitrary")),
        name="diff_attn",
    )(pa, kdt, pa, lq1.reshape(1, dh), lk1.reshape(1, dh), lq2.reshape(1, dh),
      lk2.reshape(1, dh), norm_g.reshape(1, hw))


def _rope_tables(seq):
    half = ROT_DIM // 2
    pos = jnp.arange(seq, dtype=F32)
    inv_freq = jnp.power(ROPE_THETA, -jnp.arange(0, ROT_DIM, 2, dtype=F32) / ROT_DIM)
    ang = pos[:, None] * inv_freq[None, :]
    cos, sin = jnp.cos(ang), jnp.sin(ang)
    ones = jnp.ones((seq, DIFF_DH - ROT_DIM), F32)
    zeros = jnp.zeros((seq, DIFF_DH - ROT_DIM), F32)
    zh = jnp.zeros((seq, half), F32)
    c = jnp.concatenate([cos, cos, ones], axis=1)
    a = jnp.concatenate([-sin, zh, zeros], axis=1)
    b = jnp.concatenate([zh, sin, zeros], axis=1)
    return tuple(jnp.tile(m, (1, 2)) for m in (c, a, b))


def _merge_kernel(of_ref, ob_ref, pb_rg, od_ref, pb_gg, pb_gd, h_ref, gn_ref, wa_ref, wb_ref, wo_ref,
                  g1_ref, b1_ref, rwh_ref, rwl_ref, rb_ref, h1_ref, route_ref, routet_ref, cnt_ref,
                  *, dn_alpha):
    tm = of_ref.shape[0]
    og = of_ref[...] + ob_ref[...]
    parts = []
    for hh in range(GLA_HEADS):
        xh = og[:, hh * GLA_DV:(hh + 1) * GLA_DV]
        ms = jnp.mean(xh * xh, axis=-1, keepdims=True)
        parts.append(xh * lax.rsqrt(ms + EPS))
    rg = pb_rg[...]
    o_gla = jnp.concatenate(parts, axis=1) * gn_ref[...] * (rg * _sigmoid(rg))
    a = jnp.dot(o_gla.astype(BF16), wa_ref[...], preferred_element_type=F32)
    bb = jnp.dot(od_ref[...], wb_ref[...], preferred_element_type=F32)
    mixed = _sigmoid(pb_gg[...]) * a + _sigmoid(pb_gd[...]) * bb
    mix = jnp.dot(mixed.astype(BF16), wo_ref[...], preferred_element_type=F32)
    h1 = _layer_norm(dn_alpha * h_ref[...] + mix, g1_ref[...], b1_ref[...])
    h1_ref[...] = h1

    hh = h1.astype(BF16)
    hl = (h1 - hh.astype(F32)).astype(BF16)
    logits = (jnp.dot(hh, rwh_ref[...], preferred_element_type=F32)
              + jnp.dot(hh, rwl_ref[...], preferred_element_type=F32)
              + jnp.dot(hl, rwh_ref[...], preferred_element_type=F32)) + rb_ref[...]
    lane = lax.broadcasted_iota(jnp.int32, (tm, LANES), 1).astype(F32)
    work = logits
    top_v, top_i, sels = [], [], []
    for _ in range(TOP_K):
        m = jnp.max(work, axis=-1, keepdims=True)
        idx = jnp.min(jnp.where(work == m, lane, float(LANES)), axis=-1, keepdims=True)
        sel = lane == idx
        top_v.append(m)
        top_i.append(idx)
        sels.append(sel)
        work = jnp.where(sel, -jnp.inf, work)
    exps = [jnp.exp(v - top_v[0]) for v in top_v]
    denom = exps[0]
    for e in exps[1:]:
        denom = denom + e
    onehot = sels[0].astype(F32)
    for s in sels[1:]:
        onehot = onehot + s.astype(F32)
    ri = lax.broadcasted_iota(jnp.int32, (tm, tm), 0)
    ci = lax.broadcasted_iota(jnp.int32, (tm, tm), 1)
    strict = (ci < ri).astype(BF16)
    before = jnp.dot(strict, onehot.astype(BF16), preferred_element_type=F32)
    route = jnp.zeros((tm, LANES), F32)
    for k in range(TOP_K):
        rank = jnp.sum(jnp.where(sels[k], before, 0.0), axis=-1, keepdims=True)
        route = jnp.where(lane == float(ROUTE_E + k), top_i[k], route)
        route = jnp.where(lane == float(ROUTE_G + k), exps[k] / denom, route)
        route = jnp.where(lane == float(ROUTE_R + k), rank, route)
    route_ref[...] = route
    routet_ref[...] = route.T
    cnt_ref[...] = jnp.sum(onehot, axis=0, keepdims=True)


def _merge(o_f, o_b, pb, o_diff, h, gn, wa, wb, wo, g1, b1, rw, rb, dn_alpha):
    t, d = h.shape
    tm = _moe_tile(t)
    row = lambda c: pl.BlockSpec((tm, d), lambda i: (i, c))
    full = lambda shp: pl.BlockSpec(shp, lambda i: (0, 0))
    rwh = rw.astype(BF16)
    rwl = (rw - rwh.astype(F32)).astype(BF16)
    return pl.pallas_call(
        functools.partial(_merge_kernel, dn_alpha=dn_alpha),
        out_shape=[jax.ShapeDtypeStruct((t, d), F32),
                   jax.ShapeDtypeStruct((t, LANES), F32),
                   jax.ShapeDtypeStruct((LANES, t), F32),
                   jax.ShapeDtypeStruct((t // tm, 1, LANES), F32)],
        grid=(t // tm,),
        in_specs=[row(0), row(0), row(0), row(0), row(1), row(2), row(0),
                  full((1, d)), full((d, d)), full((d, d)), full((d, d)),
                  full((1, d)), full((1, d)), full((d, LANES)), full((d, LANES)), full((1, LANES))],
        out_specs=[pl.BlockSpec((tm, d), lambda i: (i, 0)),
                   pl.BlockSpec((tm, LANES), lambda i: (i, 0)),
                   pl.BlockSpec((LANES, tm), lambda i: (0, i)),
                   pl.BlockSpec((None, 1, LANES), lambda i: (i, 0, 0))],
        compiler_params=_cparams(("parallel",)),
        name="merge_ln1_router",
    )(o_f, o_b, pb, o_diff, pb, pb, h, gn, wa, wb, wo, g1, b1, rwh, rwl, rb)


def _moe_tile(t):
    return min(ROW_TILE, t)


def _moe_slots(tm):
    return tm * TOP_K + N_EXPERTS * MOE_SEG


def _piece_loops(j, lo_s, gb_s, nbig_s, nsmall_s, make_big, make_small):
    def per_expert(e, carry):
        idx = j * N_EXPERTS + e
        base, g, nbig = lo_s[idx], gb_s[idx], nbig_s[idx]

        def big_piece(p, c):
            off = p * MOE_BIG
            make_big(pl.multiple_of(base + off, MOE_SEG), pl.multiple_of(g + off, MOE_SEG)).start()
            return c
        lax.fori_loop(0, nbig, big_piece, 0)

        def small_piece(p, c):
            off = nbig * MOE_BIG + p * MOE_SEG
            make_small(pl.multiple_of(base + off, MOE_SEG), pl.multiple_of(g + off, MOE_SEG)).start()
            return c
        lax.fori_loop(0, nsmall_s[idx], small_piece, 0)
        return carry
    lax.fori_loop(0, N_EXPERTS, per_expert, 0)


def _wait_pieces(n, make_copy):
    def body(p, c):
        make_copy(0, 0).wait()
        return c
    lax.fori_loop(0, n, body, 0)


def _dispatch_kernel(lo_s, gb_s, nbig_s, nsmall_s, tbig_s, tsmall_s, ts_s, tnp_s, h_ref, routet_ref, locol_ref,
                     xb_ref, sorted_s, zero_s, sem_big, sem_small, zsem, *, nt):
    tm = h_ref.shape[0]
    n_slots = sorted_s.shape[1]
    j = pl.program_id(0)
    par = j % 2

    def seg_copy(pp, rows, sem):
        def make(local_row, global_row):
            return pltpu.make_async_copy(sorted_s.at[pp, pl.ds(local_row, rows)],
                                         xb_ref.at[pl.ds(global_row, rows)], sem.at[pp])
        return make

    big = lambda pp: seg_copy(pp, MOE_BIG, sem_big)
    small = lambda pp: seg_copy(pp, MOE_SEG, sem_small)

    blk_rows = zero_s.shape[0]

    def zero_tail(_, global_row):
        return pltpu.make_async_copy(zero_s.at[pl.ds(0, MOE_SEG)], xb_ref.at[pl.ds(global_row, MOE_SEG)],
                                     zsem.at[0])

    def zero_block(_, global_row):
        return pltpu.make_async_copy(zero_s, xb_ref.at[pl.ds(global_row, blk_rows)], zsem.at[1])

    @pl.when(j == 0)
    def _():
        zero_s[...] = jnp.zeros_like(zero_s)

        def per_region(e, carry):
            def per_piece(p, c):
                zero_tail(0, pl.multiple_of(ts_s[e] + p * MOE_SEG, MOE_SEG)).start()
                return c
            lax.fori_loop(0, tnp_s[e], per_piece, 0)
            return carry
        lax.fori_loop(0, N_EXPERTS, per_region, 0)

        def per_block(p, c):
            zero_block(0, pl.multiple_of(ts_s[N_EXPERTS] + p * blk_rows, blk_rows)).start()
            return c
        lax.fori_loop(0, tnp_s[N_EXPERTS], per_block, 0)

    routet = routet_ref[...]
    sub = lax.broadcasted_iota(jnp.int32, (LANES, tm), 0).astype(F32)
    slot_iota = lax.broadcasted_iota(jnp.int32, (n_slots, tm), 0).astype(F32)
    perm = jnp.zeros((n_slots, tm), F32)
    for k in range(TOP_K):
        e_row = routet[ROUTE_E + k:ROUTE_E + k + 1, :]
        lo_k = jnp.sum(jnp.where(sub == e_row, locol_ref[...], 0.0), axis=0, keepdims=True)
        slot = lo_k + routet[ROUTE_R + k:ROUTE_R + k + 1, :]
        perm = jnp.where(slot_iota == slot, 1.0, perm)
    sorted_s[par] = jnp.dot(perm.astype(BF16), h_ref[...].astype(BF16), preferred_element_type=F32)

    _piece_loops(j, lo_s, gb_s, nbig_s, nsmall_s, big(par), small(par))

    @pl.when(j > 0)
    def _():
        _wait_pieces(tbig_s[j - 1], big(1 - par))
        _wait_pieces(tsmall_s[j - 1], small(1 - par))

    @pl.when(j == nt - 1)
    def _():
        _wait_pieces(tbig_s[j], big(par))
        _wait_pieces(tsmall_s[j], small(par))
        _wait_pieces(tnp_s[N_EXPERTS + 1], zero_tail)
        _wait_pieces(tnp_s[N_EXPERTS], zero_block)


def _dispatch(tables, h1, routet, lo_col, n_rows, tb):
    t, d = h1.shape
    tm = _moe_tile(t)
    nt = t // tm
    n_slots = _moe_slots(tm)
    return pl.pallas_call(
        functools.partial(_dispatch_kernel, nt=nt),
        out_shape=jax.ShapeDtypeStruct((n_rows, d), F32),
        grid_spec=pltpu.PrefetchScalarGridSpec(
            num_scalar_prefetch=len(tables),
            grid=(nt,),
            in_specs=[
                pl.BlockSpec((tm, d), lambda i, *_: (i, 0)),
                pl.BlockSpec((LANES, tm), lambda i, *_: (0, i)),
                pl.BlockSpec((None, LANES, 1), lambda i, *_: (i, 0, 0)),
            ],
            out_specs=pl.BlockSpec(memory_space=pl.ANY),
            scratch_shapes=[pltpu.VMEM((2, n_slots, d), F32), pltpu.VMEM((tb, d), F32),
                            pltpu.SemaphoreType.DMA((2,)), pltpu.SemaphoreType.DMA((2,)),
                            pltpu.SemaphoreType.DMA((2,))],
        ),
        compiler_params=_cparams(("arbitrary",)),
        name="moe_dispatch",
    )(*tables, h1, routet, lo_col)


def _experts_kernel(be_ref, nu_ref, nx_ref, x_ref, wg_hbm, bg_ref, wu_hbm, bu_ref, wd_hbm, bd_ref, y_ref,
                    stage_s, wg_s, wu_s, wd_s, sem):
    i = pl.program_id(0)

    def fetch(e):
        return [pltpu.make_async_copy(w.at[e], stage_s.at[n], sem.at[n])
                for n, w in enumerate((wg_hbm, wu_hbm, wd_hbm))]

    @pl.when(i >= nu_ref[0])
    def _():
        y_ref[...] = jnp.zeros_like(y_ref)

    @pl.when(i < nu_ref[0])
    def _():
        e = be_ref[i]
        prev = be_ref[jnp.maximum(i - 1, 0)]

        @pl.when(i == 0)
        def _():
            for c in fetch(e):
                c.start()

        @pl.when((i == 0) | (e != prev))
        def _():
            for c in fetch(e):
                c.wait()
            wg_s[...] = stage_s[0].astype(BF16)
            wu_s[...] = stage_s[1].astype(BF16)
            wd_s[...] = stage_s[2].astype(BF16)

            @pl.when(nx_ref[i] != e)
            def _():
                for c in fetch(nx_ref[i]):
                    c.start()

        x = x_ref[...].astype(BF16)
        g = jnp.minimum(jnp.dot(x, wg_s[...], preferred_element_type=F32) + bg_ref[...], SWIGLU_LIMIT)
        u = jnp.clip(jnp.dot(x, wu_s[...], preferred_element_type=F32) + bu_ref[...],
                     -SWIGLU_LIMIT, SWIGLU_LIMIT)
        a = g * _sigmoid(SWIGLU_ALPHA * g) * (u + 1.0)
        y_ref[...] = jnp.dot(a.astype(BF16), wd_s[...], preferred_element_type=F32) + bd_ref[...]


def _experts(blk_e, n_used, blk_next, xb, wg, bg, wu, bu, wd, bd, tb):
    p, d = xb.shape
    ne, _, dff = wg.shape
    assert d == dff, (d, dff)
    nb = p // tb
    rowblk = lambda i, be, nu, nx: (jnp.maximum(jnp.minimum(i, nu[0] - 1), 0), 0)
    bspec = lambda n: pl.BlockSpec((None, 1, n), lambda i, be, nu, nx: (be[i], 0, 0))
    hbm = pl.BlockSpec(memory_space=pl.ANY)
    return pl.pallas_call(
        _experts_kernel,
        out_shape=jax.ShapeDtypeStruct((p, d), F32),
        grid_spec=pltpu.PrefetchScalarGridSpec(
            num_scalar_prefetch=3,
            grid=(nb,),
            in_specs=[pl.BlockSpec((tb, d), rowblk), hbm, bspec(dff), hbm, bspec(dff), hbm, bspec(d)],
            out_specs=pl.BlockSpec((tb, d), lambda i, be, nu, nx: (i, 0)),
            scratch_shapes=[pltpu.VMEM((3, d, dff), F32), pltpu.VMEM((d, dff), BF16),
                            pltpu.VMEM((d, dff), BF16), pltpu.VMEM((dff, d), BF16),
                            pltpu.SemaphoreType.DMA((3,))],
        ),
        compiler_params=_cparams(("arbitrary",)),
        name="moe_experts",
    )(blk_e, n_used, blk_next, xb, wg, bg.reshape(ne, 1, dff), wu, bu.reshape(ne, 1, dff), wd,
      bd.reshape(ne, 1, d))


def _combine_kernel(lo_s, gb_s, nbig_s, nsmall_s, tbig_s, tsmall_s, yb_ref, h_ref, route_ref, lorow_ref,
                    g_ref, b_ref, o_ref, ybuf, sem_big, sem_small, *, dn_alpha, nt):
    tm = h_ref.shape[0]
    n_slots = ybuf.shape[1]
    j = pl.program_id(0)
    par = j % 2

    def seg_copy(pp, rows, sem):
        def make(local_row, global_row):
            return pltpu.make_async_copy(yb_ref.at[pl.ds(global_row, rows)],
                                         ybuf.at[pp, pl.ds(local_row, rows)], sem.at[pp])
        return make

    big = lambda pp: seg_copy(pp, MOE_BIG, sem_big)
    small = lambda pp: seg_copy(pp, MOE_SEG, sem_small)

    @pl.when(j == 0)
    def _():
        ybuf[...] = jnp.zeros_like(ybuf)
        _piece_loops(j, lo_s, gb_s, nbig_s, nsmall_s, big(0), small(0))

    @pl.when(j + 1 < nt)
    def _():
        _piece_loops(j + 1, lo_s, gb_s, nbig_s, nsmall_s, big(1 - par), small(1 - par))

    _wait_pieces(tbig_s[j], big(par))
    _wait_pieces(tsmall_s[j], small(par))

    route = route_ref[...]
    lane = lax.broadcasted_iota(jnp.int32, (tm, LANES), 1).astype(F32)
    slot_iota = lax.broadcasted_iota(jnp.int32, (tm, n_slots), 1).astype(F32)
    wgt = jnp.zeros((tm, n_slots), F32)
    for k in range(TOP_K):
        e_col = route[:, ROUTE_E + k:ROUTE_E + k + 1]
        lo_k = jnp.sum(jnp.where(lane == e_col, lorow_ref[...], 0.0), axis=-1, keepdims=True)
        slot = lo_k + route[:, ROUTE_R + k:ROUTE_R + k + 1]
        wgt = jnp.where(slot_iota == slot, route[:, ROUTE_G + k:ROUTE_G + k + 1], wgt)
    y = ybuf[par]
    wh = wgt.astype(BF16)
    wl = (wgt - wh.astype(F32)).astype(BF16)
    yh = y.astype(BF16)
    yl = (y - yh.astype(F32)).astype(BF16)
    ffn = (jnp.dot(wh, yh, preferred_element_type=F32) + jnp.dot(wh, yl, preferred_element_type=F32)
           + jnp.dot(wl, yh, preferred_element_type=F32))
    o_ref[...] = _layer_norm(dn_alpha * h_ref[...] + ffn, g_ref[...], b_ref[...])


def _combine(tables, yb, h1, route, lo_row, g2, b2, dn_alpha):
    t, d = h1.shape
    tm = _moe_tile(t)
    nt = t // tm
    return pl.pallas_call(
        functools.partial(_combine_kernel, dn_alpha=dn_alpha, nt=nt),
        out_shape=jax.ShapeDtypeStruct((t, d), F32),
        grid_spec=pltpu.PrefetchScalarGridSpec(
            num_scalar_prefetch=len(tables),
            grid=(nt,),
            in_specs=[
                pl.BlockSpec(memory_space=pl.ANY),
                pl.BlockSpec((tm, d), lambda i, *_: (i, 0)),
                pl.BlockSpec((tm, LANES), lambda i, *_: (i, 0)),
                pl.BlockSpec((None, 1, LANES), lambda i, *_: (i, 0, 0)),
                pl.BlockSpec((1, d), lambda i, *_: (0, 0)),
                pl.BlockSpec((1, d), lambda i, *_: (0, 0)),
            ],
            out_specs=pl.BlockSpec((tm, d), lambda i, *_: (i, 0)),
            scratch_shapes=[pltpu.VMEM((2, _moe_slots(tm), d), F32), pltpu.SemaphoreType.DMA((2,)),
                            pltpu.SemaphoreType.DMA((2,))],
        ),
        compiler_params=_cparams(("arbitrary",)),
        name="moe_combine_ln2",
    )(*tables, yb, h1, route, lo_row, g2.reshape(1, d), b2.reshape(1, d))


def _moe_tables(counts, tm, tb, n_rows):
    seg = ((counts + MOE_SEG - 1) // MOE_SEG) * MOE_SEG
    lo = jnp.cumsum(seg, axis=1) - seg
    rows_e = jnp.sum(seg, axis=0)
    region = ((rows_e + tb - 1) // tb) * tb
    pend = jnp.cumsum(region)
    pstart = pend - region
    gbase = pstart[None, :] + jnp.cumsum(seg, axis=0) - seg
    n_big = seg // MOE_BIG
    n_small = (seg - n_big * MOE_BIG) // MOE_SEG
    tail_start = jnp.concatenate([pstart + rows_e, pend[-1:]])
    tail_small = (region - rows_e) // MOE_SEG
    tail_pieces = jnp.concatenate([tail_small, (n_rows - pend[-1:]) // tb, jnp.sum(tail_small, keepdims=True)])
    i32 = lambda a: a.reshape(-1).astype(jnp.int32)
    seg_tables = (i32(lo), i32(gbase), i32(n_big), i32(n_small),
                  i32(jnp.sum(n_big, axis=1)), i32(jnp.sum(n_small, axis=1)))
    n_used = (pend[-1] // tb).astype(jnp.int32)
    blk = jnp.minimum(jnp.arange(n_rows // tb, dtype=jnp.int32), n_used - 1)
    blk_e = jnp.minimum(jnp.sum((pend[None, :] <= (blk * tb)[:, None]).astype(jnp.int32), axis=1),
                        N_EXPERTS - 1)
    ids = jnp.arange(N_EXPERTS, dtype=jnp.int32)
    later = jnp.where((region[None, :] > 0) & (ids[None, :] > ids[:, None]), ids[None, :], N_EXPERTS)
    nxt_e = jnp.min(later, axis=1)
    nxt_e = jnp.where(nxt_e < N_EXPERTS, nxt_e, ids).astype(jnp.int32)
    lo_f = jnp.pad(lo.astype(F32), ((0, 0), (0, LANES - N_EXPERTS)))
    return (seg_tables, (i32(tail_start), i32(tail_pieces)), (blk_e, n_used.reshape(1), nxt_e[blk_e]), lo_f)


def kernel(x, ln0_g, ln0_b, w_in, gla_wf_fwd, gla_bf_fwd, gla_wf_bwd, gla_bf_bwd, gla_norm_g, diff_lq1, diff_lk1, diff_lq2, diff_lk2, diff_norm_g, w_br_gla, w_br_diff, w_out, ln1_g, ln1_b, router_w, router_b, exp_w_gate, exp_b_gate, exp_w_up, exp_b_up, exp_w_down, exp_b_down, ln2_g, ln2_b):
    batch, seq, d = x.shape
    depth = w_in.shape[0]
    t = batch * seq
    dn_alpha = (2 * depth) ** 0.25
    gk, gv = GLA_HEADS * GLA_DK, GLA_HEADS * GLA_DV
    dq = DIFF_HEADS * 2 * DIFF_DH
    splits = (gk, gk, gv, gv, GLA_RANK, GLA_RANK, dq, dq, dq, d, d)
    cuts = [0]
    for s in splits:
        cuts.append(cuts[-1] + s)
    col = lambda w, i: w[:, cuts[i]:cuts[i + 1]]
    tabs = _rope_tables(seq)
    tm = _moe_tile(t)
    tb = _moe_tile(t)
    n_rows = t * TOP_K + (t // tm) * N_EXPERTS * (MOE_SEG - 1) + N_EXPERTS * (tb - 1)
    n_rows = ((n_rows + tb - 1) // tb) * tb

    assert depth == 1, depth
    cur = x.reshape(t, d)
    g_in, b_in = ln0_g, ln0_b
    for l in range(depth):
        lam_init = 0.8 - 0.6 * math.exp(-0.3 * l)
        w = w_in[l]
        wa = jnp.concatenate([col(w, 0), col(w, 1), col(w, 2), col(w, 6), col(w, 7), col(w, 8)], axis=1).astype(BF16)
        wb = jnp.concatenate([col(w, 3), col(w, 9), col(w, 10), col(w, 4), col(w, 5),
                              jnp.zeros((d, LANES - 2 * GLA_RANK), F32)], axis=1).astype(BF16)
        q_lo = 2 * gk + gv
        k_cols = (q_lo + dq, q_lo + 2 * dq)
        pa, pb, h, kgt, kdt = _ln_proj(cur, g_in, b_in, wa, wb, tabs, seq, PROJ_COLS_BF16, PROJ_COLS_F32,
                                       (q_lo, q_lo + dq), k_cols, DIFF_DH ** -0.5 * math.log2(math.e),
                                       ((gk, 2 * gk), k_cols))

        lrt = pb[:, 3 * d:3 * d + 2 * GLA_RANK].reshape(batch, seq, 2 * GLA_RANK).transpose(0, 2, 1)
        zr = jnp.zeros((GLA_RANK, gk), F32)
        pad = jnp.zeros((LANES - 2 * GLA_RANK, gk), F32)
        wf_f = jnp.concatenate([gla_wf_fwd[l], zr, pad], axis=0)
        wf_b = jnp.concatenate([zr, gla_wf_bwd[l], pad], axis=0)
        o_f, o_b = _gla(
            pa, kgt, pb, lrt,
            (wf_f, wf_f[:2 * GLA_RANK].T, gla_bf_fwd[l].reshape(1, gk), gla_bf_fwd[l].reshape(gk, 1)),
            (wf_b, wf_b[:2 * GLA_RANK].T, gla_bf_bwd[l].reshape(1, gk), gla_bf_bwd[l].reshape(gk, 1)),
            batch, seq)

        o_diff = _diff_attn(pa, kdt, diff_lq1[l], diff_lk1[l], diff_lq2[l], diff_lk2[l], diff_norm_g[l],
                            batch, seq, lam_init)

        rw = jnp.concatenate([router_w[l], jnp.zeros((d, LANES - N_EXPERTS), F32)], axis=1)
        rb = jnp.concatenate([router_b[l], jnp.full((LANES - N_EXPERTS,), -jnp.inf, F32)]).reshape(1, LANES)
        h1, route, routet, counts = _merge(
            o_f, o_b, pb, o_diff, h, jnp.tile(gla_norm_g[l], GLA_HEADS).reshape(1, gv),
            w_br_gla[l].astype(BF16), w_br_diff[l].astype(BF16), w_out[l].astype(BF16),
            ln1_g[l].reshape(1, d), ln1_b[l].reshape(1, d), rw, rb, dn_alpha)

        seg_tables, tail_tables, blk_tables, lo_f = _moe_tables(
            counts[:, 0, :N_EXPERTS].astype(jnp.int32), tm, tb, n_rows)
        xb = _dispatch(seg_tables + tail_tables, h1, routet, lo_f[:, :, None], n_rows, tb)
        yb = _experts(*blk_tables, xb, exp_w_gate[l], exp_b_gate[l], exp_w_up[l], exp_b_up[l],
                      exp_w_down[l], exp_b_down[l], tb)
        cur = _combine(seg_tables, yb, h1, route, lo_f[:, None, :], ln2_g[l], ln2_b[l], dn_alpha)
    return cur.reshape(batch, seq, d)
```

```python
import functools
import math

import jax
import jax.numpy as jnp
from jax import lax
from jax.experimental import pallas as pl
from jax.experimental.pallas import tpu as pltpu

F32 = jnp.float32
BF16 = jnp.bfloat16

GLA_HEADS = 4
GLA_DK = 128
GLA_DV = 256
GLA_RANK = 16
GLA_TAU = 16.0
GLA_CHUNK = 64
DIFF_HEADS = 8
DIFF_DH = 64
ROT_DIM = DIFF_DH // 4
ROPE_THETA = 500000.0
N_EXPERTS = 32
TOP_K = 4
SWIGLU_LIMIT = 7.0
SWIGLU_ALPHA = 1.702
EPS = 1e-5

LANES = 128
SUBLANES = 8
VMEM_LIMIT = 48 * 1024 * 1024

ROW_TILE = 256
PROJ_COLS_BF16 = 512
PROJ_COLS_F32 = 640
GLA_CHUNKS_PER_STEP = 4
ATTN_Q_TILE = 1024
ATTN_K_TILE = 512

ROUTE_E = 0
ROUTE_G = TOP_K
ROUTE_R = 2 * TOP_K
MOE_SEG = SUBLANES
MOE_BIG = 4 * MOE_SEG


def _cparams(sem):
    return pltpu.CompilerParams(dimension_semantics=sem, vmem_limit_bytes=VMEM_LIMIT)


def _layer_norm(x, g, b):
    mu = jnp.mean(x, axis=-1, keepdims=True)
    xc = x - mu
    var = jnp.mean(xc * xc, axis=-1, keepdims=True)
    return xc * lax.rsqrt(var + EPS) * g + b


def _sigmoid(x):
    return 1.0 / (1.0 + jnp.exp(-x))


def _log_sigmoid(x):
    return jnp.minimum(x, 0.0) - jnp.log(1.0 + jnp.exp(-jnp.abs(x)))


def _lane_tile(x, n):
    return jnp.concatenate([x] * n, axis=1)


def _split3(x):
    hi = x.astype(BF16)
    r1 = x - hi.astype(F32)
    mid = r1.astype(BF16)
    lo = (r1 - mid.astype(F32)).astype(BF16)
    return hi, mid, lo


def _rope(x, c, a, b):
    return (x * c + pltpu.roll(x, LANES - ROT_DIM // 2, 1) * a + pltpu.roll(x, ROT_DIM // 2, 1) * b)


def _rope_coeffs(cs):
    half = ROT_DIM // 2
    r = lax.broadcasted_iota(jnp.int32, (ROT_DIM, LANES), 0)
    j = jnp.bitwise_and(lax.broadcasted_iota(jnp.int32, (ROT_DIM, LANES), 1), DIFF_DH - 1)
    e_c = jnp.where((j == r) | (j == r + half), 1.0, 0.0) * jnp.where(r < half, 1.0, 0.0)
    e_a = jnp.where(j == r - half, -1.0, 0.0)
    e_b = jnp.where(j == r, 1.0, 0.0) * jnp.where(r >= half, 1.0, 0.0)
    sel = jnp.concatenate([e_c, e_a, e_b], axis=1).astype(BF16)
    out = sum(jnp.dot(part, sel, preferred_element_type=F32) for part in _split3(cs))
    lane = jnp.bitwise_and(lax.broadcasted_iota(jnp.int32, (1, LANES), 1), DIFF_DH - 1)
    ones = jnp.where(lane >= ROT_DIM, 1.0, 0.0)
    return out[:, :LANES] + ones, out[:, LANES:2 * LANES], out[:, 2 * LANES:]


def _ln_proj_kernel(x_ref, g_ref, b_ref, wa_ref, wb_ref, cs_ref, pa_ref, pb_ref, h_ref, *kt_refs,
                    tna, tnb, q_cols, k_cols, q_scale, t_cols):
    h = _layer_norm(x_ref[...], g_ref[...], b_ref[...])
    h_ref[...] = h
    h16 = h.astype(BF16)
    rope_c, rope_a, rope_b = _rope_coeffs(cs_ref[...])
    for j in range(wa_ref.shape[1] // tna):
        res = jnp.dot(h16, wa_ref[:, j * tna:(j + 1) * tna], preferred_element_type=F32)
        for c in range(0, tna, LANES):
            col = j * tna + c
            blk = res[:, c:c + LANES]
            if q_cols[0] <= col < q_cols[1]:
                blk = _rope(blk, rope_c, rope_a, rope_b) * q_scale
            elif k_cols[0] <= col < k_cols[1]:
                blk = _rope(blk, rope_c, rope_a, rope_b)
            pa_ref[:, col:col + LANES] = blk.astype(pa_ref.dtype)
            for (lo, hi), kt_ref in zip(t_cols, kt_refs):
                if lo <= col < hi:
                    kt_ref[col - lo:col - lo + LANES, :] = blk.T.astype(kt_ref.dtype)
    for j in range(wb_ref.shape[1] // tnb):
        cols = slice(j * tnb, (j + 1) * tnb)
        pb_ref[:, cols] = jnp.dot(h16, wb_ref[:, cols], preferred_element_type=F32)


def _ln_proj(x, g, b, wa16, wb16, cs, seq, tna, tnb, q_cols, k_cols, q_scale, t_cols):
    t, d = x.shape
    na, nb = wa16.shape[1], wb16.shape[1]
    tm = min(ROW_TILE, t)
    nrow = seq // tm
    row = lambda n: pl.BlockSpec((tm, n), lambda i: (i, 0))
    full = lambda r, n: pl.BlockSpec((r, n), lambda i: (0, 0))
    tab = pl.BlockSpec((tm, ROT_DIM), lambda i: (i % nrow, 0))
    widths = [hi - lo for lo, hi in t_cols]
    return pl.pallas_call(
        functools.partial(_ln_proj_kernel, tna=tna, tnb=tnb, q_cols=q_cols, k_cols=k_cols, q_scale=q_scale,
                          t_cols=t_cols),
        out_shape=[jax.ShapeDtypeStruct((t, na), BF16), jax.ShapeDtypeStruct((t, nb), F32),
                   jax.ShapeDtypeStruct((t, d), F32)]
        + [jax.ShapeDtypeStruct((t // seq, w, seq), BF16) for w in widths],
        grid=(t // tm,),
        in_specs=[row(d), full(1, d), full(1, d), full(d, na), full(d, nb), tab],
        out_specs=[row(na), row(nb), row(d)]
        + [pl.BlockSpec((None, w, tm), lambda i: (i // nrow, 0, i % nrow)) for w in widths],
        compiler_params=_cparams(("parallel",)),
        name="ln_proj",
    )(x, g.reshape(1, d), b.reshape(1, d), wa16, wb16, cs)


def _gla_kernel(*refs, n_chunks, batch):
    ins, (of_ref, ob_ref, sf_ref, sb_ref) = refs[:-4], refs[-4:]
    n_in = len(ins) // 2

    @pl.when(pl.program_id(1) == 0)
    def _():
        sf_ref[...] = jnp.zeros_like(sf_ref)
        sb_ref[...] = jnp.zeros_like(sb_ref)

    chains = []
    for bb in range(batch):
        chains.append(_gla_chain(*ins[:n_in], of_ref, sf_ref, bb=bb, rev=False, n_chunks=n_chunks))
        chains.append(_gla_chain(*ins[n_in:], ob_ref, sb_ref, bb=bb, rev=True, n_chunks=n_chunks))
    for _ in zip(*chains):
        pass


def _gla_chain(q_ref, kt_ref, v_ref, lr_ref, lrt_ref, wf_ref, wft_ref, bf_ref, bft_ref,
               o_ref, s_ref, *, bb, rev, n_chunks):
    c_len = GLA_CHUNK

    z = jnp.dot(lr_ref[bb].astype(BF16), wf_ref[...].astype(BF16), preferred_element_type=F32) + bf_ref[...]
    logf = _log_sigmoid(z) * (1.0 / GLA_TAU)
    zt = jnp.dot(wft_ref[...].astype(BF16), lrt_ref[bb].astype(BF16), preferred_element_type=F32) + bft_ref[...]
    logft = _log_sigmoid(zt) * (1.0 / GLA_TAU)
    yield

    r = n_chunks * c_len
    shift = c_len.bit_length() - 1
    ri = lax.broadcasted_iota(jnp.int32, (r, r), 0)
    ci = lax.broadcasted_iota(jnp.int32, (r, r), 1)
    same = jnp.right_shift(ri, shift) == jnp.right_shift(ci, shift)
    lower = same & (ri >= ci)
    upper = same & (ri <= ci)
    keep = upper if rev else lower
    tri = keep.astype(BF16)
    tri_t = (lower if rev else upper).astype(BF16)
    blk = same.astype(BF16)
    q_scale = GLA_DK ** -0.5

    b = sum(jnp.dot(tri, part, preferred_element_type=F32) for part in _split3(logf))
    bts = sum(jnp.dot(part, jnp.concatenate([tri_t, blk], axis=1), preferred_element_type=F32)
              for part in _split3(logft))
    bt, bt_tot = bts[:, :r], bts[:, r:]
    yield
    q_in = (q_ref[bb].astype(F32) * q_scale * jnp.exp(b)).astype(BF16)
    kt = kt_ref[bb].astype(F32)
    k_in_t = (kt * jnp.exp(-bt)).astype(BF16)
    k_st_t = (kt * jnp.exp(bt_tot - bt)).astype(BF16)
    decay = jnp.exp(bt_tot)
    v = v_ref[bb]
    yield
    att = jnp.dot(q_in, k_in_t, preferred_element_type=F32)
    att = jnp.where(keep, att, 0.0).astype(BF16)
    yield
    o_intra = jnp.dot(att, v, preferred_element_type=F32)
    yield

    order = range(n_chunks - 1, -1, -1) if rev else range(n_chunks)
    for c in order:
        rows = slice(c * c_len, (c + 1) * c_len)
        state = s_ref[bb]
        o_ref[bb, rows, :] = o_intra[rows, :] + jnp.dot(q_in[rows, :], state.astype(BF16),
                                                         preferred_element_type=F32)
        s_ref[bb] = (decay[:, c * c_len:c * c_len + 1] * state
                     + jnp.dot(k_st_t[:, rows], v[rows, :], preferred_element_type=F32))
        yield


def _gla(pa, kgt, pb, lrt, gates_f, gates_b, batch, seq):
    t = pa.shape[0]
    n_chunks = GLA_CHUNKS_PER_STEP if seq % (GLA_CHUNKS_PER_STEP * GLA_CHUNK) == 0 else 1
    r = n_chunks * GLA_CHUNK
    ng = seq // r
    dk, dv, h = GLA_DK, GLA_DV, GLA_HEADS
    v_off = (2 * h * dk) // dv
    lr_blk = (3 * h * dv) // LANES

    def specs(gi):
        return [
            pl.BlockSpec((batch, r, dk), lambda hh, g: (0, gi(g), hh)),
            pl.BlockSpec((batch, dk, r), lambda hh, g: (0, hh, gi(g))),
            pl.BlockSpec((batch, r, dv), lambda hh, g: (0, gi(g), v_off + hh)),
            pl.BlockSpec((batch, r, LANES), lambda hh, g: (0, gi(g), lr_blk)),
            pl.BlockSpec((batch, 2 * GLA_RANK, r), lambda hh, g: (0, 0, gi(g))),
            pl.BlockSpec((LANES, dk), lambda hh, g: (0, hh)),
            pl.BlockSpec((dk, 2 * GLA_RANK), lambda hh, g: (hh, 0)),
            pl.BlockSpec((1, dk), lambda hh, g: (0, hh)),
            pl.BlockSpec((dk, 1), lambda hh, g: (hh, 0)),
        ]

    fwd = lambda g: g
    bwd = lambda g: ng - 1 - g
    out = lambda gi: pl.BlockSpec((batch, r, dv), lambda hh, g: (0, gi(g), hh))
    pa3 = pa.reshape(batch, seq, pa.shape[1])
    pb3 = pb.reshape(batch, seq, pb.shape[1])
    o_f, o_b = pl.pallas_call(
        functools.partial(_gla_kernel, n_chunks=n_chunks, batch=batch),
        out_shape=[jax.ShapeDtypeStruct((batch, seq, h * dv), F32)] * 2,
        grid=(h, ng),
        in_specs=specs(fwd) + specs(bwd),
        out_specs=[out(fwd), out(bwd)],
        scratch_shapes=[pltpu.VMEM((batch, dk, dv), F32)] * 2,
        compiler_params=_cparams(("parallel", "arbitrary")),
        name="gla",
    )(pa3, kgt, pa3, pb3, lrt, *gates_f, pa3, kgt, pa3, pb3, lrt, *gates_b)
    return o_f.reshape(t, h * dv), o_b.reshape(t, h * dv)


def _diff_attn_kernel(q_s, kt_s, v_ref, lq1_ref, lk1_ref, lq2_ref, lk2_ref, g_ref, o_ref,
                      ve_s, m_s, acc_s, sa_s, sb_s, ma_s, mb_s, aa_s, ab_s, *, seq, tk, lam_init):
    dh = DIFF_DH
    hw = 2 * dh

    @pl.when(pl.program_id(2) == 0)
    def _():
        def body(i, carry):
            rows = pl.ds(pl.multiple_of(i * tk, tk), tk)
            ve_s[rows, :] = jnp.concatenate([v_ref[rows, :], jnp.ones((tk, hw), BF16)], axis=1)
            return carry
        lax.fori_loop(0, seq // tk, body, 0)

    m_s[...] = jnp.full_like(m_s, -jnp.inf)
    acc_s[...] = jnp.zeros_like(acc_s)

    def scores(i, buf):
        s_buf, mx_buf, al_buf = buf
        cols = pl.ds(pl.multiple_of(i * tk, tk), tk)
        kt = kt_s[:, cols]
        for p in range(2):
            s = jnp.dot(q_s[:, p * dh:(p + 1) * dh], kt[p * dh:(p + 1) * dh, :], preferred_element_type=F32)
            s_buf[p] = s
            m_old = m_s[p]
            m_new = jnp.maximum(m_old, jnp.max(s, axis=-1, keepdims=True))
            al_buf[p] = jnp.exp2(m_old - m_new)
            mx_buf[p] = m_new
            m_s[p] = m_new

    def softmax_pv(i, buf):
        s_buf, mx_buf, al_buf = buf
        ve = ve_s[pl.ds(pl.multiple_of(i * tk, tk), tk), :]
        for p in range(2):
            pr = jnp.exp2((s_buf[p] - _lane_tile(mx_buf[p], tk // LANES)).astype(BF16))
            acc_s[p] = (_lane_tile(al_buf[p], 2) * acc_s[p]
                        + jnp.dot(pr, ve, preferred_element_type=F32))

    n_t = seq // tk
    buf_a, buf_b = (sa_s, ma_s, aa_s), (sb_s, mb_s, ab_s)
    scores(0, buf_a)

    def pair_body(j, carry):
        scores(2 * j + 1, buf_b)
        softmax_pv(2 * j, buf_a)
        scores(2 * j + 2, buf_a)
        softmax_pv(2 * j + 1, buf_b)
        return carry
    lax.fori_loop(0, (n_t - 1) // 2, pair_body, 0)
    if n_t % 2 == 0:
        scores(n_t - 1, buf_b)
        softmax_pv(n_t - 2, buf_a)
        softmax_pv(n_t - 1, buf_b)
    else:
        softmax_pv(n_t - 1, buf_a)

    lam = (jnp.exp(jnp.sum(lq1_ref[...] * lk1_ref[...], axis=-1, keepdims=True))
           - jnp.exp(jnp.sum(lq2_ref[...] * lk2_ref[...], axis=-1, keepdims=True)) + lam_init)
    o = acc_s[0, :, :hw] / acc_s[0, :, hw:] - lam * (acc_s[1, :, :hw] / acc_s[1, :, hw:])
    ms = jnp.mean(o * o, axis=-1, keepdims=True)
    o_ref[...] = (o * lax.rsqrt(ms + EPS) * g_ref[...] * (1.0 - lam_init)).astype(o_ref.dtype)


def _diff_attn(pa, kdt, lq1, lk1, lq2, lk2, norm_g, batch, seq, lam_init):
    t = pa.shape[0]
    h, dh = DIFF_HEADS, DIFF_DH
    hw = 2 * dh
    tq = min(ATTN_Q_TILE, seq)
    tk = min(ATTN_K_TILE, seq)
    nq = seq // tq
    q_off = (2 * GLA_HEADS * GLA_DK + GLA_HEADS * GLA_DV) // hw
    v_off = q_off + 2 * h
    vec = lambda n: pl.BlockSpec((1, n), lambda b, hh, qi: (0, 0))
    return pl.pallas_call(
        functools.partial(_diff_attn_kernel, seq=seq, tk=tk, lam_init=lam_init),
        out_shape=jax.ShapeDtypeStruct((t, h * hw), BF16),
        grid=(batch, h, nq),
        in_specs=[
            pl.BlockSpec((tq, hw), lambda b, hh, qi: (b * nq + qi, q_off + hh)),
            pl.BlockSpec((None, hw, seq), lambda b, hh, qi: (b, hh, 0)),
            pl.BlockSpec((seq, hw), lambda b, hh, qi: (b, v_off + hh)),
            vec(dh), vec(dh), vec(dh), vec(dh), vec(hw),
        ],
        out_specs=pl.BlockSpec((tq, hw), lambda b, hh, qi: (b * nq + qi, hh)),
        scratch_shapes=[
            pltpu.VMEM((seq, 2 * hw), BF16),
            pltpu.VMEM((2, tq, LANES), F32),
            pltpu.VMEM((2, tq, 2 * hw), F32),
            pltpu.VMEM((2, tq, tk), F32),
            pltpu.VMEM((2, tq, tk), F32),
        ] + [pltpu.VMEM((2, tq, LANES), F32)] * 4,
        compiler_params=_cparams(("parallel", "parallel", "arbitrary")),
        name="diff_attn",
    )(pa, kdt, pa, lq1.reshape(1, dh), lk1.reshape(1, dh), lq2.reshape(1, dh),
      lk2.reshape(1, dh), norm_g.reshape(1, hw))


def _rope_tables(seq):
    pos = jnp.arange(seq, dtype=F32)
    inv_freq = jnp.power(ROPE_THETA, -jnp.arange(0, ROT_DIM, 2, dtype=F32) / ROT_DIM)
    ang = pos[:, None] * inv_freq[None, :]
    return jnp.concatenate([jnp.cos(ang), jnp.sin(ang)], axis=1)


def _merge_kernel(of_ref, ob_ref, pb_rg, od_ref, pb_gg, pb_gd, h_ref, gn_ref, wa_ref, wb_ref, wo_ref,
                  g1_ref, b1_ref, rwh_ref, rwl_ref, rb_ref, h1_ref, route_ref, routet_ref, cnt_ref,
                  *, dn_alpha):
    tm = of_ref.shape[0]
    og = of_ref[...] + ob_ref[...]
    parts = []
    for hh in range(GLA_HEADS):
        xh = og[:, hh * GLA_DV:(hh + 1) * GLA_DV]
        ms = jnp.mean(xh * xh, axis=-1, keepdims=True)
        parts.append(xh * lax.rsqrt(ms + EPS))
    rg = pb_rg[...]
    o_gla = jnp.concatenate(parts, axis=1) * gn_ref[...] * (rg * _sigmoid(rg))
    a = jnp.dot(o_gla.astype(BF16), wa_ref[...], preferred_element_type=F32)
    bb = jnp.dot(od_ref[...], wb_ref[...], preferred_element_type=F32)
    mixed = _sigmoid(pb_gg[...]) * a + _sigmoid(pb_gd[...]) * bb
    mix = jnp.dot(mixed.astype(BF16), wo_ref[...], preferred_element_type=F32)
    h1 = _layer_norm(dn_alpha * h_ref[...] + mix, g1_ref[...], b1_ref[...])
    h1_ref[...] = h1

    hh = h1.astype(BF16)
    hl = (h1 - hh.astype(F32)).astype(BF16)
    logits = (jnp.dot(hh, rwh_ref[...], preferred_element_type=F32)
              + jnp.dot(hh, rwl_ref[...], preferred_element_type=F32)
              + jnp.dot(hl, rwh_ref[...], preferred_element_type=F32)) + rb_ref[...]
    lane = lax.broadcasted_iota(jnp.int32, (tm, LANES), 1).astype(F32)
    work = logits
    top_v, top_i, sels = [], [], []
    for _ in range(TOP_K):
        m = jnp.max(work, axis=-1, keepdims=True)
        idx = jnp.min(jnp.where(work == m, lane, float(LANES)), axis=-1, keepdims=True)
        sel = lane == idx
        top_v.append(m)
        top_i.append(idx)
        sels.append(sel)
        work = jnp.where(sel, -jnp.inf, work)
    exps = [jnp.exp(v - top_v[0]) for v in top_v]
    denom = exps[0]
    for e in exps[1:]:
        denom = denom + e
    onehot = sels[0].astype(F32)
    for s in sels[1:]:
        onehot = onehot + s.astype(F32)
    ri = lax.broadcasted_iota(jnp.int32, (tm, tm), 0)
    ci = lax.broadcasted_iota(jnp.int32, (tm, tm), 1)
    strict = (ci < ri).astype(BF16)
    before = jnp.dot(strict, onehot.astype(BF16), preferred_element_type=F32)
    route = jnp.zeros((tm, LANES), F32)
    for k in range(TOP_K):
        rank = jnp.sum(jnp.where(sels[k], before, 0.0), axis=-1, keepdims=True)
        route = jnp.where(lane == float(ROUTE_E + k), top_i[k], route)
        route = jnp.where(lane == float(ROUTE_G + k), exps[k] / denom, route)
        route = jnp.where(lane == float(ROUTE_R + k), rank, route)
    route_ref[...] = route
    routet_ref[...] = route.T
    cnt_ref[...] = jnp.sum(onehot, axis=0, keepdims=True)


def _merge(o_f, o_b, pb, o_diff, h, gn, wa, wb, wo, g1, b1, rw, rb, dn_alpha):
    t, d = h.shape
    tm = _moe_tile(t)
    row = lambda c: pl.BlockSpec((tm, d), lambda i: (i, c))
    full = lambda shp: pl.BlockSpec(shp, lambda i: (0, 0))
    rwh = rw.astype(BF16)
    rwl = (rw - rwh.astype(F32)).astype(BF16)
    return pl.pallas_call(
        functools.partial(_merge_kernel, dn_alpha=dn_alpha),
        out_shape=[jax.ShapeDtypeStruct((t, d), F32),
                   jax.ShapeDtypeStruct((t, LANES), F32),
                   jax.ShapeDtypeStruct((LANES, t), F32),
                   jax.ShapeDtypeStruct((t // tm, 1, LANES), F32)],
        grid=(t // tm,),
        in_specs=[row(0), row(0), row(0), row(0), row(1), row(2), row(0),
                  full((1, d)), full((d, d)), full((d, d)), full((d, d)),
                  full((1, d)), full((1, d)), full((d, LANES)), full((d, LANES)), full((1, LANES))],
        out_specs=[pl.BlockSpec((tm, d), lambda i: (i, 0)),
                   pl.BlockSpec((tm, LANES), lambda i: (i, 0)),
                   pl.BlockSpec((LANES, tm), lambda i: (0, i)),
                   pl.BlockSpec((None, 1, LANES), lambda i: (i, 0, 0))],
        compiler_params=_cparams(("parallel",)),
        name="merge_ln1_router",
    )(o_f, o_b, pb, o_diff, pb, pb, h, gn, wa, wb, wo, g1, b1, rwh, rwl, rb)


def _moe_tile(t):
    return min(ROW_TILE, t)


def _moe_slots(tm):
    return tm * TOP_K + N_EXPERTS * MOE_SEG


def _piece_loops(j, lo_s, gb_s, nbig_s, nsmall_s, make_big, make_small):
    def per_expert(e, carry):
        idx = j * N_EXPERTS + e
        base, g, nbig = lo_s[idx], gb_s[idx], nbig_s[idx]

        def big_piece(p, c):
            off = p * MOE_BIG
            make_big(pl.multiple_of(base + off, MOE_SEG), pl.multiple_of(g + off, MOE_SEG)).start()
            return c
        lax.fori_loop(0, nbig, big_piece, 0)

        def small_piece(p, c):
            off = nbig * MOE_BIG + p * MOE_SEG
            make_small(pl.multiple_of(base + off, MOE_SEG), pl.multiple_of(g + off, MOE_SEG)).start()
            return c
        lax.fori_loop(0, nsmall_s[idx], small_piece, 0)
        return carry
    lax.fori_loop(0, N_EXPERTS, per_expert, 0)


def _wait_pieces(n, make_copy):
    def body(p, c):
        make_copy(0, 0).wait()
        return c
    lax.fori_loop(0, n, body, 0)


def _dispatch_kernel(lo_s, gb_s, nbig_s, nsmall_s, tbig_s, tsmall_s, ts_s, tnp_s, h_ref, routet_ref, locol_ref,
                     xb_ref, sorted_s, zero_s, sem_big, sem_small, zsem, *, nt):
    tm = h_ref.shape[0]
    n_slots = sorted_s.shape[1]
    j = pl.program_id(0)
    par = j % 2

    def seg_copy(pp, rows, sem):
        def make(local_row, global_row):
            return pltpu.make_async_copy(sorted_s.at[pp, pl.ds(local_row, rows)],
                                         xb_ref.at[pl.ds(global_row, rows)], sem.at[pp])
        return make

    big = lambda pp: seg_copy(pp, MOE_BIG, sem_big)
    small = lambda pp: seg_copy(pp, MOE_SEG, sem_small)

    blk_rows = zero_s.shape[0]

    def zero_tail(_, global_row):
        return pltpu.make_async_copy(zero_s.at[pl.ds(0, MOE_SEG)], xb_ref.at[pl.ds(global_row, MOE_SEG)],
                                     zsem.at[0])

    def zero_block(_, global_row):
        return pltpu.make_async_copy(zero_s, xb_ref.at[pl.ds(global_row, blk_rows)], zsem.at[1])

    @pl.when(j == 0)
    def _():
        zero_s[...] = jnp.zeros_like(zero_s)

        def per_region(e, carry):
            def per_piece(p, c):
                zero_tail(0, pl.multiple_of(ts_s[e] + p * MOE_SEG, MOE_SEG)).start()
                return c
            lax.fori_loop(0, tnp_s[e], per_piece, 0)
            return carry
        lax.fori_loop(0, N_EXPERTS, per_region, 0)

        def per_block(p, c):
            zero_block(0, pl.multiple_of(ts_s[N_EXPERTS] + p * blk_rows, blk_rows)).start()
            return c
        lax.fori_loop(0, tnp_s[N_EXPERTS], per_block, 0)

    routet = routet_ref[...]
    sub = lax.broadcasted_iota(jnp.int32, (LANES, tm), 0).astype(F32)
    slot_iota = lax.broadcasted_iota(jnp.int32, (n_slots, tm), 0).astype(F32)
    perm = jnp.zeros((n_slots, tm), F32)
    for k in range(TOP_K):
        e_row = routet[ROUTE_E + k:ROUTE_E + k + 1, :]
        lo_k = jnp.sum(jnp.where(sub == e_row, locol_ref[...], 0.0), axis=0, keepdims=True)
        slot = lo_k + routet[ROUTE_R + k:ROUTE_R + k + 1, :]
        perm = jnp.where(slot_iota == slot, 1.0, perm)
    sorted_s[par] = jnp.dot(perm.astype(BF16), h_ref[...].astype(BF16), preferred_element_type=F32)

    _piece_loops(j, lo_s, gb_s, nbig_s, nsmall_s, big(par), small(par))

    @pl.when(j > 0)
    def _():
        _wait_pieces(tbig_s[j - 1], big(1 - par))
        _wait_pieces(tsmall_s[j - 1], small(1 - par))

    @pl.when(j == nt - 1)
    def _():
        _wait_pieces(tbig_s[j], big(par))
        _wait_pieces(tsmall_s[j], small(par))
        _wait_pieces(tnp_s[N_EXPERTS + 1], zero_tail)
        _wait_pieces(tnp_s[N_EXPERTS], zero_block)


def _dispatch(tables, h1, routet, lo_col, n_rows, tb):
    t, d = h1.shape
    tm = _moe_tile(t)
    nt = t // tm
    n_slots = _moe_slots(tm)
    return pl.pallas_call(
        functools.partial(_dispatch_kernel, nt=nt),
        out_shape=jax.ShapeDtypeStruct((n_rows, d), F32),
        grid_spec=pltpu.PrefetchScalarGridSpec(
            num_scalar_prefetch=len(tables),
            grid=(nt,),
            in_specs=[
                pl.BlockSpec((tm, d), lambda i, *_: (i, 0)),
                pl.BlockSpec((LANES, tm), lambda i, *_: (0, i)),
                pl.BlockSpec((None, LANES, 1), lambda i, *_: (i, 0, 0)),
            ],
            out_specs=pl.BlockSpec(memory_space=pl.ANY),
            scratch_shapes=[pltpu.VMEM((2, n_slots, d), F32), pltpu.VMEM((tb, d), F32),
                            pltpu.SemaphoreType.DMA((2,)), pltpu.SemaphoreType.DMA((2,)),
                            pltpu.SemaphoreType.DMA((2,))],
        ),
        compiler_params=_cparams(("arbitrary",)),
        name="moe_dispatch",
    )(*tables, h1, routet, lo_col)


def _experts_kernel(be_ref, nu_ref, nx_ref, x_ref, wg_hbm, bg_ref, wu_hbm, bu_ref, wd_hbm, bd_ref, y_ref,
                    stage_s, wg_s, wu_s, wd_s, sem):
    i = pl.program_id(0)

    def fetch(e):
        return [pltpu.make_async_copy(w.at[e], stage_s.at[n], sem.at[n])
                for n, w in enumerate((wg_hbm, wu_hbm, wd_hbm))]

    @pl.when(i >= nu_ref[0])
    def _():
        y_ref[...] = jnp.zeros_like(y_ref)

    @pl.when(i < nu_ref[0])
    def _():
        e = be_ref[i]
        prev = be_ref[jnp.maximum(i - 1, 0)]

        @pl.when(i == 0)
        def _():
            for c in fetch(e):
                c.start()

        @pl.when((i == 0) | (e != prev))
        def _():
            for c in fetch(e):
                c.wait()
            wg_s[...] = stage_s[0].astype(BF16)
            wu_s[...] = stage_s[1].astype(BF16)
            wd_s[...] = stage_s[2].astype(BF16)

            @pl.when(nx_ref[i] != e)
            def _():
                for c in fetch(nx_ref[i]):
                    c.start()

        x = x_ref[...].astype(BF16)
        g = jnp.minimum(jnp.dot(x, wg_s[...], preferred_element_type=F32) + bg_ref[...], SWIGLU_LIMIT)
        u = jnp.clip(jnp.dot(x, wu_s[...], preferred_element_type=F32) + bu_ref[...],
                     -SWIGLU_LIMIT, SWIGLU_LIMIT)
        a = g * _sigmoid(SWIGLU_ALPHA * g) * (u + 1.0)
        y_ref[...] = jnp.dot(a.astype(BF16), wd_s[...], preferred_element_type=F32) + bd_ref[...]


def _experts(blk_e, n_used, blk_next, xb, wg, bg, wu, bu, wd, bd, tb):
    p, d = xb.shape
    ne, _, dff = wg.shape
    assert d == dff, (d, dff)
    nb = p // tb
    rowblk = lambda i, be, nu, nx: (jnp.maximum(jnp.minimum(i, nu[0] - 1), 0), 0)
    bspec = lambda n: pl.BlockSpec((None, 1, n), lambda i, be, nu, nx: (be[i], 0, 0))
    hbm = pl.BlockSpec(memory_space=pl.ANY)
    return pl.pallas_call(
        _experts_kernel,
        out_shape=jax.ShapeDtypeStruct((p, d), F32),
        grid_spec=pltpu.PrefetchScalarGridSpec(
            num_scalar_prefetch=3,
            grid=(nb,),
            in_specs=[pl.BlockSpec((tb, d), rowblk), hbm, bspec(dff), hbm, bspec(dff), hbm, bspec(d)],
            out_specs=pl.BlockSpec((tb, d), lambda i, be, nu, nx: (i, 0)),
            scratch_shapes=[pltpu.VMEM((3, d, dff), F32), pltpu.VMEM((d, dff), BF16),
                            pltpu.VMEM((d, dff), BF16), pltpu.VMEM((dff, d), BF16),
                            pltpu.SemaphoreType.DMA((3,))],
        ),
        compiler_params=_cparams(("arbitrary",)),
        name="moe_experts",
    )(blk_e, n_used, blk_next, xb, wg, bg.reshape(ne, 1, dff), wu, bu.reshape(ne, 1, dff), wd,
      bd.reshape(ne, 1, d))


def _combine_kernel(lo_s, gb_s, nbig_s, nsmall_s, tbig_s, tsmall_s, yb_ref, h_ref, route_ref, lorow_ref,
                    g_ref, b_ref, o_ref, ybuf, sem_big, sem_small, *, dn_alpha, nt):
    tm = h_ref.shape[0]
    n_slots = ybuf.shape[1]
    j = pl.program_id(0)
    par = j % 2

    def seg_copy(pp, rows, sem):
        def make(local_row, global_row):
            return pltpu.make_async_copy(yb_ref.at[pl.ds(global_row, rows)],
                                         ybuf.at[pp, pl.ds(local_row, rows)], sem.at[pp])
        return make

    big = lambda pp: seg_copy(pp, MOE_BIG, sem_big)
    small = lambda pp: seg_copy(pp, MOE_SEG, sem_small)

    @pl.when(j == 0)
    def _():
        ybuf[...] = jnp.zeros_like(ybuf)
        _piece_loops(j, lo_s, gb_s, nbig_s, nsmall_s, big(0), small(0))

    @pl.when(j + 1 < nt)
    def _():
        _piece_loops(j + 1, lo_s, gb_s, nbig_s, nsmall_s, big(1 - par), small(1 - par))

    _wait_pieces(tbig_s[j], big(par))
    _wait_pieces(tsmall_s[j], small(par))

    route = route_ref[...]
    lane = lax.broadcasted_iota(jnp.int32, (tm, LANES), 1).astype(F32)
    slot_iota = lax.broadcasted_iota(jnp.int32, (tm, n_slots), 1).astype(F32)
    wgt = jnp.zeros((tm, n_slots), F32)
    for k in range(TOP_K):
        e_col = route[:, ROUTE_E + k:ROUTE_E + k + 1]
        lo_k = jnp.sum(jnp.where(lane == e_col, lorow_ref[...], 0.0), axis=-1, keepdims=True)
        slot = lo_k + route[:, ROUTE_R + k:ROUTE_R + k + 1]
        wgt = jnp.where(slot_iota == slot, route[:, ROUTE_G + k:ROUTE_G + k + 1], wgt)
    y = ybuf[par]
    wh = wgt.astype(BF16)
    wl = (wgt - wh.astype(F32)).astype(BF16)
    yh = y.astype(BF16)
    yl = (y - yh.astype(F32)).astype(BF16)
    ffn = (jnp.dot(wh, yh, preferred_element_type=F32) + jnp.dot(wh, yl, preferred_element_type=F32)
           + jnp.dot(wl, yh, preferred_element_type=F32))
    o_ref[...] = _layer_norm(dn_alpha * h_ref[...] + ffn, g_ref[...], b_ref[...])


def _combine(tables, yb, h1, route, lo_row, g2, b2, dn_alpha):
    t, d = h1.shape
    tm = _moe_tile(t)
    nt = t // tm
    return pl.pallas_call(
        functools.partial(_combine_kernel, dn_alpha=dn_alpha, nt=nt),
        out_shape=jax.ShapeDtypeStruct((t, d), F32),
        grid_spec=pltpu.PrefetchScalarGridSpec(
            num_scalar_prefetch=len(tables),
            grid=(nt,),
            in_specs=[
                pl.BlockSpec(memory_space=pl.ANY),
                pl.BlockSpec((tm, d), lambda i, *_: (i, 0)),
                pl.BlockSpec((tm, LANES), lambda i, *_: (i, 0)),
                pl.BlockSpec((None, 1, LANES), lambda i, *_: (i, 0, 0)),
                pl.BlockSpec((1, d), lambda i, *_: (0, 0)),
                pl.BlockSpec((1, d), lambda i, *_: (0, 0)),
            ],
            out_specs=pl.BlockSpec((tm, d), lambda i, *_: (i, 0)),
            scratch_shapes=[pltpu.VMEM((2, _moe_slots(tm), d), F32), pltpu.SemaphoreType.DMA((2,)),
                            pltpu.SemaphoreType.DMA((2,))],
        ),
        compiler_params=_cparams(("arbitrary",)),
        name="moe_combine_ln2",
    )(*tables, yb, h1, route, lo_row, g2.reshape(1, d), b2.reshape(1, d))


def _moe_tables(counts, tm, tb, n_rows):
    seg = ((counts + MOE_SEG - 1) // MOE_SEG) * MOE_SEG
    lo = jnp.cumsum(seg, axis=1) - seg
    rows_e = jnp.sum(seg, axis=0)
    region = ((rows_e + tb - 1) // tb) * tb
    pend = jnp.cumsum(region)
    pstart = pend - region
    gbase = pstart[None, :] + jnp.cumsum(seg, axis=0) - seg
    n_big = seg // MOE_BIG
    n_small = (seg - n_big * MOE_BIG) // MOE_SEG
    tail_start = jnp.concatenate([pstart + rows_e, pend[-1:]])
    tail_small = (region - rows_e) // MOE_SEG
    tail_pieces = jnp.concatenate([tail_small, (n_rows - pend[-1:]) // tb, jnp.sum(tail_small, keepdims=True)])
    i32 = lambda a: a.reshape(-1).astype(jnp.int32)
    seg_tables = (i32(lo), i32(gbase), i32(n_big), i32(n_small),
                  i32(jnp.sum(n_big, axis=1)), i32(jnp.sum(n_small, axis=1)))
    n_used = (pend[-1] // tb).astype(jnp.int32)
    blk = jnp.minimum(jnp.arange(n_rows // tb, dtype=jnp.int32), n_used - 1)
    blk_e = jnp.minimum(jnp.sum((pend[None, :] <= (blk * tb)[:, None]).astype(jnp.int32), axis=1),
                        N_EXPERTS - 1)
    ids = jnp.arange(N_EXPERTS, dtype=jnp.int32)
    later = jnp.where((region[None, :] > 0) & (ids[None, :] > ids[:, None]), ids[None, :], N_EXPERTS)
    nxt_e = jnp.min(later, axis=1)
    nxt_e = jnp.where(nxt_e < N_EXPERTS, nxt_e, ids).astype(jnp.int32)
    lo_f = jnp.pad(lo.astype(F32), ((0, 0), (0, LANES - N_EXPERTS)))
    return (seg_tables, (i32(tail_start), i32(tail_pieces)), (blk_e, n_used.reshape(1), nxt_e[blk_e]), lo_f)


def kernel(x, ln0_g, ln0_b, w_in, gla_wf_fwd, gla_bf_fwd, gla_wf_bwd, gla_bf_bwd, gla_norm_g, diff_lq1, diff_lk1, diff_lq2, diff_lk2, diff_norm_g, w_br_gla, w_br_diff, w_out, ln1_g, ln1_b, router_w, router_b, exp_w_gate, exp_b_gate, exp_w_up, exp_b_up, exp_w_down, exp_b_down, ln2_g, ln2_b):
    batch, seq, d = x.shape
    depth = w_in.shape[0]
    t = batch * seq
    dn_alpha = (2 * depth) ** 0.25
    gk, gv = GLA_HEADS * GLA_DK, GLA_HEADS * GLA_DV
    dq = DIFF_HEADS * 2 * DIFF_DH
    splits = (gk, gk, gv, gv, GLA_RANK, GLA_RANK, dq, dq, dq, d, d)
    cuts = [0]
    for s in splits:
        cuts.append(cuts[-1] + s)
    col = lambda w, i: w[:, cuts[i]:cuts[i + 1]]
    tabs = _rope_tables(seq)
    tm = _moe_tile(t)
    tb = _moe_tile(t)
    n_rows = t * TOP_K + (t // tm) * N_EXPERTS * (MOE_SEG - 1) + N_EXPERTS * (tb - 1)
    n_rows = ((n_rows + tb - 1) // tb) * tb

    assert depth == 1, depth
    cur = x.reshape(t, d)
    g_in, b_in = ln0_g, ln0_b
    for l in range(depth):
        lam_init = 0.8 - 0.6 * math.exp(-0.3 * l)
        w = w_in[l]
        wa = jnp.concatenate([col(w, i).astype(BF16) for i in (0, 1, 2, 6, 7, 8)], axis=1)
        wb = jnp.concatenate([col(w, i).astype(BF16) for i in (3, 9, 10, 4, 5)]
                             + [jnp.zeros((d, LANES - 2 * GLA_RANK), BF16)], axis=1)
        q_lo = 2 * gk + gv
        k_cols = (q_lo + dq, q_lo + 2 * dq)
        pa, pb, h, kgt, kdt = _ln_proj(cur, g_in, b_in, wa, wb, tabs, seq, PROJ_COLS_BF16, PROJ_COLS_F32,
                                       (q_lo, q_lo + dq), k_cols, DIFF_DH ** -0.5 * math.log2(math.e),
                                       ((gk, 2 * gk), k_cols))

        lrt = pb[:, 3 * d:3 * d + 2 * GLA_RANK].reshape(batch, seq, 2 * GLA_RANK).transpose(0, 2, 1)
        zr = jnp.zeros((GLA_RANK, gk), F32)
        pad = jnp.zeros((LANES - 2 * GLA_RANK, gk), F32)
        wf_f = jnp.concatenate([gla_wf_fwd[l], zr, pad], axis=0)
        wf_b = jnp.concatenate([zr, gla_wf_bwd[l], pad], axis=0)
        o_f, o_b = _gla(
            pa, kgt, pb, lrt,
            (wf_f, wf_f[:2 * GLA_RANK].T, gla_bf_fwd[l].reshape(1, gk), gla_bf_fwd[l].reshape(gk, 1)),
            (wf_b, wf_b[:2 * GLA_RANK].T, gla_bf_bwd[l].reshape(1, gk), gla_bf_bwd[l].reshape(gk, 1)),
            batch, seq)

        o_diff = _diff_attn(pa, kdt, diff_lq1[l], diff_lk1[l], diff_lq2[l], diff_lk2[l], diff_norm_g[l],
                            batch, seq, lam_init)

        rw = jnp.concatenate([router_w[l], jnp.zeros((d, LANES - N_EXPERTS), F32)], axis=1)
        rb = jnp.concatenate([router_b[l], jnp.full((LANES - N_EXPERTS,), -jnp.inf, F32)]).reshape(1, LANES)
        h1, route, routet, counts = _merge(
            o_f, o_b, pb, o_diff, h, jnp.tile(gla_norm_g[l], GLA_HEADS).reshape(1, gv),
            w_br_gla[l].astype(BF16), w_br_diff[l].astype(BF16), w_out[l].astype(BF16),
            ln1_g[l].reshape(1, d), ln1_b[l].reshape(1, d), rw, rb, dn_alpha)

        seg_tables, tail_tables, blk_tables, lo_f = _moe_tables(
            counts[:, 0, :N_EXPERTS].astype(jnp.int32), tm, tb, n_rows)
        xb = _dispatch(seg_tables + tail_tables, h1, routet, lo_f[:, :, None], n_rows, tb)
        yb = _experts(*blk_tables, xb, exp_w_gate[l], exp_b_gate[l], exp_w_up[l], exp_b_up[l],
                      exp_w_down[l], exp_b_down[l], tb)
        cur = _combine(seg_tables, yb, h1, route, lo_f[:, None, :], ln2_g[l], ln2_b[l], dn_alpha)
    return cur.reshape(batch, seq, d)
```
